```python
import math
import jax, jax.numpy as jnp
from jax import lax
import numpy as np

D_MODEL = 2048
BATCH = 1
SEQ = 8192
DEPTH = 1

N_HEADS = 16
HEAD_DIM = 128
ATT_WIDTH = N_HEADS * HEAD_DIM
MOBA_BLOCK = 256
MOBA_TOPK = 3
QUERY_CHUNK = 32
LRU_WIDTH = D_MODEL
LRU_BLOCKS = 16
LRU_BLOCK_W = LRU_WIDTH // LRU_BLOCKS
CONV_WIDTH = 4
LRU_C = 8.0
D_FF = int(math.ceil(8 * D_MODEL / 3 / 256)) * 256
IN_COLS = 3 * ATT_WIDTH + 2 * LRU_WIDTH + 2 * D_MODEL
SPLIT_POINTS = (ATT_WIDTH, 2 * ATT_WIDTH, 3 * ATT_WIDTH,
                3 * ATT_WIDTH + LRU_WIDTH, 3 * ATT_WIDTH + 2 * LRU_WIDTH,
                3 * ATT_WIDTH + 2 * LRU_WIDTH + D_MODEL)
EPS = 1e-6
NEG_INF = -1e30

kernel_name = "moba_rglru_gated_hybrid"


def rms_norm(x, w):
    xf = x.astype(jnp.float32)
    y = xf * lax.rsqrt(jnp.mean(xf * xf, axis=-1, keepdims=True) + EPS)
    return (y * w.astype(jnp.float32)).astype(x.dtype)


def alibi_slopes(n_heads):
    h = jnp.arange(1, n_heads + 1, dtype=jnp.float32)
    return jnp.exp2(-8.0 * h / n_heads)


def moba_attention(q, k, v):
    B, S, H, Dh = q.shape
    n_blk = -(-S // MOBA_BLOCK)
    s_pad = n_blk * MOBA_BLOCK
    topk = min(MOBA_TOPK, n_blk)
    pad = ((0, 0), (0, s_pad - S), (0, 0), (0, 0))
    kb = jnp.pad(k, pad).reshape(B, n_blk, MOBA_BLOCK, H, Dh).transpose(0, 3, 1, 2, 4)
    vb = jnp.pad(v, pad).reshape(B, n_blk, MOBA_BLOCK, H, Dh).transpose(0, 3, 1, 2, 4)
    k_mean = jnp.mean(kb.astype(jnp.float32), axis=3).astype(q.dtype)
    slopes = alibi_slopes(H)
    scale = HEAD_DIM ** -0.5
    b_idx = jnp.arange(B)[:, None, None, None]
    h_idx = jnp.arange(H)[None, None, :, None]
    blk_offsets = jnp.arange(MOBA_BLOCK)
    n_chunks = S // QUERY_CHUNK

    def one_chunk(c):
        t0 = c * QUERY_CHUNK
        blk = t0 // MOBA_BLOCK
        qc = lax.dynamic_slice_in_dim(q, t0, QUERY_CHUNK, axis=1)
        pos_q = t0 + jnp.arange(QUERY_CHUNK)
        gate = jnp.einsum('bqhd,bhnd->bqhn', qc, k_mean).astype(jnp.float32)
        gate = jnp.where(jnp.arange(n_blk) < blk, gate, NEG_INF)
        _, sel = lax.top_k(gate, topk)
        valid = jnp.arange(topk) < blk
        k_sel = kb[b_idx, h_idx, sel]
        v_sel = vb[b_idx, h_idx, sel]
        s_sel = jnp.einsum('bqhd,bqhjsd->bqhjs', qc, k_sel).astype(jnp.float32) * scale
        key_pos = sel[..., None] * MOBA_BLOCK + blk_offsets
        dist_sel = (pos_q[None, :, None, None, None] - key_pos).astype(jnp.float32)
        s_sel = s_sel - slopes[None, None, :, None, None] * dist_sel
        s_sel = jnp.where(valid[:, None], s_sel, NEG_INF)
        k_own = lax.dynamic_index_in_dim(kb, blk, axis=2, keepdims=False)
        v_own = lax.dynamic_index_in_dim(vb, blk, axis=2, keepdims=False)
        s_own = jnp.einsum('bqhd,bhsd->bqhs', qc, k_own).astype(jnp.float32) * scale
        dist_own = pos_q[:, None] - (blk * MOBA_BLOCK + blk_offsets)[None, :]
        s_own = jnp.where(dist_own[None, :, None, :] >= 0,
                          s_own - slopes[None, None, :, None] * dist_own.astype(jnp.float32)[None, :, None, :],
                          NEG_INF)
        s_all = jnp.concatenate([s_sel.reshape(B, QUERY_CHUNK, H, topk * MOBA_BLOCK), s_own], axis=-1)
        p = jax.nn.softmax(s_all, axis=-1).astype(v.dtype)
        p_sel = p[..., :topk * MOBA_BLOCK].reshape(B, QUERY_CHUNK, H, topk, MOBA_BLOCK)
        p_own = p[..., topk * MOBA_BLOCK:]
        return (jnp.einsum('bqhjs,bqhjsd->bqhd', p_sel, v_sel)
                + jnp.einsum('bqhs,bhsd->bqhd', p_own, v_own))

    out = lax.map(one_chunk, jnp.arange(n_chunks))
    return out.transpose(1, 0, 2, 3, 4).reshape(B, S, H * Dh)


def _lru_combine(left, right):
    a_l, b_l = left
    a_r, b_r = right
    return a_l * a_r, a_r * b_l + b_r


def rglru_branch(xr, yr, conv_w, conv_b, w_rg_a, b_rg_a, w_rg_x, b_rg_x, lru_lambda):
    B, S, W = xr.shape
    xp = jnp.pad(xr, ((0, 0), (CONV_WIDTH - 1, 0), (0, 0)))
    u = conv_b
    for tap in range(CONV_WIDTH):
        u = u + xp[:, tap:tap + S] * conv_w[tap]
    ub = u.reshape(B, S, LRU_BLOCKS, LRU_BLOCK_W)
    r = jax.nn.sigmoid((jnp.einsum('bsnc,ncd->bsnd', ub, w_rg_a).reshape(B, S, W) + b_rg_a).astype(jnp.float32))
    i = jax.nn.sigmoid((jnp.einsum('bsnc,ncd->bsnd', ub, w_rg_x).reshape(B, S, W) + b_rg_x).astype(jnp.float32))
    log_a = -LRU_C * r * jax.nn.softplus(-lru_lambda.astype(jnp.float32))
    a = jnp.exp(log_a)
    mult = jnp.sqrt(-jnp.expm1(2.0 * log_a))
    bx = mult * i * u.astype(jnp.float32)
    _, h = lax.associative_scan(_lru_combine, (a, bx), axis=1)
    return h.astype(xr.dtype) * jax.nn.gelu(yr)


def hybrid_layer(x, norm1_w, w_in, q_norm_w, k_norm_w, conv_w, conv_b, w_rg_a, b_rg_a,
                 w_rg_x, b_rg_x, lru_lambda, w_proj_attn, w_proj_lru, w_out,
                 norm2_w, w_ffn_gate, w_ffn_up, w_ffn_down):
    B, S, _ = x.shape
    h = rms_norm(x, norm1_w)
    proj = h @ w_in
    q, k, v, xr, yr, g_att, g_lru = jnp.split(proj, SPLIT_POINTS, axis=-1)
    q = rms_norm(q.reshape(B, S, N_HEADS, HEAD_DIM), q_norm_w)
    k = rms_norm(k.reshape(B, S, N_HEADS, HEAD_DIM), k_norm_w)
    v = v.reshape(B, S, N_HEADS, HEAD_DIM)
    att = moba_attention(q, k, v)
    lru = rglru_branch(xr, yr, conv_w, conv_b, w_rg_a, b_rg_a, w_rg_x, b_rg_x, lru_lambda)
    merged = (jax.nn.sigmoid(g_att) * (att @ w_proj_attn)
              + jax.nn.sigmoid(g_lru) * (lru @ w_proj_lru))
    x = x + merged @ w_out
    h2 = rms_norm(x, norm2_w)
    x = x + (jax.nn.silu(h2 @ w_ffn_gate) * (h2 @ w_ffn_up)) @ w_ffn_down
    return x


def setup_inputs(seed: int = 0) -> dict:
    key = jax.random.key(seed)
    ks = jax.random.split(key, 20)
    L = DEPTH
    nrm = lambda k, shape, fan_in: jax.random.normal(k, shape, jnp.float32) * fan_in ** -0.5
    u = jax.random.uniform(ks[11], (L, LRU_WIDTH), jnp.float32, minval=0.9, maxval=0.999)
    a0 = u ** (1.0 / LRU_C)
    lam = jnp.log(a0) - jnp.log1p(-a0)
    return {
        "x": jax.random.normal(ks[0], (BATCH, SEQ, D_MODEL), jnp.float32),
        "norm1_w": 1.0 + 0.05 * jax.random.normal(ks[1], (L, D_MODEL), jnp.float32),
        "w_in": nrm(ks[2], (L, D_MODEL, IN_COLS), D_MODEL),
        "q_norm_w": 1.0 + 0.05 * jax.random.normal(ks[3], (L, HEAD_DIM), jnp.float32),
        "k_norm_w": 1.0 + 0.05 * jax.random.normal(ks[4], (L, HEAD_DIM), jnp.float32),
        "conv_w": nrm(ks[5], (L, CONV_WIDTH, LRU_WIDTH), CONV_WIDTH),
        "conv_b": 0.01 * jax.random.normal(ks[6], (L, LRU_WIDTH), jnp.float32),
        "w_rg_a": nrm(ks[7], (L, LRU_BLOCKS, LRU_BLOCK_W, LRU_BLOCK_W), LRU_BLOCK_W),
        "b_rg_a": 0.01 * jax.random.normal(ks[8], (L, LRU_WIDTH), jnp.float32),
        "w_rg_x": nrm(ks[9], (L, LRU_BLOCKS, LRU_BLOCK_W, LRU_BLOCK_W), LRU_BLOCK_W),
        "b_rg_x": 0.01 * jax.random.normal(ks[10], (L, LRU_WIDTH), jnp.float32),
        "lru_lambda": lam,
        "w_proj_attn": nrm(ks[12], (L, ATT_WIDTH, D_MODEL), ATT_WIDTH),
        "w_proj_lru": nrm(ks[13], (L, LRU_WIDTH, D_MODEL), LRU_WIDTH),
        "w_out": nrm(ks[14], (L, D_MODEL, D_MODEL), D_MODEL),
        "norm2_w": 1.0 + 0.05 * jax.random.normal(ks[15], (L, D_MODEL), jnp.float32),
        "w_ffn_gate": nrm(ks[16], (L, D_MODEL, D_FF), D_MODEL),
        "w_ffn_up": nrm(ks[17], (L, D_MODEL, D_FF), D_MODEL),
        "w_ffn_down": nrm(ks[18], (L, D_FF, D_MODEL), D_FF),
    }


def reference(x, norm1_w, w_in, q_norm_w, k_norm_w, conv_w, conv_b, w_rg_a, b_rg_a,
              w_rg_x, b_rg_x, lru_lambda, w_proj_attn, w_proj_lru, w_out,
              norm2_w, w_ffn_gate, w_ffn_up, w_ffn_down):
    for layer in range(DEPTH):
        x = hybrid_layer(x, norm1_w[layer], w_in[layer], q_norm_w[layer], k_norm_w[layer],
                         conv_w[layer], conv_b[layer], w_rg_a[layer], b_rg_a[layer],
                         w_rg_x[layer], b_rg_x[layer], lru_lambda[layer],
                         w_proj_attn[layer], w_proj_lru[layer], w_out[layer],
                         norm2_w[layer], w_ffn_gate[layer], w_ffn_up[layer], w_ffn_down[layer])
    return x
```

```python
import functools

import jax
import jax.numpy as jnp
from jax import lax
from jax.experimental import pallas as pl
from jax.experimental.pallas import tpu as pltpu

F32 = jnp.float32
BF16 = jnp.bfloat16

N_HEADS = 16
HEAD_DIM = 128
MOBA_BLOCK = 256
MOBA_TOPK = 3
LRU_BLOCK_W = 128
CONV_WIDTH = 4
LRU_C = 8.0
EPS = 1e-6
NEG_INF = -1e30

V7X_VMEM_BYTES = 64 * 1024 * 1024
VMEM_LIMIT = 56 * 1024 * 1024


def _params(semantics):
    return pltpu.CompilerParams(dimension_semantics=semantics, vmem_limit_bytes=VMEM_LIMIT)


def _rmsnorm_kernel(x_ref, w_ref, o_ref):
    x = x_ref[...]
    y = x * lax.rsqrt(jnp.mean(x * x, axis=-1, keepdims=True) + EPS)
    o_ref[...] = (y * w_ref[...]).astype(o_ref.dtype)


def _rmsnorm(x, w, tm=512):
    m, d = x.shape
    return pl.pallas_call(
        _rmsnorm_kernel,
        grid=(m // tm,),
        in_specs=[pl.BlockSpec((tm, d), lambda i: (i, 0)),
                  pl.BlockSpec((1, d), lambda i: (0, 0))],
        out_specs=pl.BlockSpec((tm, d), lambda i: (i, 0)),
        out_shape=jax.ShapeDtypeStruct((m, d), BF16),
        compiler_params=_params(("parallel",)),
        name="rmsnorm1",
    )(x, w.reshape(1, d))


def _qkv_kernel(h_ref, w_ref, qw_ref, kw_ref, o_ref, *, qk_tiles):
    j = pl.program_id(1)
    acc = jnp.dot(h_ref[...], w_ref[...], preferred_element_type=F32)
    bn = acc.shape[1]

    @pl.when(j < qk_tiles)
    def _():
        nw = jnp.where(j < qk_tiles // 2, qw_ref[...], kw_ref[...])
        for hh in range(bn // HEAD_DIM):
            a = acc[:, hh * HEAD_DIM:(hh + 1) * HEAD_DIM]
            y = a * lax.rsqrt(jnp.mean(a * a, axis=-1, keepdims=True) + EPS)
            o_ref[:, hh * HEAD_DIM:(hh + 1) * HEAD_DIM] = (y * nw).astype(o_ref.dtype)

    @pl.when(j >= qk_tiles)
    def _():
        o_ref[...] = acc.astype(o_ref.dtype)


def _qkv_proj(h, w_in, q_norm_w, k_norm_w, n_cols, bm=1024, bn=1024):
    m, k = h.shape
    return pl.pallas_call(
        functools.partial(_qkv_kernel, qk_tiles=(2 * N_HEADS * HEAD_DIM) // bn),
        grid=(m // bm, n_cols // bn),
        in_specs=[pl.BlockSpec((bm, k), lambda i, j: (i, 0)),
                  pl.BlockSpec((k, bn), lambda i, j: (0, j)),
                  pl.BlockSpec((1, HEAD_DIM), lambda i, j: (0, 0)),
                  pl.BlockSpec((1, HEAD_DIM), lambda i, j: (0, 0))],
        out_specs=pl.BlockSpec((bm, bn), lambda i, j: (i, j)),
        out_shape=jax.ShapeDtypeStruct((m, n_cols), BF16),
        compiler_params=_params(("parallel", "arbitrary")),
        name="qkv_proj",
    )(h, w_in, q_norm_w.reshape(1, HEAD_DIM), k_norm_w.reshape(1, HEAD_DIM))


def _matmul_kernel(a_ref, w_ref, o_ref):
    o_ref[...] = jnp.dot(a_ref[...], w_ref[...], preferred_element_type=F32).astype(o_ref.dtype)


def _rest_proj(h, w_in, col0, n_cols, bm=1024, bn=1024):
    m, k = h.shape
    jb = col0 // bn
    return pl.pallas_call(
        _matmul_kernel,
        grid=(m // bm, n_cols // bn),
        in_specs=[pl.BlockSpec((bm, k), lambda i, j: (i, 0)),
                  pl.BlockSpec((k, bn), lambda i, j: (0, jb + j))],
        out_specs=pl.BlockSpec((bm, bn), lambda i, j: (i, j)),
        out_shape=jax.ShapeDtypeStruct((m, n_cols), F32),
        compiler_params=_params(("parallel", "arbitrary")),
        name="rest_proj",
    )(h, w_in)


def _attn_kernel(slope_ref, q_ref, k_ref, v_ref, o_ref,
                 kmean_ref, vt_ref, bias_ref, dmat_ref, acc_ref, *, n_blk):
    blk = MOBA_BLOCK
    i = pl.program_id(1)
    slope = slope_ref[...]
    scale = HEAD_DIM ** -0.5
    kk = lax.broadcasted_iota(jnp.int32, (blk, blk), 0)
    qq = lax.broadcasted_iota(jnp.int32, (blk, blk), 1)

    @pl.when(i == 0)
    def _per_head_setup():
        def setup(jj, c):
            r0 = pl.multiple_of(jj * blk, blk)
            kb = k_ref[pl.ds(r0, blk), :].astype(F32)
            kmean_ref[pl.ds(jj, 1), :] = jnp.sum(kb, axis=0, keepdims=True) * (1.0 / blk)
            vt_ref[jj] = v_ref[pl.ds(r0, blk), :].astype(F32).T.astype(BF16)
            return c
        lax.fori_loop(0, n_blk, setup, 0)
        dmat_ref[...] = slope * (qq - kk).astype(F32)

    qt = q_ref[...].astype(F32).T.astype(BF16)

    gate = jnp.dot(kmean_ref[...].astype(BF16), qt, preferred_element_type=F32)
    row = lax.broadcasted_iota(jnp.int32, (n_blk, blk), 0)
    g = jnp.where(row < i, gate, NEG_INF)
    bias = jnp.full((n_blk, blk), NEG_INF, F32)
    for r in range(MOBA_TOPK):
        mx = jnp.max(g, axis=0, keepdims=True)
        idx = jnp.min(jnp.where(g == mx, row, n_blk), axis=0, keepdims=True)
        pick = row == idx
        bias = jnp.where(pick, jnp.where(i > r, 0.0, NEG_INF), bias)
        g = jnp.where(pick, -jnp.inf, g)
    bias_ref[...] = bias

    r_i = pl.multiple_of(i * blk, blk)
    s = jnp.dot(k_ref[pl.ds(r_i, blk), :], qt, preferred_element_type=F32)
    t = jnp.where(kk <= qq, s * scale - dmat_ref[...], NEG_INF)
    m0 = jnp.max(t, axis=0, keepdims=True)
    p = jnp.exp(t - m0)
    l0 = jnp.sum(p, axis=0, keepdims=True)
    acc_ref[...] = jnp.dot(vt_ref[i], p.astype(BF16), preferred_element_type=F32)

    def body(j, carry):
        m, l = carry
        r_j = pl.multiple_of(j * blk, blk)
        s = jnp.dot(k_ref[pl.ds(r_j, blk), :], qt, preferred_element_type=F32)
        rb = bias_ref[pl.ds(j, 1), :] - slope * ((i - j) * blk).astype(F32)
        t = s * scale - dmat_ref[...] + rb
        m_new = jnp.maximum(m, jnp.max(t, axis=0, keepdims=True))
        alpha = jnp.exp(m - m_new)
        p = jnp.exp(t - m_new)
        l = alpha * l + jnp.sum(p, axis=0, keepdims=True)
        acc_ref[...] = alpha * acc_ref[...] + jnp.dot(vt_ref[j], p.astype(BF16),
                                                      preferred_element_type=F32)
        return m_new, l

    _, l = lax.fori_loop(0, i, body, (m0, l0))
    o_ref[...] = (acc_ref[...] / l).T.astype(o_ref.dtype)


def _moba_attention(qkv, slopes):
    s = qkv.shape[0]
    blk = MOBA_BLOCK
    n_blk = s // blk
    return pl.pallas_call(
        functools.partial(_attn_kernel, n_blk=n_blk),
        grid=(N_HEADS, n_blk),
        in_specs=[pl.BlockSpec((None, 1, blk), lambda h, i: (h, 0, 0)),
                  pl.BlockSpec((blk, HEAD_DIM), lambda h, i: (i, h)),
                  pl.BlockSpec((s, HEAD_DIM), lambda h, i: (0, N_HEADS + h)),
                  pl.BlockSpec((s, HEAD_DIM), lambda h, i: (0, 2 * N_HEADS + h))],
        out_specs=pl.BlockSpec((blk, HEAD_DIM), lambda h, i: (i, h)),
        out_shape=jax.ShapeDtypeStruct((s, N_HEADS * HEAD_DIM), BF16),
        scratch_shapes=[pltpu.VMEM((n_blk, HEAD_DIM), F32),
                        pltpu.VMEM((n_blk, HEAD_DIM, blk), BF16),
                        pltpu.VMEM((n_blk, blk), F32),
                        pltpu.VMEM((blk, blk), F32),
                        pltpu.VMEM((HEAD_DIM, blk), F32)],
        compiler_params=_params(("parallel", "arbitrary")),
        name="moba_attention",
    )(slopes, qkv, qkv, qkv)


def _lru_kernel(xr_ref, yr_ref, cw_ref, cb_ref, wa_ref, ba_ref, wx_ref, bx_ref, lam_ref, o_ref,
                xbuf_ref, a_ref, b_ref, h_ref, *, ts, tc):
    t = pl.program_id(1)
    pad = 8

    @pl.when(t == 0)
    def _():
        xbuf_ref[0:pad, :] = jnp.zeros((pad, tc), F32)
        h_ref[...] = jnp.zeros_like(h_ref)

    xbuf_ref[pad:pad + ts, :] = xr_ref[...]
    cw = cw_ref[...]
    u = cb_ref[...]
    for tap in range(CONV_WIDTH):
        off = pad - (CONV_WIDTH - 1) + tap
        u = u + xbuf_ref[off:off + ts, :] * cw[tap:tap + 1, :]
    xbuf_ref[0:pad, :] = xbuf_ref[ts:ts + pad, :]

    ub = u.astype(BF16)
    ga, gx = [], []
    for n in range(tc // LRU_BLOCK_W):
        un = ub[:, n * LRU_BLOCK_W:(n + 1) * LRU_BLOCK_W]
        ga.append(jnp.dot(un, wa_ref[n], preferred_element_type=F32))
        gx.append(jnp.dot(un, wx_ref[n], preferred_element_type=F32))
    r = jax.nn.sigmoid(jnp.concatenate(ga, axis=1) + ba_ref[...])
    ig = jax.nn.sigmoid(jnp.concatenate(gx, axis=1) + bx_ref[...])
    log_a = -LRU_C * r * jax.nn.softplus(-lam_ref[...])
    a = jnp.exp(log_a)
    a_ref[...] = a
    one_minus_a2 = -jnp.tanh(log_a) * (a * a + 1.0)
    b_ref[...] = jnp.sqrt(one_minus_a2) * ig * u

    row = lax.broadcasted_iota(jnp.int32, (8, tc), 0)

    def group(g, hprev):
        r0 = pl.multiple_of(g * 8, 8)
        av = a_ref[pl.ds(r0, 8), :]
        bv = b_ref[pl.ds(r0, 8), :]
        for d in (1, 2, 4):
            keep = row >= d
            a_sh = pltpu.roll(av, d, 0)
            b_sh = pltpu.roll(bv, d, 0)
            bv = jnp.where(keep, av * b_sh + bv, bv)
            av = jnp.where(keep, av * a_sh, av)
        hv = av * hprev + bv
        b_ref[pl.ds(r0, 8), :] = hv
        return jnp.broadcast_to(hv[7:8, :], (8, tc))

    h_ref[...] = lax.fori_loop(0, ts // 8, group, h_ref[...])
    o_ref[...] = (b_ref[...] * jax.nn.gelu(yr_ref[...])).astype(o_ref.dtype)


def _rglru(rest, conv_w, conv_b, w_rg_a, b_rg_a, w_rg_x, b_rg_x, lru_lambda, width, ts=256, tc=512):
    s = rest.shape[0]
    nct = width // tc
    nb = tc // LRU_BLOCK_W
    vec = lambda v: v.reshape(1, width)
    vspec = pl.BlockSpec((1, tc), lambda c, t: (0, c))
    wspec = pl.BlockSpec((nb, LRU_BLOCK_W, LRU_BLOCK_W), lambda c, t: (c, 0, 0))
    return pl.pallas_call(
        functools.partial(_lru_kernel, ts=ts, tc=tc),
        grid=(nct, s // ts),
        in_specs=[pl.BlockSpec((ts, tc), lambda c, t: (t, c)),
                  pl.BlockSpec((ts, tc), lambda c, t: (t, nct + c)),
                  pl.BlockSpec((CONV_WIDTH, tc), lambda c, t: (0, c)),
                  vspec, wspec, vspec, wspec, vspec, vspec],
        out_specs=pl.BlockSpec((ts, tc), lambda c, t: (t, c)),
        out_shape=jax.ShapeDtypeStruct((s, width), BF16),
        scratch_shapes=[pltpu.VMEM((ts + 8, tc), F32),
                        pltpu.VMEM((ts, tc), F32),
                        pltpu.VMEM((ts, tc), F32),
                        pltpu.VMEM((8, tc), F32)],
        compiler_params=_params(("parallel", "arbitrary")),
        name="rglru",
    )(rest, rest, conv_w, vec(conv_b), w_rg_a, vec(b_rg_a), w_rg_x, vec(b_rg_x), vec(lru_lambda))


def _merge_kernel(att_ref, lru_ref, wa_ref, wl_ref, ga_ref, gl_ref, o_ref):
    pa = jnp.dot(att_ref[...], wa_ref[...], preferred_element_type=F32)
    plru = jnp.dot(lru_ref[...], wl_ref[...], preferred_element_type=F32)
    o_ref[...] = (jax.nn.sigmoid(ga_ref[...]) * pa + jax.nn.sigmoid(gl_ref[...]) * plru).astype(o_ref.dtype)


def _merge(att, lru, w_att, w_lru, rest, gate_col0, bm=512, bn=1024):
    m, k = att.shape
    n = w_att.shape[1]
    ga0 = gate_col0 // bn
    gl0 = (gate_col0 + n) // bn
    return pl.pallas_call(
        _merge_kernel,
        grid=(n // bn, m // bm),
        in_specs=[pl.BlockSpec((bm, k), lambda j, i: (i, 0)),
                  pl.BlockSpec((bm, k), lambda j, i: (i, 0)),
                  pl.BlockSpec((k, bn), lambda j, i: (0, j)),
                  pl.BlockSpec((k, bn), lambda j, i: (0, j)),
                  pl.BlockSpec((bm, bn), lambda j, i: (i, ga0 + j)),
                  pl.BlockSpec((bm, bn), lambda j, i: (i, gl0 + j))],
        out_specs=pl.BlockSpec((bm, bn), lambda j, i: (i, j)),
        out_shape=jax.ShapeDtypeStruct((m, n), BF16),
        compiler_params=_params(("parallel", "arbitrary")),
        name="merge",
    )(att, lru, w_att, w_lru, rest, rest)


def _outproj_kernel(a_ref, w_ref, x_ref, nw_ref, x1_ref, h2_ref):
    x1 = x_ref[...] + jnp.dot(a_ref[...], w_ref[...], preferred_element_type=F32)
    x1_ref[...] = x1
    y = x1 * lax.rsqrt(jnp.mean(x1 * x1, axis=-1, keepdims=True) + EPS)
    h2_ref[...] = (y * nw_ref[...]).astype(h2_ref.dtype)


def _outproj(merged, w_out, x, norm2_w, bm=512):
    m, k = merged.shape
    d = w_out.shape[1]
    return pl.pallas_call(
        _outproj_kernel,
        grid=(m // bm,),
        in_specs=[pl.BlockSpec((bm, k), lambda i: (i, 0)),
                  pl.BlockSpec((k, d), lambda i: (0, 0)),
                  pl.BlockSpec((bm, d), lambda i: (i, 0)),
                  pl.BlockSpec((1, d), lambda i: (0, 0))],
        out_specs=[pl.BlockSpec((bm, d), lambda i: (i, 0)),
                   pl.BlockSpec((bm, d), lambda i: (i, 0))],
        out_shape=[jax.ShapeDtypeStruct((m, d), F32), jax.ShapeDtypeStruct((m, d), BF16)],
        compiler_params=_params(("parallel",)),
        name="outproj",
    )(merged, w_out, x, norm2_w.reshape(1, d))


def _ffn_up_kernel(h_ref, wg_ref, wu_ref, o_ref):
    h = h_ref[...]
    g = jnp.dot(h, wg_ref[...], preferred_element_type=F32)
    u = jnp.dot(h, wu_ref[...], preferred_element_type=F32)
    o_ref[...] = (jax.nn.silu(g) * u).astype(o_ref.dtype)


def _ffn_up(h2, w_gate, w_up, bm=1024, bn=512):
    m, k = h2.shape
    n = w_gate.shape[1]
    return pl.pallas_call(
        _ffn_up_kernel,
        grid=(n // bn, m // bm),
        in_specs=[pl.BlockSpec((bm, k), lambda j, i: (i, 0)),
                  pl.BlockSpec((k, bn), lambda j, i: (0, j)),
                  pl.BlockSpec((k, bn), lambda j, i: (0, j))],
        out_specs=pl.BlockSpec((bm, bn), lambda j, i: (i, j)),
        out_shape=jax.ShapeDtypeStruct((m, n), BF16),
        compiler_params=_params(("parallel", "arbitrary")),
        name="ffn_up",
    )(h2, w_gate, w_up)


def _ffn_down_kernel(a_ref, w_ref, x_ref, o_ref):
    o_ref[...] = x_ref[...] + jnp.dot(a_ref[...], w_ref[...], preferred_element_type=F32)


def _ffn_down(act, w_down, x1, bm=512, bn=1024):
    m, k = act.shape
    n = w_down.shape[1]
    return pl.pallas_call(
        _ffn_down_kernel,
        grid=(n // bn, m // bm),
        in_specs=[pl.BlockSpec((bm, k), lambda j, i: (i, 0)),
                  pl.BlockSpec((k, bn), lambda j, i: (0, j)),
                  pl.BlockSpec((bm, bn), lambda j, i: (i, j))],
        out_specs=pl.BlockSpec((bm, bn), lambda j, i: (i, j)),
        out_shape=jax.ShapeDtypeStruct((m, n), F32),
        compiler_params=_params(("parallel", "arbitrary")),
        name="ffn_down",
    )(act, w_down, x1)


def _layer(x, norm1_w, w_in, q_norm_w, k_norm_w, conv_w, conv_b, w_rg_a, b_rg_a, w_rg_x, b_rg_x,
           lru_lambda, w_proj_attn, w_proj_lru, w_out, norm2_w, w_ffn_gate, w_ffn_up, w_ffn_down):
    d = x.shape[1]
    att_w = N_HEADS * HEAD_DIM
    lru_w = w_proj_lru.shape[0]
    w_in_b = w_in.astype(BF16)

    h = _rmsnorm(x, norm1_w)
    qkv = _qkv_proj(h, w_in_b, q_norm_w, k_norm_w, 3 * att_w)
    rest = _rest_proj(h, w_in_b, 3 * att_w, 2 * lru_w + 2 * d)

    head = jnp.arange(1, N_HEADS + 1, dtype=F32)
    slopes = jnp.broadcast_to(jnp.exp2(-8.0 * head / N_HEADS)[:, None, None], (N_HEADS, 1, MOBA_BLOCK))
    att = _moba_attention(qkv, slopes)

    lru = _rglru(rest, conv_w, conv_b, w_rg_a.astype(BF16), b_rg_a, w_rg_x.astype(BF16), b_rg_x,
                 lru_lambda, lru_w)

    merged = _merge(att, lru, w_proj_attn.astype(BF16), w_proj_lru.astype(BF16), rest, 2 * lru_w)
    x1, h2 = _outproj(merged, w_out.astype(BF16), x, norm2_w)
    act = _ffn_up(h2, w_ffn_gate.astype(BF16), w_ffn_up.astype(BF16))
    return _ffn_down(act, w_ffn_down.astype(BF16), x1)


def kernel(x, norm1_w, w_in, q_norm_w, k_norm_w, conv_w, conv_b, w_rg_a, b_rg_a, w_rg_x, b_rg_x,
           lru_lambda, w_proj_attn, w_proj_lru, w_out, norm2_w, w_ffn_gate, w_ffn_up, w_ffn_down):
    b, s, d = x.shape
    assert b == 1, "kernel handles the batch-1 prefill shape"
    y = x.reshape(s, d)
    for layer in range(norm1_w.shape[0]):
        y = _layer(y, norm1_w[layer], w_in[layer], q_norm_w[layer], k_norm_w[layer], conv_w[layer],
                   conv_b[layer], w_rg_a[layer], b_rg_a[layer], w_rg_x[layer], b_rg_x[layer],
                   lru_lambda[layer], w_proj_attn[layer], w_proj_lru[layer], w_out[layer],
                   norm2_w[layer], w_ffn_gate[layer], w_ffn_up[layer], w_ffn_down[layer])
    return y.reshape(b, s, d)
```

```python
import functools

import jax
import jax.numpy as jnp
from jax import lax
from jax.experimental import pallas as pl
from jax.experimental.pallas import tpu as pltpu

F32 = jnp.float32
BF16 = jnp.bfloat16

N_HEADS = 16
HEAD_DIM = 128
MOBA_BLOCK = 256
MOBA_TOPK = 3
LRU_BLOCK_W = 128
CONV_WIDTH = 4
LRU_C = 8.0
EPS = 1e-6
NEG_INF = -1e30
LOG2E = 1.4426950408889634

V7X_VMEM_BYTES = 64 * 1024 * 1024
VMEM_LIMIT = 56 * 1024 * 1024


def _params(semantics):
    return pltpu.CompilerParams(dimension_semantics=semantics, vmem_limit_bytes=VMEM_LIMIT)


def _rmsnorm_kernel(x_ref, w_ref, o_ref):
    x = x_ref[...]
    y = x * lax.rsqrt(jnp.mean(x * x, axis=-1, keepdims=True) + EPS)
    o_ref[...] = (y * w_ref[...]).astype(o_ref.dtype)


def _rmsnorm(x, w, tm=512):
    m, d = x.shape
    return pl.pallas_call(
        _rmsnorm_kernel,
        grid=(m // tm,),
        in_specs=[pl.BlockSpec((tm, d), lambda i: (i, 0)),
                  pl.BlockSpec((1, d), lambda i: (0, 0))],
        out_specs=pl.BlockSpec((tm, d), lambda i: (i, 0)),
        out_shape=jax.ShapeDtypeStruct((m, d), BF16),
        compiler_params=_params(("parallel",)),
        name="rmsnorm1",
    )(x, w.reshape(1, d))


def _qkv_kernel(h_ref, w_ref, qw_ref, kw_ref, o_ref, *, qk_tiles):
    j = pl.program_id(1)
    acc = jnp.dot(h_ref[...], w_ref[...], preferred_element_type=F32)
    bn = acc.shape[1]

    @pl.when(j < qk_tiles)
    def _():
        nw = jnp.where(j < qk_tiles // 2, qw_ref[...], kw_ref[...])
        for hh in range(bn // HEAD_DIM):
            a = acc[:, hh * HEAD_DIM:(hh + 1) * HEAD_DIM]
            y = a * lax.rsqrt(jnp.mean(a * a, axis=-1, keepdims=True) + EPS)
            o_ref[:, hh * HEAD_DIM:(hh + 1) * HEAD_DIM] = (y * nw).astype(o_ref.dtype)

    @pl.when(j >= qk_tiles)
    def _():
        o_ref[...] = acc.astype(o_ref.dtype)


def _qkv_proj(h, w_in, q_norm_w, k_norm_w, n_cols, bm=1024, bn=1024):
    m, k = h.shape
    return pl.pallas_call(
        functools.partial(_qkv_kernel, qk_tiles=(2 * N_HEADS * HEAD_DIM) // bn),
        grid=(m // bm, n_cols // bn),
        in_specs=[pl.BlockSpec((bm, k), lambda i, j: (i, 0)),
                  pl.BlockSpec((k, bn), lambda i, j: (0, j)),
                  pl.BlockSpec((1, HEAD_DIM), lambda i, j: (0, 0)),
                  pl.BlockSpec((1, HEAD_DIM), lambda i, j: (0, 0))],
        out_specs=pl.BlockSpec((bm, bn), lambda i, j: (i, j)),
        out_shape=jax.ShapeDtypeStruct((m, n_cols), BF16),
        compiler_params=_params(("parallel", "arbitrary")),
        name="qkv_proj",
    )(h, w_in, q_norm_w.reshape(1, HEAD_DIM), k_norm_w.reshape(1, HEAD_DIM))


def _matmul_kernel(a_ref, w_ref, o_ref):
    o_ref[...] = jnp.dot(a_ref[...], w_ref[...], preferred_element_type=F32).astype(o_ref.dtype)


def _rest_proj(h, w_in, col0, n_cols, bm=1024, bn=1024):
    m, k = h.shape
    jb = col0 // bn
    return pl.pallas_call(
        _matmul_kernel,
        grid=(m // bm, n_cols // bn),
        in_specs=[pl.BlockSpec((bm, k), lambda i, j: (i, 0)),
                  pl.BlockSpec((k, bn), lambda i, j: (0, jb + j))],
        out_specs=pl.BlockSpec((bm, bn), lambda i, j: (i, j)),
        out_shape=jax.ShapeDtypeStruct((m, n_cols), F32),
        compiler_params=_params(("parallel", "arbitrary")),
        name="rest_proj",
    )(h, w_in)


def _attn_prep_kernel(slope_ref, q_ref, k_ref, v_ref, qt_ref, vt_ref, rb_ref, kmean_ref,
                      *, n_blk, group, cols):
    blk = MOBA_BLOCK
    for jb in range(n_blk):
        rows = slice(jb * blk, (jb + 1) * blk)
        qt_ref[:, rows] = q_ref[rows, :].astype(F32).T.astype(BF16)
        c, g = divmod(jb, group)
        vt_ref[c, :, g * blk:(g + 1) * blk] = v_ref[rows, :].astype(F32).T.astype(BF16)
        kmean_ref[jb:jb + 1, :] = jnp.sum(k_ref[rows, :].astype(F32), axis=0, keepdims=True) * (1.0 / blk)
    kmean = kmean_ref[...].astype(BF16)
    slope = slope_ref[:, 0:1]
    row = lax.broadcasted_iota(jnp.int32, (n_blk, cols), 0)
    col = lax.broadcasted_iota(jnp.int32, (n_blk, cols), 1)
    for qc in range(qt_ref.shape[1] // cols):
        csl = slice(qc * cols, (qc + 1) * cols)
        gate = jnp.dot(kmean, qt_ref[:, csl], preferred_element_type=F32)
        qblk = lax.shift_right_logical(col + qc * cols, blk.bit_length() - 1)
        g = jnp.where(row < qblk, gate, NEG_INF)
        bias = jnp.full((n_blk, cols), NEG_INF, F32)
        for r in range(MOBA_TOPK):
            mx = jnp.max(g, axis=0, keepdims=True)
            idx = jnp.min(jnp.where(g == mx, row, n_blk), axis=0, keepdims=True)
            pick = row == idx
            bias = jnp.where(pick, jnp.where(qblk > r, 0.0, NEG_INF), bias)
            g = jnp.where(pick, -jnp.inf, g)
        past = bias - slope * ((qblk - row) * blk).astype(F32)
        rb = jnp.where(row == qblk, 0.0, jnp.where(row < qblk, past, NEG_INF))
        rb_ref[:, csl] = rb * LOG2E


def _attn_kernel(slope_ref, qt_ref, k_ref, vt_ref, rb_ref, o_ref, u_ref, dmat_ref, acc_ref, *, group):
    blk = MOBA_BLOCK
    cb = group * blk
    i = pl.program_id(1)

    @pl.when(i == 0)
    def _per_head_setup():
        kk = lax.broadcasted_iota(jnp.int32, (blk, blk), 0)
        qq = lax.broadcasted_iota(jnp.int32, (blk, blk), 1)
        d = slope_ref[:, 0:1] * (qq - kk).astype(F32) * LOG2E
        dmat_ref[0] = d
        dmat_ref[1] = jnp.where(kk <= qq, d, -NEG_INF)

    qt = qt_ref[...]
    k2 = (HEAD_DIM ** -0.5) * LOG2E
    n_chunks = lax.shift_right_logical(i + group, group.bit_length() - 1)

    def pass1(c, mx):
        r0 = pl.multiple_of(c * cb, cb)
        s = jnp.dot(k_ref[pl.ds(r0, cb), :], qt, preferred_element_type=F32)
        for g in range(group):
            j = c * group + g
            own = (j == i).astype(jnp.int32)
            u = s[g * blk:(g + 1) * blk, :] * k2 - dmat_ref[own]
            u_ref[c, g * blk:(g + 1) * blk, :] = u
            mx = jnp.maximum(mx, jnp.max(u, axis=0, keepdims=True) + rb_ref[pl.ds(j, 1), :])
        return mx

    m2 = lax.fori_loop(0, n_chunks, pass1, jnp.full((1, blk), -jnp.inf, F32))
    acc_ref[...] = jnp.zeros_like(acc_ref)

    def pass2(c, l):
        ps = []
        for g in range(group):
            j = c * group + g
            p = jnp.exp2(u_ref[c, g * blk:(g + 1) * blk, :] + (rb_ref[pl.ds(j, 1), :] - m2))
            l = l + jnp.sum(p, axis=0, keepdims=True)
            ps.append(p.astype(BF16))
        acc_ref[...] += jnp.dot(vt_ref[c], jnp.concatenate(ps, axis=0), preferred_element_type=F32)
        return l

    l = lax.fori_loop(0, n_chunks, pass2, jnp.zeros((1, blk), F32))
    o_ref[...] = (acc_ref[...] / l).T.astype(o_ref.dtype)


def _moba_attention(qkv, slopes, group=4, cols=2048):
    s = qkv.shape[0]
    blk = MOBA_BLOCK
    n_blk = s // blk
    n_chunk = n_blk // group
    qt, vt, rb = pl.pallas_call(
        functools.partial(_attn_prep_kernel, n_blk=n_blk, group=group, cols=cols),
        grid=(N_HEADS,),
        in_specs=[pl.BlockSpec((None, 1, blk), lambda h: (h, 0, 0)),
                  pl.BlockSpec((s, HEAD_DIM), lambda h: (0, h)),
                  pl.BlockSpec((s, HEAD_DIM), lambda h: (0, N_HEADS + h)),
                  pl.BlockSpec((s, HEAD_DIM), lambda h: (0, 2 * N_HEADS + h))],
        out_specs=[pl.BlockSpec((None, HEAD_DIM, s), lambda h: (h, 0, 0)),
                   pl.BlockSpec((None, n_chunk, HEAD_DIM, group * blk), lambda h: (h, 0, 0, 0)),
                   pl.BlockSpec((None, n_blk, s), lambda h: (h, 0, 0))],
        out_shape=[jax.ShapeDtypeStruct((N_HEADS, HEAD_DIM, s), BF16),
                   jax.ShapeDtypeStruct((N_HEADS, n_chunk, HEAD_DIM, group * blk), BF16),
                   jax.ShapeDtypeStruct((N_HEADS, n_blk, s), F32)],
        scratch_shapes=[pltpu.VMEM((n_blk, HEAD_DIM), F32)],
        compiler_params=_params(("parallel",)),
        name="moba_prep",
    )(slopes, qkv, qkv, qkv)
    return pl.pallas_call(
        functools.partial(_attn_kernel, group=group),
        grid=(N_HEADS, n_blk),
        in_specs=[pl.BlockSpec((None, 1, blk), lambda h, i: (h, 0, 0)),
                  pl.BlockSpec((None, HEAD_DIM, blk), lambda h, i: (h, 0, i)),
                  pl.BlockSpec((s, HEAD_DIM), lambda h, i: (0, N_HEADS + h)),
                  pl.BlockSpec((None, n_chunk, HEAD_DIM, group * blk), lambda h, i: (h, 0, 0, 0)),
                  pl.BlockSpec((None, n_blk, blk), lambda h, i: (h, 0, i))],
        out_specs=pl.BlockSpec((blk, HEAD_DIM), lambda h, i: (i, h)),
        out_shape=jax.ShapeDtypeStruct((s, N_HEADS * HEAD_DIM), BF16),
        scratch_shapes=[pltpu.VMEM((n_chunk, group * blk, blk), F32),
                        pltpu.VMEM((2, blk, blk), F32),
                        pltpu.VMEM((HEAD_DIM, blk), F32)],
        compiler_params=_params(("parallel", "arbitrary")),
        name="moba_attention",
    )(slopes, qt, qkv, vt, rb)


def _lru_kernel(xr_ref, yr_ref, cw_ref, cb_ref, wa_ref, ba_ref, wx_ref, bx_ref, lam_ref, o_ref,
                xbuf_ref, a_ref, b_ref, h_ref, *, ts, tc):
    t = pl.program_id(1)
    pad = 8

    @pl.when(t == 0)
    def _():
        xbuf_ref[0:pad, :] = jnp.zeros((pad, tc), F32)
        h_ref[...] = jnp.zeros_like(h_ref)

    xbuf_ref[pad:pad + ts, :] = xr_ref[...]
    cw = cw_ref[...]
    u = cb_ref[...]
    for tap in range(CONV_WIDTH):
        off = pad - (CONV_WIDTH - 1) + tap
        u = u + xbuf_ref[off:off + ts, :] * cw[tap:tap + 1, :]
    xbuf_ref[0:pad, :] = xbuf_ref[ts:ts + pad, :]

    ub = u.astype(BF16)
    ga, gx = [], []
    for n in range(tc // LRU_BLOCK_W):
        un = ub[:, n * LRU_BLOCK_W:(n + 1) * LRU_BLOCK_W]
        ga.append(jnp.dot(un, wa_ref[n], preferred_element_type=F32))
        gx.append(jnp.dot(un, wx_ref[n], preferred_element_type=F32))
    r = jax.nn.sigmoid(jnp.concatenate(ga, axis=1) + ba_ref[...])
    ig = jax.nn.sigmoid(jnp.concatenate(gx, axis=1) + bx_ref[...])
    log_a = -LRU_C * r * jax.nn.softplus(-lam_ref[...])
    a = jnp.exp(log_a)
    a_ref[...] = a
    one_minus_a2 = -jnp.tanh(log_a) * (a * a + 1.0)
    b_ref[...] = jnp.sqrt(one_minus_a2) * ig * u

    row = lax.broadcasted_iota(jnp.int32, (8, tc), 0)

    def group(g, hprev):
        r0 = pl.multiple_of(g * 8, 8)
        av = a_ref[pl.ds(r0, 8), :]
        bv = b_ref[pl.ds(r0, 8), :]
        for d in (1, 2, 4):
            keep = row >= d
            a_sh = pltpu.roll(av, d, 0)
            b_sh = pltpu.roll(bv, d, 0)
            bv = jnp.where(keep, av * b_sh + bv, bv)
            av = jnp.where(keep, av * a_sh, av)
        hv = av * hprev + bv
        b_ref[pl.ds(r0, 8), :] = hv
        return jnp.broadcast_to(hv[7:8, :], (8, tc))

    h_ref[...] = lax.fori_loop(0, ts // 8, group, h_ref[...])
    o_ref[...] = (b_ref[...] * jax.nn.gelu(yr_ref[...])).astype(o_ref.dtype)


def _rglru(rest, conv_w, conv_b, w_rg_a, b_rg_a, w_rg_x, b_rg_x, lru_lambda, width, ts=256, tc=512):
    s = rest.shape[0]
    nct = width // tc
    nb = tc // LRU_BLOCK_W
    vec = lambda v: v.reshape(1, width)
    vspec = pl.BlockSpec((1, tc), lambda c, t: (0, c))
    wspec = pl.BlockSpec((nb, LRU_BLOCK_W, LRU_BLOCK_W), lambda c, t: (c, 0, 0))
    return pl.pallas_call(
        functools.partial(_lru_kernel, ts=ts, tc=tc),
        grid=(nct, s // ts),
        in_specs=[pl.BlockSpec((ts, tc), lambda c, t: (t, c)),
                  pl.BlockSpec((ts, tc), lambda c, t: (t, nct + c)),
                  pl.BlockSpec((CONV_WIDTH, tc), lambda c, t: (0, c)),
                  vspec, wspec, vspec, wspec, vspec, vspec],
        out_specs=pl.BlockSpec((ts, tc), lambda c, t: (t, c)),
        out_shape=jax.ShapeDtypeStruct((s, width), BF16),
        scratch_shapes=[pltpu.VMEM((ts + 8, tc), F32),
                        pltpu.VMEM((ts, tc), F32),
                        pltpu.VMEM((ts, tc), F32),
                        pltpu.VMEM((8, tc), F32)],
        compiler_params=_params(("parallel", "arbitrary")),
        name="rglru",
    )(rest, rest, conv_w, vec(conv_b), w_rg_a, vec(b_rg_a), w_rg_x, vec(b_rg_x), vec(lru_lambda))


def _merge_kernel(att_ref, lru_ref, wa_ref, wl_ref, ga_ref, gl_ref, o_ref):
    pa = jnp.dot(att_ref[...], wa_ref[...], preferred_element_type=F32)
    plru = jnp.dot(lru_ref[...], wl_ref[...], preferred_element_type=F32)
    o_ref[...] = (jax.nn.sigmoid(ga_ref[...]) * pa + jax.nn.sigmoid(gl_ref[...]) * plru).astype(o_ref.dtype)


def _merge(att, lru, w_att, w_lru, rest, gate_col0, bm=512, bn=1024):
    m, k = att.shape
    n = w_att.shape[1]
    ga0 = gate_col0 // bn
    gl0 = (gate_col0 + n) // bn
    return pl.pallas_call(
        _merge_kernel,
        grid=(n // bn, m // bm),
        in_specs=[pl.BlockSpec((bm, k), lambda j, i: (i, 0)),
                  pl.BlockSpec((bm, k), lambda j, i: (i, 0)),
                  pl.BlockSpec((k, bn), lambda j, i: (0, j)),
                  pl.BlockSpec((k, bn), lambda j, i: (0, j)),
                  pl.BlockSpec((bm, bn), lambda j, i: (i, ga0 + j)),
                  pl.BlockSpec((bm, bn), lambda j, i: (i, gl0 + j))],
        out_specs=pl.BlockSpec((bm, bn), lambda j, i: (i, j)),
        out_shape=jax.ShapeDtypeStruct((m, n), BF16),
        compiler_params=_params(("parallel", "arbitrary")),
        name="merge",
    )(att, lru, w_att, w_lru, rest, rest)


def _outproj_kernel(a_ref, w_ref, x_ref, nw_ref, x1_ref, h2_ref):
    x1 = x_ref[...] + jnp.dot(a_ref[...], w_ref[...], preferred_element_type=F32)
    x1_ref[...] = x1
    y = x1 * lax.rsqrt(jnp.mean(x1 * x1, axis=-1, keepdims=True) + EPS)
    h2_ref[...] = (y * nw_ref[...]).astype(h2_ref.dtype)


def _outproj(merged, w_out, x, norm2_w, bm=512):
    m, k = merged.shape
    d = w_out.shape[1]
    return pl.pallas_call(
        _outproj_kernel,
        grid=(m // bm,),
        in_specs=[pl.BlockSpec((bm, k), lambda i: (i, 0)),
                  pl.BlockSpec((k, d), lambda i: (0, 0)),
                  pl.BlockSpec((bm, d), lambda i: (i, 0)),
                  pl.BlockSpec((1, d), lambda i: (0, 0))],
        out_specs=[pl.BlockSpec((bm, d), lambda i: (i, 0)),
                   pl.BlockSpec((bm, d), lambda i: (i, 0))],
        out_shape=[jax.ShapeDtypeStruct((m, d), F32), jax.ShapeDtypeStruct((m, d), BF16)],
        compiler_params=_params(("parallel",)),
        name="outproj",
    )(merged, w_out, x, norm2_w.reshape(1, d))


def _ffn_up_kernel(h_ref, wg_ref, wu_ref, o_ref):
    h = h_ref[...]
    g = jnp.dot(h, wg_ref[...], preferred_element_type=F32)
    u = jnp.dot(h, wu_ref[...], preferred_element_type=F32)
    o_ref[...] = (jax.nn.silu(g) * u).astype(o_ref.dtype)


def _ffn_up(h2, w_gate, w_up, bm=1024, bn=512):
    m, k = h2.shape
    n = w_gate.shape[1]
    return pl.pallas_call(
        _ffn_up_kernel,
        grid=(n // bn, m // bm),
        in_specs=[pl.BlockSpec((bm, k), lambda j, i: (i, 0)),
                  pl.BlockSpec((k, bn), lambda j, i: (0, j)),
                  pl.BlockSpec((k, bn), lambda j, i: (0, j))],
        out_specs=pl.BlockSpec((bm, bn), lambda j, i: (i, j)),
        out_shape=jax.ShapeDtypeStruct((m, n), BF16),
        compiler_params=_params(("parallel", "arbitrary")),
        name="ffn_up",
    )(h2, w_gate, w_up)


def _ffn_down_kernel(a_ref, w_ref, x_ref, o_ref):
    o_ref[...] = x_ref[...] + jnp.dot(a_ref[...], w_ref[...], preferred_element_type=F32)


def _ffn_down(act, w_down, x1, bm=512, bn=1024):
    m, k = act.shape
    n = w_down.shape[1]
    return pl.pallas_call(
        _ffn_down_kernel,
        grid=(n // bn, m // bm),
        in_specs=[pl.BlockSpec((bm, k), lambda j, i: (i, 0)),
                  pl.BlockSpec((k, bn), lambda j, i: (0, j)),
                  pl.BlockSpec((bm, bn), lambda j, i: (i, j))],
        out_specs=pl.BlockSpec((bm, bn), lambda j, i: (i, j)),
        out_shape=jax.ShapeDtypeStruct((m, n), F32),
        compiler_params=_params(("parallel", "arbitrary")),
        name="ffn_down",
    )(act, w_down, x1)


def _layer(x, norm1_w, w_in, q_norm_w, k_norm_w, conv_w, conv_b, w_rg_a, b_rg_a, w_rg_x, b_rg_x,
           lru_lambda, w_proj_attn, w_proj_lru, w_out, norm2_w, w_ffn_gate, w_ffn_up, w_ffn_down):
    d = x.shape[1]
    att_w = N_HEADS * HEAD_DIM
    lru_w = w_proj_lru.shape[0]
    w_in_b = w_in.astype(BF16)

    h = _rmsnorm(x, norm1_w)
    qkv = _qkv_proj(h, w_in_b, q_norm_w, k_norm_w, 3 * att_w)
    rest = _rest_proj(h, w_in_b, 3 * att_w, 2 * lru_w + 2 * d)

    head = jnp.arange(1, N_HEADS + 1, dtype=F32)
    slopes = jnp.broadcast_to(jnp.exp2(-8.0 * head / N_HEADS)[:, None, None], (N_HEADS, 1, MOBA_BLOCK))
    att = _moba_attention(qkv, slopes)

    lru = _rglru(rest, conv_w, conv_b, w_rg_a.astype(BF16), b_rg_a, w_rg_x.astype(BF16), b_rg_x,
                 lru_lambda, lru_w)

    merged = _merge(att, lru, w_proj_attn.astype(BF16), w_proj_lru.astype(BF16), rest, 2 * lru_w)
    x1, h2 = _outproj(merged, w_out.astype(BF16), x, norm2_w)
    act = _ffn_up(h2, w_ffn_gate.astype(BF16), w_ffn_up.astype(BF16))
    return _ffn_down(act, w_ffn_down.astype(BF16), x1)


def kernel(x, norm1_w, w_in, q_norm_w, k_norm_w, conv_w, conv_b, w_rg_a, b_rg_a, w_rg_x, b_rg_x,
           lru_lambda, w_proj_attn, w_proj_lru, w_out, norm2_w, w_ffn_gate, w_ffn_up, w_ffn_down):
    b, s, d = x.shape
    assert b == 1, "kernel handles the batch-1 prefill shape"
    y = x.reshape(s, d)
    for layer in range(norm1_w.shape[0]):
        y = _layer(y, norm1_w[layer], w_in[layer], q_norm_w[layer], k_norm_w[layer], conv_w[layer],
                   conv_b[layer], w_rg_a[layer], b_rg_a[layer], w_rg_x[layer], b_rg_x[layer],
                   lru_lambda[layer], w_proj_attn[layer], w_proj_lru[layer], w_out[layer],
                   norm2_w[layer], w_ffn_gate[layer], w_ffn_up[layer], w_ffn_down[layer])
    return y.reshape(b, s, d)
```

```python
import functools

import jax
import jax.numpy as jnp
from jax import lax
from jax.experimental import pallas as pl
from jax.experimental.pallas import tpu as pltpu

F32 = jnp.float32
BF16 = jnp.bfloat16

N_HEADS = 16
HEAD_DIM = 128
MOBA_BLOCK = 256
MOBA_TOPK = 3
LRU_BLOCK_W = 128
CONV_WIDTH = 4
LRU_C = 8.0
EPS = 1e-6
NEG_INF = -1e30
LOG2E = 1.4426950408889634

V7X_VMEM_BYTES = 64 * 1024 * 1024
VMEM_LIMIT = 56 * 1024 * 1024


def _params(semantics):
    return pltpu.CompilerParams(dimension_semantics=semantics, vmem_limit_bytes=VMEM_LIMIT)


def _rmsnorm_kernel(x_ref, w_ref, o_ref):
    x = x_ref[...]
    y = x * lax.rsqrt(jnp.mean(x * x, axis=-1, keepdims=True) + EPS)
    o_ref[...] = (y * w_ref[...]).astype(o_ref.dtype)


def _rmsnorm(x, w, tm=512):
    m, d = x.shape
    return pl.pallas_call(
        _rmsnorm_kernel,
        grid=(m // tm,),
        in_specs=[pl.BlockSpec((tm, d), lambda i: (i, 0)),
                  pl.BlockSpec((1, d), lambda i: (0, 0))],
        out_specs=pl.BlockSpec((tm, d), lambda i: (i, 0)),
        out_shape=jax.ShapeDtypeStruct((m, d), BF16),
        compiler_params=_params(("parallel",)),
        name="rmsnorm1",
    )(x, w.reshape(1, d))


def _qkv_kernel(h_ref, w_ref, qw_ref, kw_ref, o_ref, *, qk_tiles):
    j = pl.program_id(1)
    acc = jnp.dot(h_ref[...], w_ref[...], preferred_element_type=F32)
    bn = acc.shape[1]

    @pl.when(j < qk_tiles)
    def _():
        nw = jnp.where(j < qk_tiles // 2, qw_ref[...], kw_ref[...])
        for hh in range(bn // HEAD_DIM):
            a = acc[:, hh * HEAD_DIM:(hh + 1) * HEAD_DIM]
            y = a * lax.rsqrt(jnp.mean(a * a, axis=-1, keepdims=True) + EPS)
            o_ref[:, hh * HEAD_DIM:(hh + 1) * HEAD_DIM] = (y * nw).astype(o_ref.dtype)

    @pl.when(j >= qk_tiles)
    def _():
        o_ref[...] = acc.astype(o_ref.dtype)


def _qkv_proj(h, w_in, q_norm_w, k_norm_w, n_cols, bm=1024, bn=1024):
    m, k = h.shape
    return pl.pallas_call(
        functools.partial(_qkv_kernel, qk_tiles=(2 * N_HEADS * HEAD_DIM) // bn),
        grid=(m // bm, n_cols // bn),
        in_specs=[pl.BlockSpec((bm, k), lambda i, j: (i, 0)),
                  pl.BlockSpec((k, bn), lambda i, j: (0, j)),
                  pl.BlockSpec((1, HEAD_DIM), lambda i, j: (0, 0)),
                  pl.BlockSpec((1, HEAD_DIM), lambda i, j: (0, 0))],
        out_specs=pl.BlockSpec((bm, bn), lambda i, j: (i, j)),
        out_shape=jax.ShapeDtypeStruct((m, n_cols), BF16),
        compiler_params=_params(("parallel", "arbitrary")),
        name="qkv_proj",
    )(h, w_in, q_norm_w.reshape(1, HEAD_DIM), k_norm_w.reshape(1, HEAD_DIM))


def _matmul_kernel(a_ref, w_ref, o_ref):
    o_ref[...] = jnp.dot(a_ref[...], w_ref[...], preferred_element_type=F32).astype(o_ref.dtype)


def _rest_proj(h, w_in, col0, n_cols, bm=1024, bn=1024):
    m, k = h.shape
    jb = col0 // bn
    return pl.pallas_call(
        _matmul_kernel,
        grid=(m // bm, n_cols // bn),
        in_specs=[pl.BlockSpec((bm, k), lambda i, j: (i, 0)),
                  pl.BlockSpec((k, bn), lambda i, j: (0, jb + j))],
        out_specs=pl.BlockSpec((bm, bn), lambda i, j: (i, j)),
        out_shape=jax.ShapeDtypeStruct((m, n_cols), F32),
        compiler_params=_params(("parallel", "arbitrary")),
        name="rest_proj",
    )(h, w_in)


def _attn_prep_kernel(slope_ref, q_ref, k_ref, v_ref, qt_ref, vt_ref, rb_ref, kmean_ref,
                      *, n_blk, group, cols):
    blk = MOBA_BLOCK
    for jb in range(n_blk):
        rows = slice(jb * blk, (jb + 1) * blk)
        qt_ref[:, rows] = q_ref[rows, :].astype(F32).T.astype(BF16)
        c, g = divmod(jb, group)
        vt_ref[c, :, g * blk:(g + 1) * blk] = v_ref[rows, :].astype(F32).T.astype(BF16)
        kmean_ref[jb:jb + 1, :] = jnp.sum(k_ref[rows, :].astype(F32), axis=0, keepdims=True) * (1.0 / blk)
    kmean = kmean_ref[...].astype(BF16)
    slope = slope_ref[:, 0:1]
    row = lax.broadcasted_iota(jnp.int32, (n_blk, cols), 0)
    col = lax.broadcasted_iota(jnp.int32, (n_blk, cols), 1)
    for qc in range(qt_ref.shape[1] // cols):
        csl = slice(qc * cols, (qc + 1) * cols)
        gate = jnp.dot(kmean, qt_ref[:, csl], preferred_element_type=F32)
        qblk = lax.shift_right_logical(col + qc * cols, blk.bit_length() - 1)
        g = jnp.where(row < qblk, gate, NEG_INF)
        bias = jnp.full((n_blk, cols), NEG_INF, F32)
        for r in range(MOBA_TOPK):
            mx = jnp.max(g, axis=0, keepdims=True)
            idx = jnp.min(jnp.where(g == mx, row, n_blk), axis=0, keepdims=True)
            pick = row == idx
            bias = jnp.where(pick, jnp.where(qblk > r, 0.0, NEG_INF), bias)
            g = jnp.where(pick, -jnp.inf, g)
        past = bias - slope * ((qblk - row) * blk).astype(F32)
        rb = jnp.where(row == qblk, 0.0, jnp.where(row < qblk, past, NEG_INF))
        rb_ref[:, csl] = rb * LOG2E


def _attn_kernel(slope_ref, qt_ref, k_ref, vt_ref, rb_ref, o_ref,
                 u_ref, rbs_ref, m_ref, dmat_ref, acc_ref, *, group, n_blk):
    blk = MOBA_BLOCK
    cb = group * blk
    lg = group.bit_length() - 1
    i = pl.program_id(1)
    slot = lax.rem(i, 2)
    prev = 1 - slot

    @pl.when(i == 0)
    def _per_head_setup():
        kk = lax.broadcasted_iota(jnp.int32, (blk, blk), 0)
        qq = lax.broadcasted_iota(jnp.int32, (blk, blk), 1)
        d = slope_ref[:, 0:1] * (qq - kk).astype(F32) * LOG2E
        dmat_ref[0] = d
        dmat_ref[1] = jnp.where(kk <= qq, d, -NEG_INF)
        m_ref[...] = jnp.zeros_like(m_ref)

    qt = qt_ref[...]
    k2 = (HEAD_DIM ** -0.5) * LOG2E
    n1 = jnp.where(i < n_blk, lax.shift_right_logical(i + group, lg), 0)
    n2 = lax.shift_right_logical(i + group - 1, lg)
    m_prev = m_ref[prev]

    def pass1(c, mx):
        r0 = pl.multiple_of(c * cb, cb)
        s = jnp.dot(k_ref[pl.ds(r0, cb), :], qt, preferred_element_type=F32)
        for g in range(group):
            j = c * group + g
            own = (j == i).astype(jnp.int32)
            u = s[g * blk:(g + 1) * blk, :] * k2 - dmat_ref[own]
            u_ref[slot, c, g * blk:(g + 1) * blk, :] = u
            mx = jnp.maximum(mx, jnp.max(u, axis=0, keepdims=True) + rb_ref[pl.ds(j, 1), :])
        return mx

    def pass2(c, l):
        ps = []
        for g in range(group):
            j = c * group + g
            p = jnp.exp2(u_ref[prev, c, g * blk:(g + 1) * blk, :]
                         + (rbs_ref[prev, pl.ds(j, 1), :] - m_prev))
            l = l + jnp.sum(p, axis=0, keepdims=True)
            ps.append(p.astype(BF16))
        acc_ref[...] += jnp.dot(vt_ref[c], jnp.concatenate(ps, axis=0), preferred_element_type=F32)
        return l

    def both(c, carry):
        l = pass2(c, carry[1])
        return pass1(c, carry[0]), l

    acc_ref[...] = jnp.zeros_like(acc_ref)
    common = jnp.minimum(n1, n2)
    mx, l = lax.fori_loop(0, common, both,
                          (jnp.full((1, blk), -jnp.inf, F32), jnp.zeros((1, blk), F32)))
    mx = lax.fori_loop(common, n1, pass1, mx)
    l = lax.fori_loop(common, n2, pass2, l)
    m_ref[slot] = mx
    rbs_ref[slot] = rb_ref[...]

    @pl.when(i > 0)
    def _():
        o_ref[...] = (acc_ref[...] / l).T.astype(o_ref.dtype)


def _moba_attention(qkv, slopes, group=4, cols=2048):
    s = qkv.shape[0]
    blk = MOBA_BLOCK
    n_blk = s // blk
    n_chunk = n_blk // group
    qt, vt, rb = pl.pallas_call(
        functools.partial(_attn_prep_kernel, n_blk=n_blk, group=group, cols=cols),
        grid=(N_HEADS,),
        in_specs=[pl.BlockSpec((None, 1, blk), lambda h: (h, 0, 0)),
                  pl.BlockSpec((s, HEAD_DIM), lambda h: (0, h)),
                  pl.BlockSpec((s, HEAD_DIM), lambda h: (0, N_HEADS + h)),
                  pl.BlockSpec((s, HEAD_DIM), lambda h: (0, 2 * N_HEADS + h))],
        out_specs=[pl.BlockSpec((None, HEAD_DIM, s), lambda h: (h, 0, 0)),
                   pl.BlockSpec((None, n_chunk, HEAD_DIM, group * blk), lambda h: (h, 0, 0, 0)),
                   pl.BlockSpec((None, n_blk, s), lambda h: (h, 0, 0))],
        out_shape=[jax.ShapeDtypeStruct((N_HEADS, HEAD_DIM, s), BF16),
                   jax.ShapeDtypeStruct((N_HEADS, n_chunk, HEAD_DIM, group * blk), BF16),
                   jax.ShapeDtypeStruct((N_HEADS, n_blk, s), F32)],
        scratch_shapes=[pltpu.VMEM((n_blk, HEAD_DIM), F32)],
        compiler_params=_params(("parallel",)),
        name="moba_prep",
    )(slopes, qkv, qkv, qkv)
    last = n_blk - 1
    return pl.pallas_call(
        functools.partial(_attn_kernel, group=group, n_blk=n_blk),
        grid=(N_HEADS, n_blk + 1),
        in_specs=[pl.BlockSpec((None, 1, blk), lambda h, i: (h, 0, 0)),
                  pl.BlockSpec((None, HEAD_DIM, blk), lambda h, i: (h, 0, jnp.minimum(i, last))),
                  pl.BlockSpec((s, HEAD_DIM), lambda h, i: (0, N_HEADS + h)),
                  pl.BlockSpec((None, n_chunk, HEAD_DIM, group * blk), lambda h, i: (h, 0, 0, 0)),
                  pl.BlockSpec((None, n_blk, blk), lambda h, i: (h, 0, jnp.minimum(i, last)))],
        out_specs=pl.BlockSpec((blk, HEAD_DIM), lambda h, i: (jnp.maximum(i - 1, 0), h)),
        out_shape=jax.ShapeDtypeStruct((s, N_HEADS * HEAD_DIM), BF16),
        scratch_shapes=[pltpu.VMEM((2, n_chunk, group * blk, blk), F32),
                        pltpu.VMEM((2, n_blk, blk), F32),
                        pltpu.VMEM((2, 1, blk), F32),
                        pltpu.VMEM((2, blk, blk), F32),
                        pltpu.VMEM((HEAD_DIM, blk), F32)],
        compiler_params=_params(("parallel", "arbitrary")),
        name="moba_attention",
    )(slopes, qt, qkv, vt, rb)


def _lru_kernel(xr_ref, yr_ref, cw_ref, cb_ref, wa_ref, ba_ref, wx_ref, bx_ref, lam_ref, o_ref,
                xbuf_ref, a_ref, b_ref, h_ref, *, ts, tc):
    t = pl.program_id(1)
    pad = 8

    @pl.when(t == 0)
    def _():
        xbuf_ref[0:pad, :] = jnp.zeros((pad, tc), F32)
        h_ref[...] = jnp.zeros_like(h_ref)

    xbuf_ref[pad:pad + ts, :] = xr_ref[...]
    cw = cw_ref[...]
    u = cb_ref[...]
    for tap in range(CONV_WIDTH):
        off = pad - (CONV_WIDTH - 1) + tap
        u = u + xbuf_ref[off:off + ts, :] * cw[tap:tap + 1, :]
    xbuf_ref[0:pad, :] = xbuf_ref[ts:ts + pad, :]

    ub = u.astype(BF16)
    ga, gx = [], []
    for n in range(tc // LRU_BLOCK_W):
        un = ub[:, n * LRU_BLOCK_W:(n + 1) * LRU_BLOCK_W]
        ga.append(jnp.dot(un, wa_ref[n], preferred_element_type=F32))
        gx.append(jnp.dot(un, wx_ref[n], preferred_element_type=F32))
    r = jax.nn.sigmoid(jnp.concatenate(ga, axis=1) + ba_ref[...])
    ig = jax.nn.sigmoid(jnp.concatenate(gx, axis=1) + bx_ref[...])
    log_a = -LRU_C * r * jax.nn.softplus(-lam_ref[...])
    a = jnp.exp(log_a)
    a_ref[...] = a
    one_minus_a2 = -jnp.tanh(log_a) * (a * a + 1.0)
    b_ref[...] = jnp.sqrt(one_minus_a2) * ig * u

    row = lax.broadcasted_iota(jnp.int32, (8, tc), 0)

    def group(g, hprev):
        r0 = pl.multiple_of(g * 8, 8)
        av = a_ref[pl.ds(r0, 8), :]
        bv = b_ref[pl.ds(r0, 8), :]
        for d in (1, 2, 4):
            keep = row >= d
            a_sh = pltpu.roll(av, d, 0)
            b_sh = pltpu.roll(bv, d, 0)
            bv = jnp.where(keep, av * b_sh + bv, bv)
            av = jnp.where(keep, av * a_sh, av)
        hv = av * hprev + bv
        b_ref[pl.ds(r0, 8), :] = hv
        return jnp.broadcast_to(hv[7:8, :], (8, tc))

    h_ref[...] = lax.fori_loop(0, ts // 8, group, h_ref[...])
    o_ref[...] = (b_ref[...] * jax.nn.gelu(yr_ref[...])).astype(o_ref.dtype)


def _rglru(rest, conv_w, conv_b, w_rg_a, b_rg_a, w_rg_x, b_rg_x, lru_lambda, width, ts=256, tc=512):
    s = rest.shape[0]
    nct = width // tc
    nb = tc // LRU_BLOCK_W
    vec = lambda v: v.reshape(1, width)
    vspec = pl.BlockSpec((1, tc), lambda c, t: (0, c))
    wspec = pl.BlockSpec((nb, LRU_BLOCK_W, LRU_BLOCK_W), lambda c, t: (c, 0, 0))
    return pl.pallas_call(
        functools.partial(_lru_kernel, ts=ts, tc=tc),
        grid=(nct, s // ts),
        in_specs=[pl.BlockSpec((ts, tc), lambda c, t: (t, c)),
                  pl.BlockSpec((ts, tc), lambda c, t: (t, nct + c)),
                  pl.BlockSpec((CONV_WIDTH, tc), lambda c, t: (0, c)),
                  vspec, wspec, vspec, wspec, vspec, vspec],
        out_specs=pl.BlockSpec((ts, tc), lambda c, t: (t, c)),
        out_shape=jax.ShapeDtypeStruct((s, width), BF16),
        scratch_shapes=[pltpu.VMEM((ts + 8, tc), F32),
                        pltpu.VMEM((ts, tc), F32),
                        pltpu.VMEM((ts, tc), F32),
                        pltpu.VMEM((8, tc), F32)],
        compiler_params=_params(("parallel", "arbitrary")),
        name="rglru",
    )(rest, rest, conv_w, vec(conv_b), w_rg_a, vec(b_rg_a), w_rg_x, vec(b_rg_x), vec(lru_lambda))


def _merge_kernel(att_ref, lru_ref, wa_ref, wl_ref, ga_ref, gl_ref, o_ref):
    pa = jnp.dot(att_ref[...], wa_ref[...], preferred_element_type=F32)
    plru = jnp.dot(lru_ref[...], wl_ref[...], preferred_element_type=F32)
    o_ref[...] = (jax.nn.sigmoid(ga_ref[...]) * pa + jax.nn.sigmoid(gl_ref[...]) * plru).astype(o_ref.dtype)


def _merge(att, lru, w_att, w_lru, rest, gate_col0, bm=512, bn=1024):
    m, k = att.shape
    n = w_att.shape[1]
    ga0 = gate_col0 // bn
    gl0 = (gate_col0 + n) // bn
    return pl.pallas_call(
        _merge_kernel,
        grid=(n // bn, m // bm),
        in_specs=[pl.BlockSpec((bm, k), lambda j, i: (i, 0)),
                  pl.BlockSpec((bm, k), lambda j, i: (i, 0)),
                  pl.BlockSpec((k, bn), lambda j, i: (0, j)),
                  pl.BlockSpec((k, bn), lambda j, i: (0, j)),
                  pl.BlockSpec((bm, bn), lambda j, i: (i, ga0 + j)),
                  pl.BlockSpec((bm, bn), lambda j, i: (i, gl0 + j))],
        out_specs=pl.BlockSpec((bm, bn), lambda j, i: (i, j)),
        out_shape=jax.ShapeDtypeStruct((m, n), BF16),
        compiler_params=_params(("parallel", "arbitrary")),
        name="merge",
    )(att, lru, w_att, w_lru, rest, rest)


def _outproj_kernel(a_ref, w_ref, x_ref, nw_ref, x1_ref, h2_ref):
    x1 = x_ref[...] + jnp.dot(a_ref[...], w_ref[...], preferred_element_type=F32)
    x1_ref[...] = x1
    y = x1 * lax.rsqrt(jnp.mean(x1 * x1, axis=-1, keepdims=True) + EPS)
    h2_ref[...] = (y * nw_ref[...]).astype(h2_ref.dtype)


def _outproj(merged, w_out, x, norm2_w, bm=512):
    m, k = merged.shape
    d = w_out.shape[1]
    return pl.pallas_call(
        _outproj_kernel,
        grid=(m // bm,),
        in_specs=[pl.BlockSpec((bm, k), lambda i: (i, 0)),
                  pl.BlockSpec((k, d), lambda i: (0, 0)),
                  pl.BlockSpec((bm, d), lambda i: (i, 0)),
                  pl.BlockSpec((1, d), lambda i: (0, 0))],
        out_specs=[pl.BlockSpec((bm, d), lambda i: (i, 0)),
                   pl.BlockSpec((bm, d), lambda i: (i, 0))],
        out_shape=[jax.ShapeDtypeStruct((m, d), F32), jax.ShapeDtypeStruct((m, d), BF16)],
        compiler_params=_params(("parallel",)),
        name="outproj",
    )(merged, w_out, x, norm2_w.reshape(1, d))


def _ffn_up_kernel(h_ref, wg_ref, wu_ref, o_ref):
    h = h_ref[...]
    g = jnp.dot(h, wg_ref[...], preferred_element_type=F32)
    u = jnp.dot(h, wu_ref[...], preferred_element_type=F32)
    o_ref[...] = (jax.nn.silu(g) * u).astype(o_ref.dtype)


def _ffn_up(h2, w_gate, w_up, bm=1024, bn=512):
    m, k = h2.shape
    n = w_gate.shape[1]
    return pl.pallas_call(
        _ffn_up_kernel,
        grid=(n // bn, m // bm),
        in_specs=[pl.BlockSpec((bm, k), lambda j, i: (i, 0)),
                  pl.BlockSpec((k, bn), lambda j, i: (0, j)),
                  pl.BlockSpec((k, bn), lambda j, i: (0, j))],
        out_specs=pl.BlockSpec((bm, bn), lambda j, i: (i, j)),
        out_shape=jax.ShapeDtypeStruct((m, n), BF16),
        compiler_params=_params(("parallel", "arbitrary")),
        name="ffn_up",
    )(h2, w_gate, w_up)


def _ffn_down_kernel(a_ref, w_ref, x_ref, o_ref):
    o_ref[...] = x_ref[...] + jnp.dot(a_ref[...], w_ref[...], preferred_element_type=F32)


def _ffn_down(act, w_down, x1, bm=512, bn=1024):
    m, k = act.shape
    n = w_down.shape[1]
    return pl.pallas_call(
        _ffn_down_kernel,
        grid=(n // bn, m // bm),
        in_specs=[pl.BlockSpec((bm, k), lambda j, i: (i, 0)),
                  pl.BlockSpec((k, bn), lambda j, i: (0, j)),
                  pl.BlockSpec((bm, bn), lambda j, i: (i, j))],
        out_specs=pl.BlockSpec((bm, bn), lambda j, i: (i, j)),
        out_shape=jax.ShapeDtypeStruct((m, n), F32),
        compiler_params=_params(("parallel", "arbitrary")),
        name="ffn_down",
    )(act, w_down, x1)


def _layer(x, norm1_w, w_in, q_norm_w, k_norm_w, conv_w, conv_b, w_rg_a, b_rg_a, w_rg_x, b_rg_x,
           lru_lambda, w_proj_attn, w_proj_lru, w_out, norm2_w, w_ffn_gate, w_ffn_up, w_ffn_down):
    d = x.shape[1]
    att_w = N_HEADS * HEAD_DIM
    lru_w = w_proj_lru.shape[0]
    w_in_b = w_in.astype(BF16)

    h = _rmsnorm(x, norm1_w)
    qkv = _qkv_proj(h, w_in_b, q_norm_w, k_norm_w, 3 * att_w)
    rest = _rest_proj(h, w_in_b, 3 * att_w, 2 * lru_w + 2 * d)

    head = jnp.arange(1, N_HEADS + 1, dtype=F32)
    slopes = jnp.broadcast_to(jnp.exp2(-8.0 * head / N_HEADS)[:, None, None], (N_HEADS, 1, MOBA_BLOCK))
    att = _moba_attention(qkv, slopes)

    lru = _rglru(rest, conv_w, conv_b, w_rg_a.astype(BF16), b_rg_a, w_rg_x.astype(BF16), b_rg_x,
                 lru_lambda, lru_w)

    merged = _merge(att, lru, w_proj_attn.astype(BF16), w_proj_lru.astype(BF16), rest, 2 * lru_w)
    x1, h2 = _outproj(merged, w_out.astype(BF16), x, norm2_w)
    act = _ffn_up(h2, w_ffn_gate.astype(BF16), w_ffn_up.astype(BF16))
    return _ffn_down(act, w_ffn_down.astype(BF16), x1)


def kernel(x, norm1_w, w_in, q_norm_w, k_norm_w, conv_w, conv_b, w_rg_a, b_rg_a, w_rg_x, b_rg_x,
           lru_lambda, w_proj_attn, w_proj_lru, w_out, norm2_w, w_ffn_gate, w_ffn_up, w_ffn_down):
    b, s, d = x.shape
    assert b == 1, "kernel handles the batch-1 prefill shape"
    y = x.reshape(s, d)
    for layer in range(norm1_w.shape[0]):
        y = _layer(y, norm1_w[layer], w_in[layer], q_norm_w[layer], k_norm_w[layer], conv_w[layer],
                   conv_b[layer], w_rg_a[layer], b_rg_a[layer], w_rg_x[layer], b_rg_x[layer],
                   lru_lambda[layer], w_proj_attn[layer], w_proj_lru[layer], w_out[layer],
                   norm2_w[layer], w_ffn_gate[layer], w_ffn_up[layer], w_ffn_down[layer])
    return y.reshape(b, s, d)
```

```python
import functools

import jax
import jax.numpy as jnp
from jax import lax
from jax.experimental import pallas as pl
from jax.experimental.pallas import tpu as pltpu

F32 = jnp.float32
BF16 = jnp.bfloat16

N_HEADS = 16
HEAD_DIM = 128
MOBA_BLOCK = 256
MOBA_TOPK = 3
LRU_BLOCK_W = 128
CONV_WIDTH = 4
LRU_C = 8.0
EPS = 1e-6
NEG_INF = -1e30
LOG2E = 1.4426950408889634

V7X_VMEM_BYTES = 64 * 1024 * 1024
VMEM_LIMIT = 56 * 1024 * 1024


def _params(semantics):
    return pltpu.CompilerParams(dimension_semantics=semantics, vmem_limit_bytes=VMEM_LIMIT)


def _rmsnorm_kernel(x_ref, w_ref, o_ref):
    x = x_ref[...]
    y = x * lax.rsqrt(jnp.mean(x * x, axis=-1, keepdims=True) + EPS)
    o_ref[...] = (y * w_ref[...]).astype(o_ref.dtype)


def _rmsnorm(x, w, tm=512):
    m, d = x.shape
    return pl.pallas_call(
        _rmsnorm_kernel,
        grid=(m // tm,),
        in_specs=[pl.BlockSpec((tm, d), lambda i: (i, 0)),
                  pl.BlockSpec((1, d), lambda i: (0, 0))],
        out_specs=pl.BlockSpec((tm, d), lambda i: (i, 0)),
        out_shape=jax.ShapeDtypeStruct((m, d), BF16),
        compiler_params=_params(("parallel",)),
        name="rmsnorm1",
    )(x, w.reshape(1, d))


def _load_weight(w_ref, wb_ref, row_axis=1):
    @pl.when(pl.program_id(row_axis) == 0)
    def _():
        wb_ref[...] = w_ref[...].astype(BF16)


def _qkv_kernel(h_ref, w_ref, qw_ref, kw_ref, o_ref, wb_ref, *, qk_tiles):
    j = pl.program_id(0)
    _load_weight(w_ref, wb_ref)
    acc = jnp.dot(h_ref[...], wb_ref[...], preferred_element_type=F32)
    bn = acc.shape[1]

    @pl.when(j < qk_tiles)
    def _():
        nw = jnp.where(j < qk_tiles // 2, qw_ref[...], kw_ref[...])
        for hh in range(bn // HEAD_DIM):
            a = acc[:, hh * HEAD_DIM:(hh + 1) * HEAD_DIM]
            y = a * lax.rsqrt(jnp.mean(a * a, axis=-1, keepdims=True) + EPS)
            o_ref[:, hh * HEAD_DIM:(hh + 1) * HEAD_DIM] = (y * nw).astype(o_ref.dtype)

    @pl.when(j >= qk_tiles)
    def _():
        o_ref[...] = acc.astype(o_ref.dtype)


def _qkv_proj(h, w_in, q_norm_w, k_norm_w, n_cols, bm=1024, bn=1024):
    m, k = h.shape
    return pl.pallas_call(
        functools.partial(_qkv_kernel, qk_tiles=(2 * N_HEADS * HEAD_DIM) // bn),
        grid=(n_cols // bn, m // bm),
        in_specs=[pl.BlockSpec((bm, k), lambda j, i: (i, 0)),
                  pl.BlockSpec((k, bn), lambda j, i: (0, j)),
                  pl.BlockSpec((1, HEAD_DIM), lambda j, i: (0, 0)),
                  pl.BlockSpec((1, HEAD_DIM), lambda j, i: (0, 0))],
        out_specs=pl.BlockSpec((bm, bn), lambda j, i: (i, j)),
        out_shape=jax.ShapeDtypeStruct((m, n_cols), BF16),
        scratch_shapes=[pltpu.VMEM((k, bn), BF16)],
        compiler_params=_params(("parallel", "arbitrary")),
        name="qkv_proj",
    )(h, w_in, q_norm_w.reshape(1, HEAD_DIM), k_norm_w.reshape(1, HEAD_DIM))


def _matmul_kernel(a_ref, w_ref, o_ref, wb_ref):
    _load_weight(w_ref, wb_ref)
    o_ref[...] = jnp.dot(a_ref[...], wb_ref[...], preferred_element_type=F32).astype(o_ref.dtype)


def _rest_proj(h, w_in, col0, n_cols, bm=1024, bn=1024):
    m, k = h.shape
    jb = col0 // bn
    return pl.pallas_call(
        _matmul_kernel,
        grid=(n_cols // bn, m // bm),
        in_specs=[pl.BlockSpec((bm, k), lambda j, i: (i, 0)),
                  pl.BlockSpec((k, bn), lambda j, i: (0, jb + j))],
        out_specs=pl.BlockSpec((bm, bn), lambda j, i: (i, j)),
        out_shape=jax.ShapeDtypeStruct((m, n_cols), F32),
        scratch_shapes=[pltpu.VMEM((k, bn), BF16)],
        compiler_params=_params(("parallel", "arbitrary")),
        name="rest_proj",
    )(h, w_in)


def _attn_prep_kernel(slope_ref, q_ref, k_ref, v_ref, qt_ref, vt_ref, rb_ref, kmean_ref,
                      *, n_blk, group, cols):
    blk = MOBA_BLOCK
    for jb in range(n_blk):
        rows = slice(jb * blk, (jb + 1) * blk)
        qt_ref[:, rows] = q_ref[rows, :].astype(F32).T.astype(BF16)
        c, g = divmod(jb, group)
        vt_ref[c, :, g * blk:(g + 1) * blk] = v_ref[rows, :].astype(F32).T.astype(BF16)
        kmean_ref[jb:jb + 1, :] = jnp.sum(k_ref[rows, :].astype(F32), axis=0, keepdims=True) * (1.0 / blk)
    kmean = kmean_ref[...].astype(BF16)
    slope = slope_ref[:, 0:1]
    row = lax.broadcasted_iota(jnp.int32, (n_blk, cols), 0)
    col = lax.broadcasted_iota(jnp.int32, (n_blk, cols), 1)
    for qc in range(qt_ref.shape[1] // cols):
        csl = slice(qc * cols, (qc + 1) * cols)
        gate = jnp.dot(kmean, qt_ref[:, csl], preferred_element_type=F32)
        qblk = lax.shift_right_logical(col + qc * cols, blk.bit_length() - 1)
        g = jnp.where(row < qblk, gate, NEG_INF)
        bias = jnp.full((n_blk, cols), NEG_INF, F32)
        for r in range(MOBA_TOPK):
            mx = jnp.max(g, axis=0, keepdims=True)
            idx = jnp.min(jnp.where(g == mx, row, n_blk), axis=0, keepdims=True)
            pick = row == idx
            bias = jnp.where(pick, jnp.where(qblk > r, 0.0, NEG_INF), bias)
            g = jnp.where(pick, -jnp.inf, g)
        past = bias - slope * ((qblk - row) * blk).astype(F32)
        rb = jnp.where(row == qblk, 0.0, jnp.where(row < qblk, past, NEG_INF))
        rb_ref[:, csl] = rb * LOG2E


def _attn_kernel(slope_ref, qt_ref, k_ref, vt_ref, rb_ref, o_ref,
                 u_ref, rbs_ref, m_ref, dmat_ref, acc_ref, *, group, n_blk):
    blk = MOBA_BLOCK
    cb = group * blk
    lg = group.bit_length() - 1
    i = pl.program_id(1)
    slot = lax.rem(i, 2)
    prev = 1 - slot

    @pl.when(i == 0)
    def _per_head_setup():
        kk = lax.broadcasted_iota(jnp.int32, (blk, blk), 0)
        qq = lax.broadcasted_iota(jnp.int32, (blk, blk), 1)
        d = slope_ref[:, 0:1] * (qq - kk).astype(F32) * LOG2E
        dmat_ref[0] = d
        dmat_ref[1] = jnp.where(kk <= qq, d, -NEG_INF)
        m_ref[...] = jnp.zeros_like(m_ref)

    qt = qt_ref[...]
    k2 = (HEAD_DIM ** -0.5) * LOG2E
    n1 = jnp.where(i < n_blk, lax.shift_right_logical(i + group, lg), 0)
    n2 = lax.shift_right_logical(i + group - 1, lg)
    m_prev = m_ref[prev]

    def pass1(c, mx):
        r0 = pl.multiple_of(c * cb, cb)
        s = jnp.dot(k_ref[pl.ds(r0, cb), :], qt, preferred_element_type=F32)
        for g in range(group):
            j = c * group + g
            own = (j == i).astype(jnp.int32)
            u = s[g * blk:(g + 1) * blk, :] * k2 - dmat_ref[own]
            u_ref[slot, c, g * blk:(g + 1) * blk, :] = u
            mx = jnp.maximum(mx, jnp.max(u, axis=0, keepdims=True) + rb_ref[pl.ds(j, 1), :])
        return mx

    def pass2(c, l):
        ps = []
        for g in range(group):
            j = c * group + g
            p = jnp.exp2(u_ref[prev, c, g * blk:(g + 1) * blk, :]
                         + (rbs_ref[prev, pl.ds(j, 1), :] - m_prev))
            l = l + jnp.sum(p, axis=0, keepdims=True)
            ps.append(p.astype(BF16))
        acc_ref[...] += jnp.dot(vt_ref[c], jnp.concatenate(ps, axis=0), preferred_element_type=F32)
        return l

    def both(c, carry):
        l = pass2(c, carry[1])
        return pass1(c, carry[0]), l

    acc_ref[...] = jnp.zeros_like(acc_ref)
    common = jnp.minimum(n1, n2)
    mx, l = lax.fori_loop(0, common, both,
                          (jnp.full((1, blk), -jnp.inf, F32), jnp.zeros((1, blk), F32)))
    mx = lax.fori_loop(common, n1, pass1, mx)
    l = lax.fori_loop(common, n2, pass2, l)
    m_ref[slot] = mx
    rbs_ref[slot] = rb_ref[...]

    @pl.when(i > 0)
    def _():
        o_ref[...] = (acc_ref[...] / l).T.astype(o_ref.dtype)


def _moba_attention(qkv, slopes, group=4, cols=2048):
    s = qkv.shape[0]
    blk = MOBA_BLOCK
    n_blk = s // blk
    n_chunk = n_blk // group
    qt, vt, rb = pl.pallas_call(
        functools.partial(_attn_prep_kernel, n_blk=n_blk, group=group, cols=cols),
        grid=(N_HEADS,),
        in_specs=[pl.BlockSpec((None, 1, blk), lambda h: (h, 0, 0)),
                  pl.BlockSpec((s, HEAD_DIM), lambda h: (0, h)),
                  pl.BlockSpec((s, HEAD_DIM), lambda h: (0, N_HEADS + h)),
                  pl.BlockSpec((s, HEAD_DIM), lambda h: (0, 2 * N_HEADS + h))],
        out_specs=[pl.BlockSpec((None, HEAD_DIM, s), lambda h: (h, 0, 0)),
                   pl.BlockSpec((None, n_chunk, HEAD_DIM, group * blk), lambda h: (h, 0, 0, 0)),
                   pl.BlockSpec((None, n_blk, s), lambda h: (h, 0, 0))],
        out_shape=[jax.ShapeDtypeStruct((N_HEADS, HEAD_DIM, s), BF16),
                   jax.ShapeDtypeStruct((N_HEADS, n_chunk, HEAD_DIM, group * blk), BF16),
                   jax.ShapeDtypeStruct((N_HEADS, n_blk, s), F32)],
        scratch_shapes=[pltpu.VMEM((n_blk, HEAD_DIM), F32)],
        compiler_params=_params(("parallel",)),
        name="moba_prep",
    )(slopes, qkv, qkv, qkv)
    last = n_blk - 1
    return pl.pallas_call(
        functools.partial(_attn_kernel, group=group, n_blk=n_blk),
        grid=(N_HEADS, n_blk + 1),
        in_specs=[pl.BlockSpec((None, 1, blk), lambda h, i: (h, 0, 0)),
                  pl.BlockSpec((None, HEAD_DIM, blk), lambda h, i: (h, 0, jnp.minimum(i, last))),
                  pl.BlockSpec((s, HEAD_DIM), lambda h, i: (0, N_HEADS + h)),
                  pl.BlockSpec((None, n_chunk, HEAD_DIM, group * blk), lambda h, i: (h, 0, 0, 0)),
                  pl.BlockSpec((None, n_blk, blk), lambda h, i: (h, 0, jnp.minimum(i, last)))],
        out_specs=pl.BlockSpec((blk, HEAD_DIM), lambda h, i: (jnp.maximum(i - 1, 0), h)),
        out_shape=jax.ShapeDtypeStruct((s, N_HEADS * HEAD_DIM), BF16),
        scratch_shapes=[pltpu.VMEM((2, n_chunk, group * blk, blk), F32),
                        pltpu.VMEM((2, n_blk, blk), F32),
                        pltpu.VMEM((2, 1, blk), F32),
                        pltpu.VMEM((2, blk, blk), F32),
                        pltpu.VMEM((HEAD_DIM, blk), F32)],
        compiler_params=_params(("parallel", "arbitrary")),
        name="moba_attention",
    )(slopes, qt, qkv, vt, rb)


def _lru_kernel(xr_ref, yr_ref, cw_ref, cb_ref, wa_ref, ba_ref, wx_ref, bx_ref, lam_ref, o_ref,
                xbuf_ref, a_ref, b_ref, h_ref, *, ts, tc):
    t = pl.program_id(1)
    pad = 8

    @pl.when(t == 0)
    def _():
        xbuf_ref[0:pad, :] = jnp.zeros((pad, tc), F32)
        h_ref[...] = jnp.zeros_like(h_ref)

    xbuf_ref[pad:pad + ts, :] = xr_ref[...]
    cw = cw_ref[...]
    u = cb_ref[...]
    for tap in range(CONV_WIDTH):
        off = pad - (CONV_WIDTH - 1) + tap
        u = u + xbuf_ref[off:off + ts, :] * cw[tap:tap + 1, :]
    xbuf_ref[0:pad, :] = xbuf_ref[ts:ts + pad, :]

    ub = u.astype(BF16)
    ga, gx = [], []
    for n in range(tc // LRU_BLOCK_W):
        un = ub[:, n * LRU_BLOCK_W:(n + 1) * LRU_BLOCK_W]
        ga.append(jnp.dot(un, wa_ref[n].astype(BF16), preferred_element_type=F32))
        gx.append(jnp.dot(un, wx_ref[n].astype(BF16), preferred_element_type=F32))
    r = jax.nn.sigmoid(jnp.concatenate(ga, axis=1) + ba_ref[...])
    ig = jax.nn.sigmoid(jnp.concatenate(gx, axis=1) + bx_ref[...])
    log_a = -LRU_C * r * jax.nn.softplus(-lam_ref[...])
    a = jnp.exp(log_a)
    a_ref[...] = a
    one_minus_a2 = -jnp.tanh(log_a) * (a * a + 1.0)
    b_ref[...] = jnp.sqrt(one_minus_a2) * ig * u

    row = lax.broadcasted_iota(jnp.int32, (8, tc), 0)

    def group(g, hprev):
        r0 = pl.multiple_of(g * 8, 8)
        av = a_ref[pl.ds(r0, 8), :]
        bv = b_ref[pl.ds(r0, 8), :]
        for d in (1, 2, 4):
            keep = row >= d
            a_sh = pltpu.roll(av, d, 0)
            b_sh = pltpu.roll(bv, d, 0)
            bv = jnp.where(keep, av * b_sh + bv, bv)
            av = jnp.where(keep, av * a_sh, av)
        hv = av * hprev + bv
        b_ref[pl.ds(r0, 8), :] = hv
        return jnp.broadcast_to(hv[7:8, :], (8, tc))

    h_ref[...] = lax.fori_loop(0, ts // 8, group, h_ref[...])
    o_ref[...] = (b_ref[...] * jax.nn.gelu(yr_ref[...])).astype(o_ref.dtype)


def _rglru(rest, conv_w, conv_b, w_rg_a, b_rg_a, w_rg_x, b_rg_x, lru_lambda, width, ts=256, tc=512):
    s = rest.shape[0]
    nct = width // tc
    nb = tc // LRU_BLOCK_W
    vec = lambda v: v.reshape(1, width)
    vspec = pl.BlockSpec((1, tc), lambda c, t: (0, c))
    wspec = pl.BlockSpec((nb, LRU_BLOCK_W, LRU_BLOCK_W), lambda c, t: (c, 0, 0))
    return pl.pallas_call(
        functools.partial(_lru_kernel, ts=ts, tc=tc),
        grid=(nct, s // ts),
        in_specs=[pl.BlockSpec((ts, tc), lambda c, t: (t, c)),
                  pl.BlockSpec((ts, tc), lambda c, t: (t, nct + c)),
                  pl.BlockSpec((CONV_WIDTH, tc), lambda c, t: (0, c)),
                  vspec, wspec, vspec, wspec, vspec, vspec],
        out_specs=pl.BlockSpec((ts, tc), lambda c, t: (t, c)),
        out_shape=jax.ShapeDtypeStruct((s, width), BF16),
        scratch_shapes=[pltpu.VMEM((ts + 8, tc), F32),
                        pltpu.VMEM((ts, tc), F32),
                        pltpu.VMEM((ts, tc), F32),
                        pltpu.VMEM((8, tc), F32)],
        compiler_params=_params(("parallel", "arbitrary")),
        name="rglru",
    )(rest, rest, conv_w, vec(conv_b), w_rg_a, vec(b_rg_a), w_rg_x, vec(b_rg_x), vec(lru_lambda))


def _merge_kernel(att_ref, lru_ref, wa_ref, wl_ref, ga_ref, gl_ref, o_ref, wab_ref, wlb_ref):
    _load_weight(wa_ref, wab_ref)
    _load_weight(wl_ref, wlb_ref)
    pa = jnp.dot(att_ref[...], wab_ref[...], preferred_element_type=F32)
    plru = jnp.dot(lru_ref[...], wlb_ref[...], preferred_element_type=F32)
    o_ref[...] = (jax.nn.sigmoid(ga_ref[...]) * pa + jax.nn.sigmoid(gl_ref[...]) * plru).astype(o_ref.dtype)


def _merge(att, lru, w_att, w_lru, rest, gate_col0, bm=512, bn=512):
    m, k = att.shape
    n = w_att.shape[1]
    ga0 = gate_col0 // bn
    gl0 = (gate_col0 + n) // bn
    return pl.pallas_call(
        _merge_kernel,
        grid=(n // bn, m // bm),
        in_specs=[pl.BlockSpec((bm, k), lambda j, i: (i, 0)),
                  pl.BlockSpec((bm, k), lambda j, i: (i, 0)),
                  pl.BlockSpec((k, bn), lambda j, i: (0, j)),
                  pl.BlockSpec((k, bn), lambda j, i: (0, j)),
                  pl.BlockSpec((bm, bn), lambda j, i: (i, ga0 + j)),
                  pl.BlockSpec((bm, bn), lambda j, i: (i, gl0 + j))],
        out_specs=pl.BlockSpec((bm, bn), lambda j, i: (i, j)),
        out_shape=jax.ShapeDtypeStruct((m, n), BF16),
        scratch_shapes=[pltpu.VMEM((k, bn), BF16), pltpu.VMEM((k, bn), BF16)],
        compiler_params=_params(("parallel", "arbitrary")),
        name="merge",
    )(att, lru, w_att, w_lru, rest, rest)


def _outproj_kernel(a_ref, w_ref, x_ref, nw_ref, x1_ref, h2_ref, wb_ref):
    _load_weight(w_ref, wb_ref, row_axis=0)
    x1 = x_ref[...] + jnp.dot(a_ref[...], wb_ref[...], preferred_element_type=F32)
    x1_ref[...] = x1
    y = x1 * lax.rsqrt(jnp.mean(x1 * x1, axis=-1, keepdims=True) + EPS)
    h2_ref[...] = (y * nw_ref[...]).astype(h2_ref.dtype)


def _outproj(merged, w_out, x, norm2_w, bm=512):
    m, k = merged.shape
    d = w_out.shape[1]
    return pl.pallas_call(
        _outproj_kernel,
        grid=(m // bm,),
        in_specs=[pl.BlockSpec((bm, k), lambda i: (i, 0)),
                  pl.BlockSpec((k, d), lambda i: (0, 0), pipeline_mode=pl.Buffered(1)),
                  pl.BlockSpec((bm, d), lambda i: (i, 0)),
                  pl.BlockSpec((1, d), lambda i: (0, 0))],
        out_specs=[pl.BlockSpec((bm, d), lambda i: (i, 0)),
                   pl.BlockSpec((bm, d), lambda i: (i, 0))],
        out_shape=[jax.ShapeDtypeStruct((m, d), F32), jax.ShapeDtypeStruct((m, d), BF16)],
        scratch_shapes=[pltpu.VMEM((k, d), BF16)],
        compiler_params=_params(("arbitrary",)),
        name="outproj",
    )(merged, w_out, x, norm2_w.reshape(1, d))


def _ffn_up_kernel(h_ref, wg_ref, wu_ref, o_ref, wgb_ref, wub_ref):
    _load_weight(wg_ref, wgb_ref)
    _load_weight(wu_ref, wub_ref)
    h = h_ref[...]
    g = jnp.dot(h, wgb_ref[...], preferred_element_type=F32)
    u = jnp.dot(h, wub_ref[...], preferred_element_type=F32)
    o_ref[...] = (jax.nn.silu(g) * u).astype(o_ref.dtype)


def _ffn_up(h2, w_gate, w_up, bm=1024, bn=512):
    m, k = h2.shape
    n = w_gate.shape[1]
    return pl.pallas_call(
        _ffn_up_kernel,
        grid=(n // bn, m // bm),
        in_specs=[pl.BlockSpec((bm, k), lambda j, i: (i, 0)),
                  pl.BlockSpec((k, bn), lambda j, i: (0, j)),
                  pl.BlockSpec((k, bn), lambda j, i: (0, j))],
        out_specs=pl.BlockSpec((bm, bn), lambda j, i: (i, j)),
        out_shape=jax.ShapeDtypeStruct((m, n), BF16),
        scratch_shapes=[pltpu.VMEM((k, bn), BF16), pltpu.VMEM((k, bn), BF16)],
        compiler_params=_params(("parallel", "arbitrary")),
        name="ffn_up",
    )(h2, w_gate, w_up)


def _ffn_down_kernel(a_ref, w_ref, x_ref, o_ref, wb_ref):
    _load_weight(w_ref, wb_ref)
    o_ref[...] = x_ref[...] + jnp.dot(a_ref[...], wb_ref[...], preferred_element_type=F32)


def _ffn_down(act, w_down, x1, bm=512, bn=512):
    m, k = act.shape
    n = w_down.shape[1]
    return pl.pallas_call(
        _ffn_down_kernel,
        grid=(n // bn, m // bm),
        in_specs=[pl.BlockSpec((bm, k), lambda j, i: (i, 0)),
                  pl.BlockSpec((k, bn), lambda j, i: (0, j)),
                  pl.BlockSpec((bm, bn), lambda j, i: (i, j))],
        out_specs=pl.BlockSpec((bm, bn), lambda j, i: (i, j)),
        out_shape=jax.ShapeDtypeStruct((m, n), F32),
        scratch_shapes=[pltpu.VMEM((k, bn), BF16)],
        compiler_params=_params(("parallel", "arbitrary")),
        name="ffn_down",
    )(act, w_down, x1)


def _layer(x, norm1_w, w_in, q_norm_w, k_norm_w, conv_w, conv_b, w_rg_a, b_rg_a, w_rg_x, b_rg_x,
           lru_lambda, w_proj_attn, w_proj_lru, w_out, norm2_w, w_ffn_gate, w_ffn_up, w_ffn_down):
    d = x.shape[1]
    att_w = N_HEADS * HEAD_DIM
    lru_w = w_proj_lru.shape[0]

    h = _rmsnorm(x, norm1_w)
    qkv = _qkv_proj(h, w_in, q_norm_w, k_norm_w, 3 * att_w)
    rest = _rest_proj(h, w_in, 3 * att_w, 2 * lru_w + 2 * d)

    head = jnp.arange(1, N_HEADS + 1, dtype=F32)
    slopes = jnp.broadcast_to(jnp.exp2(-8.0 * head / N_HEADS)[:, None, None], (N_HEADS, 1, MOBA_BLOCK))
    att = _moba_attention(qkv, slopes)

    lru = _rglru(rest, conv_w, conv_b, w_rg_a, b_rg_a, w_rg_x, b_rg_x, lru_lambda, lru_w)

    merged = _merge(att, lru, w_proj_attn, w_proj_lru, rest, 2 * lru_w)
    x1, h2 = _outproj(merged, w_out, x, norm2_w)
    act = _ffn_up(h2, w_ffn_gate, w_ffn_up)
    return _ffn_down(act, w_ffn_down, x1)


def kernel(x, norm1_w, w_in, q_norm_w, k_norm_w, conv_w, conv_b, w_rg_a, b_rg_a, w_rg_x, b_rg_x,
           lru_lambda, w_proj_attn, w_proj_lru, w_out, norm2_w, w_ffn_gate, w_ffn_up, w_ffn_down):
    b, s, d = x.shape
    assert b == 1, "kernel handles the batch-1 prefill shape"
    y = x.reshape(s, d)
    for layer in range(norm1_w.shape[0]):
        y = _layer(y, norm1_w[layer], w_in[layer], q_norm_w[layer], k_norm_w[layer], conv_w[layer],
                   conv_b[layer], w_rg_a[layer], b_rg_a[layer], w_rg_x[layer], b_rg_x[layer],
                   lru_lambda[layer], w_proj_attn[layer], w_proj_lru[layer], w_out[layer],
                   norm2_w[layer], w_ffn_gate[layer], w_ffn_up[layer], w_ffn_down[layer])
    return y.reshape(b, s, d)
```

```python
import functools

import jax
import jax.numpy as jnp
from jax import lax
from jax.experimental import pallas as pl
from jax.experimental.pallas import tpu as pltpu

F32 = jnp.float32
BF16 = jnp.bfloat16

N_HEADS = 16
HEAD_DIM = 128
MOBA_BLOCK = 256
MOBA_TOPK = 3
LRU_BLOCK_W = 128
CONV_WIDTH = 4
LRU_C = 8.0
EPS = 1e-6
NEG_INF = -1e30
LOG2E = 1.4426950408889634
ONES_ROWS = 16

V7X_VMEM_BYTES = 64 * 1024 * 1024
VMEM_LIMIT = 56 * 1024 * 1024


def _params(semantics):
    return pltpu.CompilerParams(dimension_semantics=semantics, vmem_limit_bytes=VMEM_LIMIT)


def _rmsnorm_kernel(x_ref, w_ref, o_ref):
    x = x_ref[...]
    y = x * lax.rsqrt(jnp.mean(x * x, axis=-1, keepdims=True) + EPS)
    o_ref[...] = (y * w_ref[...]).astype(o_ref.dtype)


def _rmsnorm(x, w, tm=512):
    m, d = x.shape
    return pl.pallas_call(
        _rmsnorm_kernel,
        grid=(m // tm,),
        in_specs=[pl.BlockSpec((tm, d), lambda i: (i, 0)),
                  pl.BlockSpec((1, d), lambda i: (0, 0))],
        out_specs=pl.BlockSpec((tm, d), lambda i: (i, 0)),
        out_shape=jax.ShapeDtypeStruct((m, d), BF16),
        compiler_params=_params(("parallel",)),
        name="rmsnorm1",
    )(x, w.reshape(1, d))


def _load_weight(w_ref, wb_ref, row_axis=1):
    @pl.when(pl.program_id(row_axis) == 0)
    def _():
        wb_ref[...] = w_ref[...].astype(BF16)


def _qkv_kernel(h_ref, w_ref, qw_ref, kw_ref, o_ref, wb_ref, *, qk_tiles):
    j = pl.program_id(0)
    _load_weight(w_ref, wb_ref)
    acc = jnp.dot(h_ref[...], wb_ref[...], preferred_element_type=F32)
    bn = acc.shape[1]

    @pl.when(j < qk_tiles)
    def _():
        nw = jnp.where(j < qk_tiles // 2, qw_ref[...], kw_ref[...])
        for hh in range(bn // HEAD_DIM):
            a = acc[:, hh * HEAD_DIM:(hh + 1) * HEAD_DIM]
            y = a * lax.rsqrt(jnp.mean(a * a, axis=-1, keepdims=True) + EPS)
            o_ref[:, hh * HEAD_DIM:(hh + 1) * HEAD_DIM] = (y * nw).astype(o_ref.dtype)

    @pl.when(j >= qk_tiles)
    def _():
        o_ref[...] = acc.astype(o_ref.dtype)


def _qkv_proj(h, w_in, q_norm_w, k_norm_w, n_cols, bm=1024, bn=1024):
    m, k = h.shape
    return pl.pallas_call(
        functools.partial(_qkv_kernel, qk_tiles=(2 * N_HEADS * HEAD_DIM) // bn),
        grid=(n_cols // bn, m // bm),
        in_specs=[pl.BlockSpec((bm, k), lambda j, i: (i, 0)),
                  pl.BlockSpec((k, bn), lambda j, i: (0, j)),
                  pl.BlockSpec((1, HEAD_DIM), lambda j, i: (0, 0)),
                  pl.BlockSpec((1, HEAD_DIM), lambda j, i: (0, 0))],
        out_specs=pl.BlockSpec((bm, bn), lambda j, i: (i, j)),
        out_shape=jax.ShapeDtypeStruct((m, n_cols), BF16),
        scratch_shapes=[pltpu.VMEM((k, bn), BF16)],
        compiler_params=_params(("parallel", "arbitrary")),
        name="qkv_proj",
    )(h, w_in, q_norm_w.reshape(1, HEAD_DIM), k_norm_w.reshape(1, HEAD_DIM))


def _matmul_kernel(a_ref, w_ref, o_ref, wb_ref):
    _load_weight(w_ref, wb_ref)
    o_ref[...] = jnp.dot(a_ref[...], wb_ref[...], preferred_element_type=F32).astype(o_ref.dtype)


def _rest_proj(h, w_in, col0, n_cols, bm=1024, bn=1024):
    m, k = h.shape
    jb = col0 // bn
    return pl.pallas_call(
        _matmul_kernel,
        grid=(n_cols // bn, m // bm),
        in_specs=[pl.BlockSpec((bm, k), lambda j, i: (i, 0)),
                  pl.BlockSpec((k, bn), lambda j, i: (0, jb + j))],
        out_specs=pl.BlockSpec((bm, bn), lambda j, i: (i, j)),
        out_shape=jax.ShapeDtypeStruct((m, n_cols), F32),
        scratch_shapes=[pltpu.VMEM((k, bn), BF16)],
        compiler_params=_params(("parallel", "arbitrary")),
        name="rest_proj",
    )(h, w_in)


def _attn_prep_kernel(slope_ref, q_ref, k_ref, v_ref, qt_ref, vt_ref, rb_ref, kmean_ref,
                      *, n_blk, group, cols):
    blk = MOBA_BLOCK
    for jb in range(n_blk):
        rows = slice(jb * blk, (jb + 1) * blk)
        qt_ref[:, rows] = q_ref[rows, :].astype(F32).T.astype(BF16)
        c, g = divmod(jb, group)
        vt_ref[c, 0:HEAD_DIM, g * blk:(g + 1) * blk] = v_ref[rows, :].astype(F32).T.astype(BF16)
        vt_ref[c, HEAD_DIM:, g * blk:(g + 1) * blk] = jnp.ones((ONES_ROWS, blk), BF16)
        kmean_ref[jb:jb + 1, :] = jnp.sum(k_ref[rows, :].astype(F32), axis=0, keepdims=True) * (1.0 / blk)
    kmean = kmean_ref[...].astype(BF16)
    slope = slope_ref[:, 0:1]
    row = lax.broadcasted_iota(jnp.int32, (n_blk, cols), 0)
    col = lax.broadcasted_iota(jnp.int32, (n_blk, cols), 1)
    for qc in range(qt_ref.shape[1] // cols):
        csl = slice(qc * cols, (qc + 1) * cols)
        gate = jnp.dot(kmean, qt_ref[:, csl], preferred_element_type=F32)
        qblk = lax.shift_right_logical(col + qc * cols, blk.bit_length() - 1)
        g = jnp.where(row < qblk, gate, NEG_INF)
        bias = jnp.full((n_blk, cols), NEG_INF, F32)
        for r in range(MOBA_TOPK):
            mx = jnp.max(g, axis=0, keepdims=True)
            idx = jnp.min(jnp.where(g == mx, row, n_blk), axis=0, keepdims=True)
            pick = row == idx
            bias = jnp.where(pick, jnp.where(qblk > r, 0.0, NEG_INF), bias)
            g = jnp.where(pick, -jnp.inf, g)
        past = bias - slope * ((qblk - row) * blk).astype(F32)
        rb = jnp.where(row == qblk, 0.0, jnp.where(row < qblk, past, NEG_INF))
        rb_ref[:, csl] = rb * LOG2E


def _attn_kernel(slope_ref, qt_ref, k_ref, vt_ref, rb_ref, o_ref, *scratch, group, n_blk, heads):
    ua_refs, ub_refs, rbs_refs, m_refs, dmat_refs, acc_refs, p_refs = (
        scratch[n * heads:(n + 1) * heads] for n in range(7))
    blk = MOBA_BLOCK
    cb = group * blk
    lg = group.bit_length() - 1
    i = pl.program_id(1)
    slot = lax.rem(i, 2)
    prev = 1 - slot

    @pl.when(i == 0)
    def _per_head_setup():
        kk = lax.broadcasted_iota(jnp.int32, (blk, blk), 0)
        qq = lax.broadcasted_iota(jnp.int32, (blk, blk), 1)
        for hh in range(heads):
            d = slope_ref[hh][:, 0:1] * (qq - kk).astype(F32) * LOG2E
            dmat_refs[hh][0] = d
            dmat_refs[hh][1] = jnp.where(kk <= qq, d, -NEG_INF)
            m_refs[hh][...] = jnp.zeros_like(m_refs[hh])

    k2 = (HEAD_DIM ** -0.5) * LOG2E
    n1 = jnp.where(i < n_blk, lax.shift_right_logical(i + group, lg), 0)
    n2 = lax.shift_right_logical(i + group - 1, lg)
    common = jnp.minimum(n1, n2)
    m_prev = [m_refs[hh][prev] for hh in range(heads)]
    for hh in range(heads):
        acc_refs[hh][...] = jnp.zeros_like(acc_refs[hh])
        p_refs[hh][...] = jnp.zeros_like(p_refs[hh])

    def run(u_cur, u_prv):
        def pass1(hh, c, mx):
            r0 = pl.multiple_of(c * cb, cb)
            k_chunk = k_ref[pl.ds(r0, cb), hh * HEAD_DIM:(hh + 1) * HEAD_DIM]
            s = jnp.dot(k_chunk, qt_ref[hh], preferred_element_type=F32)
            for g in range(group):
                j = c * group + g
                own = (j == i).astype(jnp.int32)
                u = s[g * blk:(g + 1) * blk, :] * k2 - dmat_refs[hh][own]
                u_cur[hh][c, g * blk:(g + 1) * blk, :] = u
                mx = jnp.maximum(mx, jnp.max(u, axis=0, keepdims=True) + rb_ref[hh, pl.ds(j, 1), :])
            return mx

        def probs(hh, c):
            for g in range(group):
                j = c * group + g
                p = jnp.exp2(u_prv[hh][c, g * blk:(g + 1) * blk, :]
                             + (rbs_refs[hh][prev, pl.ds(j, 1), :] - m_prev[hh]))
                p_refs[hh][g * blk:(g + 1) * blk, :] = p.astype(BF16)

        def pv(hh, c):
            acc_refs[hh][...] += jnp.dot(vt_ref[hh, jnp.maximum(c - 1, 0)], p_refs[hh][...],
                                         preferred_element_type=F32)

        def only1(c, mxs):
            return tuple(pass1(hh, c, mxs[hh]) for hh in range(heads))

        def only2(c, carry):
            for hh in range(heads):
                pv(hh, c)
            for hh in range(heads):
                probs(hh, c)
            return carry

        def both(c, mxs):
            for hh in range(heads):
                pv(hh, c)
            mxs = only1(c, mxs)
            for hh in range(heads):
                probs(hh, c)
            return mxs

        mxs = lax.fori_loop(0, common, both,
                            tuple(jnp.full((1, blk), -jnp.inf, F32) for _ in range(heads)))
        mxs = lax.fori_loop(common, n1, only1, mxs)
        lax.fori_loop(common, n2, only2, 0)
        for hh in range(heads):
            pv(hh, n2)
            m_refs[hh][slot] = mxs[hh]

    @pl.when(slot == 0)
    def _():
        run(ua_refs, ub_refs)

    @pl.when(slot == 1)
    def _():
        run(ub_refs, ua_refs)

    for hh in range(heads):
        rbs_refs[hh][slot] = rb_ref[hh]

    @pl.when(i > 0)
    def _():
        for hh in range(heads):
            acc = acc_refs[hh][...]
            o = acc[0:HEAD_DIM, :] / acc[HEAD_DIM:HEAD_DIM + 1, :]
            o_ref[:, hh * HEAD_DIM:(hh + 1) * HEAD_DIM] = o.T.astype(o_ref.dtype)


def _moba_attention(qkv, slopes, group=4, cols=2048, heads=2):
    s = qkv.shape[0]
    blk = MOBA_BLOCK
    n_blk = s // blk
    n_chunk = n_blk // group
    vt_rows = HEAD_DIM + ONES_ROWS
    qt, vt, rb = pl.pallas_call(
        functools.partial(_attn_prep_kernel, n_blk=n_blk, group=group, cols=cols),
        grid=(N_HEADS,),
        in_specs=[pl.BlockSpec((None, 1, blk), lambda h: (h, 0, 0)),
                  pl.BlockSpec((s, HEAD_DIM), lambda h: (0, h)),
                  pl.BlockSpec((s, HEAD_DIM), lambda h: (0, N_HEADS + h)),
                  pl.BlockSpec((s, HEAD_DIM), lambda h: (0, 2 * N_HEADS + h))],
        out_specs=[pl.BlockSpec((None, HEAD_DIM, s), lambda h: (h, 0, 0)),
                   pl.BlockSpec((None, n_chunk, vt_rows, group * blk), lambda h: (h, 0, 0, 0)),
                   pl.BlockSpec((None, n_blk, s), lambda h: (h, 0, 0))],
        out_shape=[jax.ShapeDtypeStruct((N_HEADS, HEAD_DIM, s), BF16),
                   jax.ShapeDtypeStruct((N_HEADS, n_chunk, vt_rows, group * blk), BF16),
                   jax.ShapeDtypeStruct((N_HEADS, n_blk, s), F32)],
        scratch_shapes=[pltpu.VMEM((n_blk, HEAD_DIM), F32)],
        compiler_params=_params(("parallel",)),
        name="moba_prep",
    )(slopes, qkv, qkv, qkv)
    last = n_blk - 1
    hw = heads * HEAD_DIM
    k_col0 = (N_HEADS * HEAD_DIM) // hw
    once = pl.Buffered(1)
    per_head = lambda shape: [pltpu.VMEM(shape, F32) for _ in range(heads)]
    return pl.pallas_call(
        functools.partial(_attn_kernel, group=group, n_blk=n_blk, heads=heads),
        grid=(N_HEADS // heads, n_blk + 1),
        in_specs=[pl.BlockSpec((heads, 1, blk), lambda h, i: (h, 0, 0)),
                  pl.BlockSpec((heads, HEAD_DIM, blk), lambda h, i: (h, 0, jnp.minimum(i, last))),
                  pl.BlockSpec((s, hw), lambda h, i: (0, k_col0 + h), pipeline_mode=once),
                  pl.BlockSpec((heads, n_chunk, vt_rows, group * blk), lambda h, i: (h, 0, 0, 0),
                               pipeline_mode=once),
                  pl.BlockSpec((heads, n_blk, blk), lambda h, i: (h, 0, jnp.minimum(i, last)))],
        out_specs=pl.BlockSpec((blk, hw), lambda h, i: (jnp.maximum(i - 1, 0), h)),
        out_shape=jax.ShapeDtypeStruct((s, N_HEADS * HEAD_DIM), BF16),
        scratch_shapes=(per_head((n_chunk, group * blk, blk))
                        + per_head((n_chunk, group * blk, blk))
                        + per_head((2, n_blk, blk))
                        + per_head((2, 1, blk))
                        + per_head((2, blk, blk))
                        + per_head((vt_rows, blk))
                        + [pltpu.VMEM((group * blk, blk), BF16) for _ in range(heads)]),
        compiler_params=_params(("parallel", "arbitrary")),
        name="moba_attention",
    )(slopes, qt, qkv, vt, rb)


def _lru_kernel(xr_ref, yr_ref, cw_ref, cb_ref, wa_ref, ba_ref, wx_ref, bx_ref, lam_ref, o_ref,
                xbuf_ref, a_ref, b_ref, h_ref, *, ts, tc):
    t = pl.program_id(1)
    pad = 8

    @pl.when(t == 0)
    def _():
        xbuf_ref[0:pad, :] = jnp.zeros((pad, tc), F32)
        h_ref[...] = jnp.zeros_like(h_ref)

    xbuf_ref[pad:pad + ts, :] = xr_ref[...]
    cw = cw_ref[...]
    u = cb_ref[...]
    for tap in range(CONV_WIDTH):
        off = pad - (CONV_WIDTH - 1) + tap
        u = u + xbuf_ref[off:off + ts, :] * cw[tap:tap + 1, :]
    xbuf_ref[0:pad, :] = xbuf_ref[ts:ts + pad, :]

    ub = u.astype(BF16)
    ga, gx = [], []
    for n in range(tc // LRU_BLOCK_W):
        un = ub[:, n * LRU_BLOCK_W:(n + 1) * LRU_BLOCK_W]
        ga.append(jnp.dot(un, wa_ref[n].astype(BF16), preferred_element_type=F32))
        gx.append(jnp.dot(un, wx_ref[n].astype(BF16), preferred_element_type=F32))
    r = jax.nn.sigmoid(jnp.concatenate(ga, axis=1) + ba_ref[...])
    ig = jax.nn.sigmoid(jnp.concatenate(gx, axis=1) + bx_ref[...])
    log_a = -LRU_C * r * jax.nn.softplus(-lam_ref[...])
    a = jnp.exp(log_a)
    a_ref[...] = a
    one_minus_a2 = -jnp.tanh(log_a) * (a * a + 1.0)
    b_ref[...] = jnp.sqrt(one_minus_a2) * ig * u

    row = lax.broadcasted_iota(jnp.int32, (8, tc), 0)

    def group(g, hprev):
        r0 = pl.multiple_of(g * 8, 8)
        av = a_ref[pl.ds(r0, 8), :]
        bv = b_ref[pl.ds(r0, 8), :]
        for d in (1, 2, 4):
            keep = row >= d
            a_sh = pltpu.roll(av, d, 0)
            b_sh = pltpu.roll(bv, d, 0)
            bv = jnp.where(keep, av * b_sh + bv, bv)
            av = jnp.where(keep, av * a_sh, av)
        hv = av * hprev + bv
        b_ref[pl.ds(r0, 8), :] = hv
        return jnp.broadcast_to(hv[7:8, :], (8, tc))

    h_ref[...] = lax.fori_loop(0, ts // 8, group, h_ref[...])
    o_ref[...] = (b_ref[...] * jax.nn.gelu(yr_ref[...])).astype(o_ref.dtype)


def _rglru(rest, conv_w, conv_b, w_rg_a, b_rg_a, w_rg_x, b_rg_x, lru_lambda, width, ts=256, tc=512):
    s = rest.shape[0]
    nct = width // tc
    nb = tc // LRU_BLOCK_W
    vec = lambda v: v.reshape(1, width)
    vspec = pl.BlockSpec((1, tc), lambda c, t: (0, c))
    wspec = pl.BlockSpec((nb, LRU_BLOCK_W, LRU_BLOCK_W), lambda c, t: (c, 0, 0))
    return pl.pallas_call(
        functools.partial(_lru_kernel, ts=ts, tc=tc),
        grid=(nct, s // ts),
        in_specs=[pl.BlockSpec((ts, tc), lambda c, t: (t, c)),
                  pl.BlockSpec((ts, tc), lambda c, t: (t, nct + c)),
                  pl.BlockSpec((CONV_WIDTH, tc), lambda c, t: (0, c)),
                  vspec, wspec, vspec, wspec, vspec, vspec],
        out_specs=pl.BlockSpec((ts, tc), lambda c, t: (t, c)),
        out_shape=jax.ShapeDtypeStruct((s, width), BF16),
        scratch_shapes=[pltpu.VMEM((ts + 8, tc), F32),
                        pltpu.VMEM((ts, tc), F32),
                        pltpu.VMEM((ts, tc), F32),
                        pltpu.VMEM((8, tc), F32)],
        compiler_params=_params(("parallel", "arbitrary")),
        name="rglru",
    )(rest, rest, conv_w, vec(conv_b), w_rg_a, vec(b_rg_a), w_rg_x, vec(b_rg_x), vec(lru_lambda))


def _merge_kernel(att_ref, lru_ref, wa_ref, wl_ref, ga_ref, gl_ref, o_ref, wab_ref, wlb_ref):
    _load_weight(wa_ref, wab_ref)
    _load_weight(wl_ref, wlb_ref)
    pa = jnp.dot(att_ref[...], wab_ref[...], preferred_element_type=F32)
    plru = jnp.dot(lru_ref[...], wlb_ref[...], preferred_element_type=F32)
    o_ref[...] = (jax.nn.sigmoid(ga_ref[...]) * pa + jax.nn.sigmoid(gl_ref[...]) * plru).astype(o_ref.dtype)


def _merge(att, lru, w_att, w_lru, rest, gate_col0, bm=512, bn=512):
    m, k = att.shape
    n = w_att.shape[1]
    ga0 = gate_col0 // bn
    gl0 = (gate_col0 + n) // bn
    return pl.pallas_call(
        _merge_kernel,
        grid=(n // bn, m // bm),
        in_specs=[pl.BlockSpec((bm, k), lambda j, i: (i, 0)),
                  pl.BlockSpec((bm, k), lambda j, i: (i, 0)),
                  pl.BlockSpec((k, bn), lambda j, i: (0, j)),
                  pl.BlockSpec((k, bn), lambda j, i: (0, j)),
                  pl.BlockSpec((bm, bn), lambda j, i: (i, ga0 + j)),
                  pl.BlockSpec((bm, bn), lambda j, i: (i, gl0 + j))],
        out_specs=pl.BlockSpec((bm, bn), lambda j, i: (i, j)),
        out_shape=jax.ShapeDtypeStruct((m, n), BF16),
        scratch_shapes=[pltpu.VMEM((k, bn), BF16), pltpu.VMEM((k, bn), BF16)],
        compiler_params=_params(("parallel", "arbitrary")),
        name="merge",
    )(att, lru, w_att, w_lru, rest, rest)


def _outproj_kernel(a_ref, w_ref, x_ref, nw_ref, x1_ref, h2_ref, wb_ref):
    _load_weight(w_ref, wb_ref, row_axis=0)
    x1 = x_ref[...] + jnp.dot(a_ref[...], wb_ref[...], preferred_element_type=F32)
    x1_ref[...] = x1
    y = x1 * lax.rsqrt(jnp.mean(x1 * x1, axis=-1, keepdims=True) + EPS)
    h2_ref[...] = (y * nw_ref[...]).astype(h2_ref.dtype)


def _outproj(merged, w_out, x, norm2_w, bm=512):
    m, k = merged.shape
    d = w_out.shape[1]
    return pl.pallas_call(
        _outproj_kernel,
        grid=(m // bm,),
        in_specs=[pl.BlockSpec((bm, k), lambda i: (i, 0)),
                  pl.BlockSpec((k, d), lambda i: (0, 0), pipeline_mode=pl.Buffered(1)),
                  pl.BlockSpec((bm, d), lambda i: (i, 0)),
                  pl.BlockSpec((1, d), lambda i: (0, 0))],
        out_specs=[pl.BlockSpec((bm, d), lambda i: (i, 0)),
                   pl.BlockSpec((bm, d), lambda i: (i, 0))],
        out_shape=[jax.ShapeDtypeStruct((m, d), F32), jax.ShapeDtypeStruct((m, d), BF16)],
        scratch_shapes=[pltpu.VMEM((k, d), BF16)],
        compiler_params=_params(("arbitrary",)),
        name="outproj",
    )(merged, w_out, x, norm2_w.reshape(1, d))


def _ffn_up_kernel(h_ref, wg_ref, wu_ref, o_ref, wgb_ref, wub_ref):
    _load_weight(wg_ref, wgb_ref)
    _load_weight(wu_ref, wub_ref)
    h = h_ref[...]
    g = jnp.dot(h, wgb_ref[...], preferred_element_type=F32)
    u = jnp.dot(h, wub_ref[...], preferred_element_type=F32)
    o_ref[...] = (jax.nn.silu(g) * u).astype(o_ref.dtype)


def _ffn_up(h2, w_gate, w_up, bm=1024, bn=512):
    m, k = h2.shape
    n = w_gate.shape[1]
    return pl.pallas_call(
        _ffn_up_kernel,
        grid=(n // bn, m // bm),
        in_specs=[pl.BlockSpec((bm, k), lambda j, i: (i, 0)),
                  pl.BlockSpec((k, bn), lambda j, i: (0, j)),
                  pl.BlockSpec((k, bn), lambda j, i: (0, j))],
        out_specs=pl.BlockSpec((bm, bn), lambda j, i: (i, j)),
        out_shape=jax.ShapeDtypeStruct((m, n), BF16),
        scratch_shapes=[pltpu.VMEM((k, bn), BF16), pltpu.VMEM((k, bn), BF16)],
        compiler_params=_params(("parallel", "arbitrary")),
        name="ffn_up",
    )(h2, w_gate, w_up)


def _ffn_down_kernel(a_ref, w_ref, x_ref, o_ref, wb_ref):
    _load_weight(w_ref, wb_ref)
    o_ref[...] = x_ref[...] + jnp.dot(a_ref[...], wb_ref[...], preferred_element_type=F32)


def _ffn_down(act, w_down, x1, bm=512, bn=512):
    m, k = act.shape
    n = w_down.shape[1]
    return pl.pallas_call(
        _ffn_down_kernel,
        grid=(n // bn, m // bm),
        in_specs=[pl.BlockSpec((bm, k), lambda j, i: (i, 0)),
                  pl.BlockSpec((k, bn), lambda j, i: (0, j)),
                  pl.BlockSpec((bm, bn), lambda j, i: (i, j))],
        out_specs=pl.BlockSpec((bm, bn), lambda j, i: (i, j)),
        out_shape=jax.ShapeDtypeStruct((m, n), F32),
        scratch_shapes=[pltpu.VMEM((k, bn), BF16)],
        compiler_params=_params(("parallel", "arbitrary")),
        name="ffn_down",
    )(act, w_down, x1)


def _layer(x, norm1_w, w_in, q_norm_w, k_norm_w, conv_w, conv_b, w_rg_a, b_rg_a, w_rg_x, b_rg_x,
           lru_lambda, w_proj_attn, w_proj_lru, w_out, norm2_w, w_ffn_gate, w_ffn_up, w_ffn_down):
    d = x.shape[1]
    att_w = N_HEADS * HEAD_DIM
    lru_w = w_proj_lru.shape[0]

    h = _rmsnorm(x, norm1_w)
    qkv = _qkv_proj(h, w_in, q_norm_w, k_norm_w, 3 * att_w)
    rest = _rest_proj(h, w_in, 3 * att_w, 2 * lru_w + 2 * d)

    head = jnp.arange(1, N_HEADS + 1, dtype=F32)
    slopes = jnp.broadcast_to(jnp.exp2(-8.0 * head / N_HEADS)[:, None, None], (N_HEADS, 1, MOBA_BLOCK))
    att = _moba_attention(qkv, slopes)

    lru = _rglru(rest, conv_w, conv_b, w_rg_a, b_rg_a, w_rg_x, b_rg_x, lru_lambda, lru_w)

    merged = _merge(att, lru, w_proj_attn, w_proj_lru, rest, 2 * lru_w)
    x1, h2 = _outproj(merged, w_out, x, norm2_w)
    act = _ffn_up(h2, w_ffn_gate, w_ffn_up)
    return _ffn_down(act, w_ffn_down, x1)


def kernel(x, norm1_w, w_in, q_norm_w, k_norm_w, conv_w, conv_b, w_rg_a, b_rg_a, w_rg_x, b_rg_x,
           lru_lambda, w_proj_attn, w_proj_lru, w_out, norm2_w, w_ffn_gate, w_ffn_up, w_ffn_down):
    b, s, d = x.shape
    assert b == 1, "kernel handles the batch-1 prefill shape"
    y = x.reshape(s, d)
    for layer in range(norm1_w.shape[0]):
        y = _layer(y, norm1_w[layer], w_in[layer], q_norm_w[layer], k_norm_w[layer], conv_w[layer],
                   conv_b[layer], w_rg_a[layer], b_rg_a[layer], w_rg_x[layer], b_rg_x[layer],
                   lru_lambda[layer], w_proj_attn[layer], w_proj_lru[layer], w_out[layer],
                   norm2_w[layer], w_ffn_gate[layer], w_ffn_up[layer], w_ffn_down[layer])
    return y.reshape(b, s, d)
```

```python
import functools

import jax
import jax.numpy as jnp
from jax import lax
from jax.experimental import pallas as pl
from jax.experimental.pallas import tpu as pltpu

F32 = jnp.float32
BF16 = jnp.bfloat16

N_HEADS = 16
HEAD_DIM = 128
MOBA_BLOCK = 256
MOBA_TOPK = 3
LRU_BLOCK_W = 128
CONV_WIDTH = 4
LRU_C = 8.0
EPS = 1e-6
NEG_INF = -1e30
LOG2E = 1.4426950408889634
ONES_ROWS = 16

V7X_VMEM_BYTES = 64 * 1024 * 1024
VMEM_LIMIT = 56 * 1024 * 1024


def _params(semantics):
    return pltpu.CompilerParams(dimension_semantics=semantics, vmem_limit_bytes=VMEM_LIMIT)


def _rmsnorm_kernel(x_ref, w_ref, o_ref):
    x = x_ref[...]
    y = x * lax.rsqrt(jnp.mean(x * x, axis=-1, keepdims=True) + EPS)
    o_ref[...] = (y * w_ref[...]).astype(o_ref.dtype)


def _rmsnorm(x, w, tm=512):
    m, d = x.shape
    return pl.pallas_call(
        _rmsnorm_kernel,
        grid=(m // tm,),
        in_specs=[pl.BlockSpec((tm, d), lambda i: (i, 0)),
                  pl.BlockSpec((1, d), lambda i: (0, 0))],
        out_specs=pl.BlockSpec((tm, d), lambda i: (i, 0)),
        out_shape=jax.ShapeDtypeStruct((m, d), BF16),
        compiler_params=_params(("parallel",)),
        name="rmsnorm1",
    )(x, w.reshape(1, d))


def _load_weight(w_ref, wb_ref, row_axis=1):
    @pl.when(pl.program_id(row_axis) == 0)
    def _():
        wb_ref[...] = w_ref[...].astype(BF16)


def _qkv_kernel(h_ref, w_ref, qw_ref, kw_ref, o_ref, wb_ref, *, qk_tiles, row_splits):
    j = pl.program_id(0)
    _load_weight(w_ref, wb_ref)
    bm, bn = o_ref.shape
    sub = bm // row_splits

    @pl.when(j < qk_tiles)
    def _():
        nw = jnp.where(j < qk_tiles // 2, qw_ref[...], kw_ref[...])
        for r in range(row_splits):
            rows = slice(r * sub, (r + 1) * sub)
            acc = jnp.dot(h_ref[rows, :], wb_ref[...], preferred_element_type=F32)
            for hh in range(bn // HEAD_DIM):
                a = acc[:, hh * HEAD_DIM:(hh + 1) * HEAD_DIM]
                y = a * lax.rsqrt(jnp.mean(a * a, axis=-1, keepdims=True) + EPS)
                o_ref[rows, hh * HEAD_DIM:(hh + 1) * HEAD_DIM] = (y * nw).astype(o_ref.dtype)

    @pl.when(j >= qk_tiles)
    def _():
        o_ref[...] = jnp.dot(h_ref[...], wb_ref[...], preferred_element_type=F32).astype(o_ref.dtype)


def _qkv_proj(h, w_in, q_norm_w, k_norm_w, n_cols, bm=1024, bn=1024):
    m, k = h.shape
    return pl.pallas_call(
        functools.partial(_qkv_kernel, qk_tiles=(2 * N_HEADS * HEAD_DIM) // bn, row_splits=4),
        grid=(n_cols // bn, m // bm),
        in_specs=[pl.BlockSpec((bm, k), lambda j, i: (i, 0)),
                  pl.BlockSpec((k, bn), lambda j, i: (0, j)),
                  pl.BlockSpec((1, HEAD_DIM), lambda j, i: (0, 0)),
                  pl.BlockSpec((1, HEAD_DIM), lambda j, i: (0, 0))],
        out_specs=pl.BlockSpec((bm, bn), lambda j, i: (i, j)),
        out_shape=jax.ShapeDtypeStruct((m, n_cols), BF16),
        scratch_shapes=[pltpu.VMEM((k, bn), BF16)],
        compiler_params=_params(("parallel", "arbitrary")),
        name="qkv_proj",
    )(h, w_in, q_norm_w.reshape(1, HEAD_DIM), k_norm_w.reshape(1, HEAD_DIM))


def _matmul_kernel(a_ref, w_ref, o_ref, wb_ref):
    _load_weight(w_ref, wb_ref)
    o_ref[...] = jnp.dot(a_ref[...], wb_ref[...], preferred_element_type=F32).astype(o_ref.dtype)


def _rest_proj(h, w_in, col0, n_cols, bm=1024, bn=1024):
    m, k = h.shape
    jb = col0 // bn
    return pl.pallas_call(
        _matmul_kernel,
        grid=(n_cols // bn, m // bm),
        in_specs=[pl.BlockSpec((bm, k), lambda j, i: (i, 0)),
                  pl.BlockSpec((k, bn), lambda j, i: (0, jb + j))],
        out_specs=pl.BlockSpec((bm, bn), lambda j, i: (i, j)),
        out_shape=jax.ShapeDtypeStruct((m, n_cols), F32),
        scratch_shapes=[pltpu.VMEM((k, bn), BF16)],
        compiler_params=_params(("parallel", "arbitrary")),
        name="rest_proj",
    )(h, w_in)


def _attn_prep_kernel(slope_ref, q_ref, k_ref, v_ref, qt_ref, vt_ref, rb_ref, kmean_ref,
                      *, n_blk, group, cols):
    blk = MOBA_BLOCK
    for jb in range(n_blk):
        rows = slice(jb * blk, (jb + 1) * blk)
        qt_ref[:, rows] = q_ref[rows, :].astype(F32).T.astype(BF16)
        c, g = divmod(jb, group)
        vt_ref[c, 0:HEAD_DIM, g * blk:(g + 1) * blk] = v_ref[rows, :].astype(F32).T.astype(BF16)
        vt_ref[c, HEAD_DIM:, g * blk:(g + 1) * blk] = jnp.ones((ONES_ROWS, blk), BF16)
        kmean_ref[jb:jb + 1, :] = jnp.sum(k_ref[rows, :].astype(F32), axis=0, keepdims=True) * (1.0 / blk)
    kmean = kmean_ref[...].astype(BF16)
    slope = slope_ref[:, 0:1]
    row = lax.broadcasted_iota(jnp.int32, (n_blk, cols), 0)
    col = lax.broadcasted_iota(jnp.int32, (n_blk, cols), 1)
    for qc in range(qt_ref.shape[1] // cols):
        csl = slice(qc * cols, (qc + 1) * cols)
        gate = jnp.dot(kmean, qt_ref[:, csl], preferred_element_type=F32)
        qblk = lax.shift_right_logical(col + qc * cols, blk.bit_length() - 1)
        g = jnp.where(row < qblk, gate, NEG_INF)
        bias = jnp.full((n_blk, cols), NEG_INF, F32)
        for r in range(MOBA_TOPK):
            mx = jnp.max(g, axis=0, keepdims=True)
            idx = jnp.min(jnp.where(g == mx, row, n_blk), axis=0, keepdims=True)
            pick = row == idx
            bias = jnp.where(pick, jnp.where(qblk > r, 0.0, NEG_INF), bias)
            g = jnp.where(pick, -jnp.inf, g)
        past = bias - slope * ((qblk - row) * blk).astype(F32)
        rb = jnp.where(row == qblk, 0.0, jnp.where(row < qblk, past, NEG_INF))
        rb_ref[:, csl] = rb * LOG2E


def _attn_kernel(slope_ref, qt_ref, k_ref, vt_ref, rb_ref, o_ref, *scratch, group, n_blk, heads):
    ua_refs, ub_refs, rbs_refs, m_refs, dmat_refs, acc_refs, p_refs = (
        scratch[n * heads:(n + 1) * heads] for n in range(7))
    blk = MOBA_BLOCK
    cb = group * blk
    lg = group.bit_length() - 1
    i = pl.program_id(1)
    slot = lax.rem(i, 2)
    prev = 1 - slot

    @pl.when(i == 0)
    def _per_head_setup():
        kk = lax.broadcasted_iota(jnp.int32, (blk, blk), 0)
        qq = lax.broadcasted_iota(jnp.int32, (blk, blk), 1)
        for hh in range(heads):
            d = slope_ref[hh][:, 0:1] * (qq - kk).astype(F32) * LOG2E
            dmat_refs[hh][0] = d
            dmat_refs[hh][1] = jnp.where(kk <= qq, d, -NEG_INF)
            m_refs[hh][...] = jnp.zeros_like(m_refs[hh])

    k2 = (HEAD_DIM ** -0.5) * LOG2E
    n1 = jnp.where(i < n_blk, lax.shift_right_logical(i + group, lg), 0)
    n2 = lax.shift_right_logical(i + group - 1, lg)
    common = jnp.minimum(n1, n2)
    m_prev = [m_refs[hh][prev] for hh in range(heads)]
    for hh in range(heads):
        acc_refs[hh][...] = jnp.zeros_like(acc_refs[hh])
        p_refs[hh][...] = jnp.zeros_like(p_refs[hh])

    def run(u_cur, u_prv):
        def pass1(hh, c, mx):
            r0 = pl.multiple_of(c * cb, cb)
            k_chunk = k_ref[pl.ds(r0, cb), hh * HEAD_DIM:(hh + 1) * HEAD_DIM]
            s = jnp.dot(k_chunk, qt_ref[hh], preferred_element_type=F32)
            for g in range(group):
                j = c * group + g
                own = (j == i).astype(jnp.int32)
                u = s[g * blk:(g + 1) * blk, :] * k2 - dmat_refs[hh][own]
                u_cur[hh][c, g * blk:(g + 1) * blk, :] = u
                mx = jnp.maximum(mx, jnp.max(u, axis=0, keepdims=True) + rb_ref[hh, pl.ds(j, 1), :])
            return mx

        def probs(hh, c):
            for g in range(group):
                j = c * group + g
                p = jnp.exp2(u_prv[hh][c, g * blk:(g + 1) * blk, :]
                             + (rbs_refs[hh][prev, pl.ds(j, 1), :] - m_prev[hh]))
                p_refs[hh][g * blk:(g + 1) * blk, :] = p.astype(BF16)

        def pv(hh, c):
            acc_refs[hh][...] += jnp.dot(vt_ref[hh, jnp.maximum(c - 1, 0)], p_refs[hh][...],
                                         preferred_element_type=F32)

        def only1(c, mxs):
            return tuple(pass1(hh, c, mxs[hh]) for hh in range(heads))

        def only2(c, carry):
            for hh in range(heads):
                pv(hh, c)
            for hh in range(heads):
                probs(hh, c)
            return carry

        def both(c, mxs):
            for hh in range(heads):
                pv(hh, c)
            mxs = only1(c, mxs)
            for hh in range(heads):
                probs(hh, c)
            return mxs

        mxs = lax.fori_loop(0, common, both,
                            tuple(jnp.full((1, blk), -jnp.inf, F32) for _ in range(heads)))
        mxs = lax.fori_loop(common, n1, only1, mxs)
        lax.fori_loop(common, n2, only2, 0)
        for hh in range(heads):
            pv(hh, n2)
            m_refs[hh][slot] = mxs[hh]

    @pl.when(slot == 0)
    def _():
        run(ua_refs, ub_refs)

    @pl.when(slot == 1)
    def _():
        run(ub_refs, ua_refs)

    for hh in range(heads):
        rbs_refs[hh][slot] = rb_ref[hh]

    @pl.when(i > 0)
    def _():
        for hh in range(heads):
            acc = acc_refs[hh][...]
            o = acc[0:HEAD_DIM, :] / acc[HEAD_DIM:HEAD_DIM + 1, :]
            o_ref[:, hh * HEAD_DIM:(hh + 1) * HEAD_DIM] = o.T.astype(o_ref.dtype)


def _moba_attention(qkv, slopes, group=4, cols=2048, heads=2):
    s = qkv.shape[0]
    blk = MOBA_BLOCK
    n_blk = s // blk
    n_chunk = n_blk // group
    vt_rows = HEAD_DIM + ONES_ROWS
    qt, vt, rb = pl.pallas_call(
        functools.partial(_attn_prep_kernel, n_blk=n_blk, group=group, cols=cols),
        grid=(N_HEADS,),
        in_specs=[pl.BlockSpec((None, 1, blk), lambda h: (h, 0, 0)),
                  pl.BlockSpec((s, HEAD_DIM), lambda h: (0, h)),
                  pl.BlockSpec((s, HEAD_DIM), lambda h: (0, N_HEADS + h)),
                  pl.BlockSpec((s, HEAD_DIM), lambda h: (0, 2 * N_HEADS + h))],
        out_specs=[pl.BlockSpec((None, HEAD_DIM, s), lambda h: (h, 0, 0)),
                   pl.BlockSpec((None, n_chunk, vt_rows, group * blk), lambda h: (h, 0, 0, 0)),
                   pl.BlockSpec((None, n_blk, s), lambda h: (h, 0, 0))],
        out_shape=[jax.ShapeDtypeStruct((N_HEADS, HEAD_DIM, s), BF16),
                   jax.ShapeDtypeStruct((N_HEADS, n_chunk, vt_rows, group * blk), BF16),
                   jax.ShapeDtypeStruct((N_HEADS, n_blk, s), F32)],
        scratch_shapes=[pltpu.VMEM((n_blk, HEAD_DIM), F32)],
        compiler_params=_params(("parallel",)),
        name="moba_prep",
    )(slopes, qkv, qkv, qkv)
    last = n_blk - 1
    hw = heads * HEAD_DIM
    k_col0 = (N_HEADS * HEAD_DIM) // hw
    once = pl.Buffered(1)
    per_head = lambda shape: [pltpu.VMEM(shape, F32) for _ in range(heads)]
    return pl.pallas_call(
        functools.partial(_attn_kernel, group=group, n_blk=n_blk, heads=heads),
        grid=(N_HEADS // heads, n_blk + 1),
        in_specs=[pl.BlockSpec((heads, 1, blk), lambda h, i: (h, 0, 0)),
                  pl.BlockSpec((heads, HEAD_DIM, blk), lambda h, i: (h, 0, jnp.minimum(i, last))),
                  pl.BlockSpec((s, hw), lambda h, i: (0, k_col0 + h), pipeline_mode=once),
                  pl.BlockSpec((heads, n_chunk, vt_rows, group * blk), lambda h, i: (h, 0, 0, 0),
                               pipeline_mode=once),
                  pl.BlockSpec((heads, n_blk, blk), lambda h, i: (h, 0, jnp.minimum(i, last)))],
        out_specs=pl.BlockSpec((blk, hw), lambda h, i: (jnp.maximum(i - 1, 0), h)),
        out_shape=jax.ShapeDtypeStruct((s, N_HEADS * HEAD_DIM), BF16),
        scratch_shapes=(per_head((n_chunk, group * blk, blk))
                        + per_head((n_chunk, group * blk, blk))
                        + per_head((2, n_blk, blk))
                        + per_head((2, 1, blk))
                        + per_head((2, blk, blk))
                        + per_head((vt_rows, blk))
                        + [pltpu.VMEM((group * blk, blk), BF16) for _ in range(heads)]),
        compiler_params=_params(("parallel", "arbitrary")),
        name="moba_attention",
    )(slopes, qt, qkv, vt, rb)


def _lru_kernel(xr_ref, yr_ref, cw_ref, cb_ref, wa_ref, ba_ref, wx_ref, bx_ref, lam_ref, o_ref,
                xbuf_ref, a_ref, b_ref, h_ref, wab_ref, wxb_ref, *, ts, tc):
    t = pl.program_id(1)
    pad = 8

    @pl.when(t == 0)
    def _():
        xbuf_ref[0:pad, :] = jnp.zeros((pad, tc), F32)
        h_ref[...] = jnp.zeros_like(h_ref)
        wab_ref[...] = wa_ref[...].astype(BF16)
        wxb_ref[...] = wx_ref[...].astype(BF16)

    xbuf_ref[pad:pad + ts, :] = xr_ref[...]
    cw = cw_ref[...]
    u = cb_ref[...]
    for tap in range(CONV_WIDTH):
        off = pad - (CONV_WIDTH - 1) + tap
        u = u + xbuf_ref[off:off + ts, :] * cw[tap:tap + 1, :]
    xbuf_ref[0:pad, :] = xbuf_ref[ts:ts + pad, :]

    ub = u.astype(BF16)
    ga, gx = [], []
    for n in range(tc // LRU_BLOCK_W):
        un = ub[:, n * LRU_BLOCK_W:(n + 1) * LRU_BLOCK_W]
        ga.append(jnp.dot(un, wab_ref[n], preferred_element_type=F32))
        gx.append(jnp.dot(un, wxb_ref[n], preferred_element_type=F32))
    r = jax.nn.sigmoid(jnp.concatenate(ga, axis=1) + ba_ref[...])
    ig = jax.nn.sigmoid(jnp.concatenate(gx, axis=1) + bx_ref[...])
    log_a = -LRU_C * r * jax.nn.softplus(-lam_ref[...])
    a = jnp.exp(log_a)
    a_ref[...] = a
    one_minus_a2 = -jnp.tanh(log_a) * (a * a + 1.0)
    mult = jnp.where(one_minus_a2 == 0.0, 0.0, one_minus_a2 * lax.rsqrt(one_minus_a2))
    b_ref[...] = mult * ig * u

    row = lax.broadcasted_iota(jnp.int32, (8, tc), 0)

    def group(g, hprev):
        r0 = pl.multiple_of(g * 8, 8)
        av = a_ref[pl.ds(r0, 8), :]
        bv = b_ref[pl.ds(r0, 8), :]
        for d in (1, 2, 4):
            keep = row >= d
            a_sh = pltpu.roll(av, d, 0)
            b_sh = pltpu.roll(bv, d, 0)
            bv = jnp.where(keep, av * b_sh + bv, bv)
            av = jnp.where(keep, av * a_sh, av)
        hv = av * hprev + bv
        b_ref[pl.ds(r0, 8), :] = hv
        return jnp.broadcast_to(hv[7:8, :], (8, tc))

    h_ref[...] = lax.fori_loop(0, ts // 8, group, h_ref[...])
    o_ref[...] = (b_ref[...] * jax.nn.gelu(yr_ref[...])).astype(o_ref.dtype)


def _rglru(rest, conv_w, conv_b, w_rg_a, b_rg_a, w_rg_x, b_rg_x, lru_lambda, width, ts=512, tc=512):
    s = rest.shape[0]
    nct = width // tc
    nb = tc // LRU_BLOCK_W
    vec = lambda v: v.reshape(1, width)
    vspec = pl.BlockSpec((1, tc), lambda c, t: (0, c))
    wspec = pl.BlockSpec((nb, LRU_BLOCK_W, LRU_BLOCK_W), lambda c, t: (c, 0, 0))
    return pl.pallas_call(
        functools.partial(_lru_kernel, ts=ts, tc=tc),
        grid=(nct, s // ts),
        in_specs=[pl.BlockSpec((ts, tc), lambda c, t: (t, c)),
                  pl.BlockSpec((ts, tc), lambda c, t: (t, nct + c)),
                  pl.BlockSpec((CONV_WIDTH, tc), lambda c, t: (0, c)),
                  vspec, wspec, vspec, wspec, vspec, vspec],
        out_specs=pl.BlockSpec((ts, tc), lambda c, t: (t, c)),
        out_shape=jax.ShapeDtypeStruct((s, width), BF16),
        scratch_shapes=[pltpu.VMEM((ts + 8, tc), F32),
                        pltpu.VMEM((ts, tc), F32),
                        pltpu.VMEM((ts, tc), F32),
                        pltpu.VMEM((8, tc), F32),
                        pltpu.VMEM((nb, LRU_BLOCK_W, LRU_BLOCK_W), BF16),
                        pltpu.VMEM((nb, LRU_BLOCK_W, LRU_BLOCK_W), BF16)],
        compiler_params=_params(("parallel", "arbitrary")),
        name="rglru",
    )(rest, rest, conv_w, vec(conv_b), w_rg_a, vec(b_rg_a), w_rg_x, vec(b_rg_x), vec(lru_lambda))


def _merge_kernel(att_ref, lru_ref, wa_ref, wl_ref, ga_ref, gl_ref, o_ref, wab_ref, wlb_ref):
    _load_weight(wa_ref, wab_ref)
    _load_weight(wl_ref, wlb_ref)
    bm = o_ref.shape[0]
    sub = bm // 2
    for r in range(2):
        rows = slice(r * sub, (r + 1) * sub)
        pa = jnp.dot(att_ref[rows, :], wab_ref[...], preferred_element_type=F32)
        plru = jnp.dot(lru_ref[rows, :], wlb_ref[...], preferred_element_type=F32)
        o_ref[rows, :] = (jax.nn.sigmoid(ga_ref[rows, :]) * pa
                          + jax.nn.sigmoid(gl_ref[rows, :]) * plru).astype(o_ref.dtype)


def _merge(att, lru, w_att, w_lru, rest, gate_col0, bm=512, bn=512):
    m, k = att.shape
    n = w_att.shape[1]
    ga0 = gate_col0 // bn
    gl0 = (gate_col0 + n) // bn
    return pl.pallas_call(
        _merge_kernel,
        grid=(n // bn, m // bm),
        in_specs=[pl.BlockSpec((bm, k), lambda j, i: (i, 0)),
                  pl.BlockSpec((bm, k), lambda j, i: (i, 0)),
                  pl.BlockSpec((k, bn), lambda j, i: (0, j)),
                  pl.BlockSpec((k, bn), lambda j, i: (0, j)),
                  pl.BlockSpec((bm, bn), lambda j, i: (i, ga0 + j)),
                  pl.BlockSpec((bm, bn), lambda j, i: (i, gl0 + j))],
        out_specs=pl.BlockSpec((bm, bn), lambda j, i: (i, j)),
        out_shape=jax.ShapeDtypeStruct((m, n), BF16),
        scratch_shapes=[pltpu.VMEM((k, bn), BF16), pltpu.VMEM((k, bn), BF16)],
        compiler_params=_params(("parallel", "arbitrary")),
        name="merge",
    )(att, lru, w_att, w_lru, rest, rest)


def _outproj_kernel(a_ref, w_ref, x_ref, nw_ref, x1_ref, h2_ref, wb_ref):
    _load_weight(w_ref, wb_ref, row_axis=0)
    x1 = x_ref[...] + jnp.dot(a_ref[...], wb_ref[...], preferred_element_type=F32)
    x1_ref[...] = x1
    y = x1 * lax.rsqrt(jnp.mean(x1 * x1, axis=-1, keepdims=True) + EPS)
    h2_ref[...] = (y * nw_ref[...]).astype(h2_ref.dtype)


def _outproj(merged, w_out, x, norm2_w, bm=512):
    m, k = merged.shape
    d = w_out.shape[1]
    return pl.pallas_call(
        _outproj_kernel,
        grid=(m // bm,),
        in_specs=[pl.BlockSpec((bm, k), lambda i: (i, 0)),
                  pl.BlockSpec((k, d), lambda i: (0, 0), pipeline_mode=pl.Buffered(1)),
                  pl.BlockSpec((bm, d), lambda i: (i, 0)),
                  pl.BlockSpec((1, d), lambda i: (0, 0))],
        out_specs=[pl.BlockSpec((bm, d), lambda i: (i, 0)),
                   pl.BlockSpec((bm, d), lambda i: (i, 0))],
        out_shape=[jax.ShapeDtypeStruct((m, d), F32), jax.ShapeDtypeStruct((m, d), BF16)],
        scratch_shapes=[pltpu.VMEM((k, d), BF16)],
        compiler_params=_params(("arbitrary",)),
        name="outproj",
    )(merged, w_out, x, norm2_w.reshape(1, d))


def _ffn_up_kernel(h_ref, wg_ref, wu_ref, o_ref, wgb_ref, wub_ref):
    _load_weight(wg_ref, wgb_ref)
    _load_weight(wu_ref, wub_ref)
    h = h_ref[...]
    g = jnp.dot(h, wgb_ref[...], preferred_element_type=F32)
    u = jnp.dot(h, wub_ref[...], preferred_element_type=F32)
    o_ref[...] = (jax.nn.silu(g) * u).astype(o_ref.dtype)


def _ffn_up(h2, w_gate, w_up, bm=1024, bn=512):
    m, k = h2.shape
    n = w_gate.shape[1]
    return pl.pallas_call(
        _ffn_up_kernel,
        grid=(n // bn, m // bm),
        in_specs=[pl.BlockSpec((bm, k), lambda j, i: (i, 0)),
                  pl.BlockSpec((k, bn), lambda j, i: (0, j)),
                  pl.BlockSpec((k, bn), lambda j, i: (0, j))],
        out_specs=pl.BlockSpec((bm, bn), lambda j, i: (i, j)),
        out_shape=jax.ShapeDtypeStruct((m, n), BF16),
        scratch_shapes=[pltpu.VMEM((k, bn), BF16), pltpu.VMEM((k, bn), BF16)],
        compiler_params=_params(("parallel", "arbitrary")),
        name="ffn_up",
    )(h2, w_gate, w_up)


def _ffn_down_kernel(a_ref, w_ref, x_ref, o_ref, wb_ref):
    _load_weight(w_ref, wb_ref)
    o_ref[...] = x_ref[...] + jnp.dot(a_ref[...], wb_ref[...], preferred_element_type=F32)


def _ffn_down(act, w_down, x1, bm=512, bn=512):
    m, k = act.shape
    n = w_down.shape[1]
    return pl.pallas_call(
        _ffn_down_kernel,
        grid=(n // bn, m // bm),
        in_specs=[pl.BlockSpec((bm, k), lambda j, i: (i, 0)),
                  pl.BlockSpec((k, bn), lambda j, i: (0, j)),
                  pl.BlockSpec((bm, bn), lambda j, i: (i, j))],
        out_specs=pl.BlockSpec((bm, bn), lambda j, i: (i, j)),
        out_shape=jax.ShapeDtypeStruct((m, n), F32),
        scratch_shapes=[pltpu.VMEM((k, bn), BF16)],
        compiler_params=_params(("parallel", "arbitrary")),
        name="ffn_down",
    )(act, w_down, x1)


def _layer(x, norm1_w, w_in, q_norm_w, k_norm_w, conv_w, conv_b, w_rg_a, b_rg_a, w_rg_x, b_rg_x,
           lru_lambda, w_proj_attn, w_proj_lru, w_out, norm2_w, w_ffn_gate, w_ffn_up, w_ffn_down):
    d = x.shape[1]
    att_w = N_HEADS * HEAD_DIM
    lru_w = w_proj_lru.shape[0]

    h = _rmsnorm(x, norm1_w)
    qkv = _qkv_proj(h, w_in, q_norm_w, k_norm_w, 3 * att_w)
    rest = _rest_proj(h, w_in, 3 * att_w, 2 * lru_w + 2 * d)

    head = jnp.arange(1, N_HEADS + 1, dtype=F32)
    slopes = jnp.broadcast_to(jnp.exp2(-8.0 * head / N_HEADS)[:, None, None], (N_HEADS, 1, MOBA_BLOCK))
    att = _moba_attention(qkv, slopes)

    lru = _rglru(rest, conv_w, conv_b, w_rg_a, b_rg_a, w_rg_x, b_rg_x, lru_lambda, lru_w)

    merged = _merge(att, lru, w_proj_attn, w_proj_lru, rest, 2 * lru_w)
    x1, h2 = _outproj(merged, w_out, x, norm2_w)
    act = _ffn_up(h2, w_ffn_gate, w_ffn_up)
    return _ffn_down(act, w_ffn_down, x1)


def kernel(x, norm1_w, w_in, q_norm_w, k_norm_w, conv_w, conv_b, w_rg_a, b_rg_a, w_rg_x, b_rg_x,
           lru_lambda, w_proj_attn, w_proj_lru, w_out, norm2_w, w_ffn_gate, w_ffn_up, w_ffn_down):
    b, s, d = x.shape
    assert b == 1, "kernel handles the batch-1 prefill shape"
    y = x.reshape(s, d)
    for layer in range(norm1_w.shape[0]):
        y = _layer(y, norm1_w[layer], w_in[layer], q_norm_w[layer], k_norm_w[layer], conv_w[layer],
                   conv_b[layer], w_rg_a[layer], b_rg_a[layer], w_rg_x[layer], b_rg_x[layer],
                   lru_lambda[layer], w_proj_attn[layer], w_proj_lru[layer], w_out[layer],
                   norm2_w[layer], w_ffn_gate[layer], w_ffn_up[layer], w_ffn_down[layer])
    return y.reshape(b, s, d)
```

```python
import functools

import jax
import jax.numpy as jnp
from jax import lax
from jax.experimental import pallas as pl
from jax.experimental.pallas import tpu as pltpu

F32 = jnp.float32
BF16 = jnp.bfloat16

N_HEADS = 16
HEAD_DIM = 128
MOBA_BLOCK = 256
MOBA_TOPK = 3
LRU_BLOCK_W = 128
CONV_WIDTH = 4
LRU_C = 8.0
EPS = 1e-6
NEG_INF = -1e30
LOG2E = 1.4426950408889634
ONES_ROWS = 16

V7X_VMEM_BYTES = 64 * 1024 * 1024
VMEM_LIMIT = 56 * 1024 * 1024


def _params(semantics):
    return pltpu.CompilerParams(dimension_semantics=semantics, vmem_limit_bytes=VMEM_LIMIT)


def _rmsnorm_kernel(x_ref, w_ref, o_ref):
    x = x_ref[...]
    y = x * lax.rsqrt(jnp.mean(x * x, axis=-1, keepdims=True) + EPS)
    o_ref[...] = (y * w_ref[...]).astype(o_ref.dtype)


def _rmsnorm(x, w, tm=512):
    m, d = x.shape
    return pl.pallas_call(
        _rmsnorm_kernel,
        grid=(m // tm,),
        in_specs=[pl.BlockSpec((tm, d), lambda i: (i, 0)),
                  pl.BlockSpec((1, d), lambda i: (0, 0))],
        out_specs=pl.BlockSpec((tm, d), lambda i: (i, 0)),
        out_shape=jax.ShapeDtypeStruct((m, d), BF16),
        compiler_params=_params(("parallel",)),
        name="rmsnorm1",
    )(x, w.reshape(1, d))


def _load_weight(w_ref, wb_ref, row_axis=1):
    @pl.when(pl.program_id(row_axis) == 0)
    def _():
        wb_ref[...] = w_ref[...].astype(BF16)


def _qkv_kernel(h_ref, w_ref, qw_ref, kw_ref, o_ref, wb_ref, *, qk_tiles, row_splits):
    j = pl.program_id(0)
    _load_weight(w_ref, wb_ref)
    bm, bn = o_ref.shape
    sub = bm // row_splits

    @pl.when(j < qk_tiles)
    def _():
        nw = jnp.where(j < qk_tiles // 2, qw_ref[...], kw_ref[...])
        for r in range(row_splits):
            rows = slice(r * sub, (r + 1) * sub)
            acc = jnp.dot(h_ref[rows, :], wb_ref[...], preferred_element_type=F32)
            for hh in range(bn // HEAD_DIM):
                a = acc[:, hh * HEAD_DIM:(hh + 1) * HEAD_DIM]
                y = a * lax.rsqrt(jnp.mean(a * a, axis=-1, keepdims=True) + EPS)
                o_ref[rows, hh * HEAD_DIM:(hh + 1) * HEAD_DIM] = (y * nw).astype(o_ref.dtype)

    @pl.when(j >= qk_tiles)
    def _():
        o_ref[...] = jnp.dot(h_ref[...], wb_ref[...], preferred_element_type=F32).astype(o_ref.dtype)


def _qkv_proj(h, w_in, q_norm_w, k_norm_w, n_cols, bm=1024, bn=1024):
    m, k = h.shape
    return pl.pallas_call(
        functools.partial(_qkv_kernel, qk_tiles=(2 * N_HEADS * HEAD_DIM) // bn, row_splits=4),
        grid=(n_cols // bn, m // bm),
        in_specs=[pl.BlockSpec((bm, k), lambda j, i: (i, 0)),
                  pl.BlockSpec((k, bn), lambda j, i: (0, j)),
                  pl.BlockSpec((1, HEAD_DIM), lambda j, i: (0, 0)),
                  pl.BlockSpec((1, HEAD_DIM), lambda j, i: (0, 0))],
        out_specs=pl.BlockSpec((bm, bn), lambda j, i: (i, j)),
        out_shape=jax.ShapeDtypeStruct((m, n_cols), BF16),
        scratch_shapes=[pltpu.VMEM((k, bn), BF16)],
        compiler_params=_params(("parallel", "arbitrary")),
        name="qkv_proj",
    )(h, w_in, q_norm_w.reshape(1, HEAD_DIM), k_norm_w.reshape(1, HEAD_DIM))


def _matmul_kernel(a_ref, w_ref, o_ref, wb_ref):
    _load_weight(w_ref, wb_ref)
    o_ref[...] = jnp.dot(a_ref[...], wb_ref[...], preferred_element_type=F32).astype(o_ref.dtype)


def _rest_proj(h, w_in, col0, n_cols, bm=1024, bn=1024):
    m, k = h.shape
    jb = col0 // bn
    return pl.pallas_call(
        _matmul_kernel,
        grid=(n_cols // bn, m // bm),
        in_specs=[pl.BlockSpec((bm, k), lambda j, i: (i, 0)),
                  pl.BlockSpec((k, bn), lambda j, i: (0, jb + j))],
        out_specs=pl.BlockSpec((bm, bn), lambda j, i: (i, j)),
        out_shape=jax.ShapeDtypeStruct((m, n_cols), F32),
        scratch_shapes=[pltpu.VMEM((k, bn), BF16)],
        compiler_params=_params(("parallel", "arbitrary")),
        name="rest_proj",
    )(h, w_in)


def _attn_prep_kernel(slope_ref, q_ref, k_ref, v_ref, qt_ref, vt_ref, rb_ref, kmean_ref,
                      *, n_blk, group, cols):
    blk = MOBA_BLOCK
    for jb in range(n_blk):
        rows = slice(jb * blk, (jb + 1) * blk)
        qt_ref[:, rows] = q_ref[rows, :].astype(F32).T.astype(BF16)
        c, g = divmod(jb, group)
        vt_ref[c, 0:HEAD_DIM, g * blk:(g + 1) * blk] = v_ref[rows, :].astype(F32).T.astype(BF16)
        vt_ref[c, HEAD_DIM:, g * blk:(g + 1) * blk] = jnp.ones((ONES_ROWS, blk), BF16)
        kmean_ref[jb:jb + 1, :] = jnp.sum(k_ref[rows, :].astype(F32), axis=0, keepdims=True) * (1.0 / blk)
    kmean = kmean_ref[...].astype(BF16)
    slope = slope_ref[:, 0:1]
    row = lax.broadcasted_iota(jnp.int32, (n_blk, cols), 0)
    col = lax.broadcasted_iota(jnp.int32, (n_blk, cols), 1)
    for qc in range(qt_ref.shape[1] // cols):
        csl = slice(qc * cols, (qc + 1) * cols)
        gate = jnp.dot(kmean, qt_ref[:, csl], preferred_element_type=F32)
        qblk = lax.shift_right_logical(col + qc * cols, blk.bit_length() - 1)
        g = jnp.where(row < qblk, gate, NEG_INF)
        bias = jnp.full((n_blk, cols), NEG_INF, F32)
        for r in range(MOBA_TOPK):
            mx = jnp.max(g, axis=0, keepdims=True)
            idx = jnp.min(jnp.where(g == mx, row, n_blk), axis=0, keepdims=True)
            pick = row == idx
            bias = jnp.where(pick, jnp.where(qblk > r, 0.0, NEG_INF), bias)
            g = jnp.where(pick, -jnp.inf, g)
        past = bias - slope * ((qblk - row) * blk).astype(F32)
        rb = jnp.where(row == qblk, 0.0, jnp.where(row < qblk, past, NEG_INF))
        rb_ref[:, csl] = rb * LOG2E


def _attn_kernel(slope_ref, qt_ref, k_ref, vt_ref, rb_ref, o_ref, *scratch, group, n_blk, heads):
    ua_refs, ub_refs, rbs_refs, m_refs, dmat_refs, acc_refs, p_refs = (
        scratch[n * heads:(n + 1) * heads] for n in range(7))
    blk = MOBA_BLOCK
    cb = group * blk
    lg = group.bit_length() - 1
    i = pl.program_id(1)
    slot = lax.rem(i, 2)
    prev = 1 - slot

    @pl.when(i == 0)
    def _per_head_setup():
        kk = lax.broadcasted_iota(jnp.int32, (blk, blk), 0)
        qq = lax.broadcasted_iota(jnp.int32, (blk, blk), 1)
        for hh in range(heads):
            d = slope_ref[hh][:, 0:1] * (qq - kk).astype(F32) * LOG2E
            dmat_refs[hh][0] = d
            dmat_refs[hh][1] = jnp.where(kk <= qq, d, -NEG_INF)
            m_refs[hh][...] = jnp.zeros_like(m_refs[hh])

    k2 = (HEAD_DIM ** -0.5) * LOG2E
    n1 = jnp.where(i < n_blk, lax.shift_right_logical(i + group, lg), 0)
    n2 = lax.shift_right_logical(i + group - 1, lg)
    common = jnp.minimum(n1, n2)
    m_prev = [m_refs[hh][prev] for hh in range(heads)]
    for hh in range(heads):
        acc_refs[hh][...] = jnp.zeros_like(acc_refs[hh])
        p_refs[hh][...] = jnp.zeros_like(p_refs[hh])

    def run(u_cur, u_prv):
        def pass1(hh, c, mx):
            r0 = pl.multiple_of(c * cb, cb)
            k_chunk = k_ref[pl.ds(r0, cb), hh * HEAD_DIM:(hh + 1) * HEAD_DIM]
            s = jnp.dot(k_chunk, qt_ref[hh], preferred_element_type=F32)
            for g in range(group):
                j = c * group + g
                own = (j == i).astype(jnp.int32)
                u = s[g * blk:(g + 1) * blk, :] * k2 - dmat_refs[hh][own]
                u_cur[hh][c, g * blk:(g + 1) * blk, :] = u
                mx = jnp.maximum(mx, jnp.max(u, axis=0, keepdims=True) + rb_ref[hh, pl.ds(j, 1), :])
            return mx

        def probs(hh, c):
            for g in range(group):
                j = c * group + g
                p = jnp.exp2(u_prv[hh][c, g * blk:(g + 1) * blk, :]
                             + (rbs_refs[hh][prev, pl.ds(j, 1), :] - m_prev[hh]))
                p_refs[hh][g * blk:(g + 1) * blk, :] = p.astype(BF16)

        def pv(hh, c):
            acc_refs[hh][...] += jnp.dot(vt_ref[hh, jnp.maximum(c - 1, 0)], p_refs[hh][...],
                                         preferred_element_type=F32)

        def only1(c, mxs):
            return tuple(pass1(hh, c, mxs[hh]) for hh in range(heads))

        def only2(c, carry):
            for hh in range(heads):
                pv(hh, c)
            for hh in range(heads):
                probs(hh, c)
            return carry

        def both(c, mxs):
            for hh in range(heads):
                pv(hh, c)
            mxs = only1(c, mxs)
            for hh in range(heads):
                probs(hh, c)
            return mxs

        mxs = lax.fori_loop(0, common, both,
                            tuple(jnp.full((1, blk), -jnp.inf, F32) for _ in range(heads)))
        mxs = lax.fori_loop(common, n1, only1, mxs)
        lax.fori_loop(common, n2, only2, 0)
        for hh in range(heads):
            pv(hh, n2)
            m_refs[hh][slot] = mxs[hh]

    @pl.when(slot == 0)
    def _():
        run(ua_refs, ub_refs)

    @pl.when(slot == 1)
    def _():
        run(ub_refs, ua_refs)

    for hh in range(heads):
        rbs_refs[hh][slot] = rb_ref[hh]

    @pl.when(i > 0)
    def _():
        for hh in range(heads):
            acc = acc_refs[hh][...]
            o = acc[0:HEAD_DIM, :] / acc[HEAD_DIM:HEAD_DIM + 1, :]
            o_ref[:, hh * HEAD_DIM:(hh + 1) * HEAD_DIM] = o.T.astype(o_ref.dtype)


def _moba_attention(qkv, slopes, group=4, cols=2048, heads=2):
    s = qkv.shape[0]
    blk = MOBA_BLOCK
    n_blk = s // blk
    n_chunk = n_blk // group
    vt_rows = HEAD_DIM + ONES_ROWS
    qt, vt, rb = pl.pallas_call(
        functools.partial(_attn_prep_kernel, n_blk=n_blk, group=group, cols=cols),
        grid=(N_HEADS,),
        in_specs=[pl.BlockSpec((None, 1, blk), lambda h: (h, 0, 0)),
                  pl.BlockSpec((s, HEAD_DIM), lambda h: (0, h)),
                  pl.BlockSpec((s, HEAD_DIM), lambda h: (0, N_HEADS + h)),
                  pl.BlockSpec((s, HEAD_DIM), lambda h: (0, 2 * N_HEADS + h))],
        out_specs=[pl.BlockSpec((None, HEAD_DIM, s), lambda h: (h, 0, 0)),
                   pl.BlockSpec((None, n_chunk, vt_rows, group * blk), lambda h: (h, 0, 0, 0)),
                   pl.BlockSpec((None, n_blk, s), lambda h: (h, 0, 0))],
        out_shape=[jax.ShapeDtypeStruct((N_HEADS, HEAD_DIM, s), BF16),
                   jax.ShapeDtypeStruct((N_HEADS, n_chunk, vt_rows, group * blk), BF16),
                   jax.ShapeDtypeStruct((N_HEADS, n_blk, s), F32)],
        scratch_shapes=[pltpu.VMEM((n_blk, HEAD_DIM), F32)],
        compiler_params=_params(("parallel",)),
        name="moba_prep",
    )(slopes, qkv, qkv, qkv)
    last = n_blk - 1
    hw = heads * HEAD_DIM
    k_col0 = (N_HEADS * HEAD_DIM) // hw
    once = pl.Buffered(1)
    per_head = lambda shape: [pltpu.VMEM(shape, F32) for _ in range(heads)]
    return pl.pallas_call(
        functools.partial(_attn_kernel, group=group, n_blk=n_blk, heads=heads),
        grid=(N_HEADS // heads, n_blk + 1),
        in_specs=[pl.BlockSpec((heads, 1, blk), lambda h, i: (h, 0, 0)),
                  pl.BlockSpec((heads, HEAD_DIM, blk), lambda h, i: (h, 0, jnp.minimum(i, last))),
                  pl.BlockSpec((s, hw), lambda h, i: (0, k_col0 + h), pipeline_mode=once),
                  pl.BlockSpec((heads, n_chunk, vt_rows, group * blk), lambda h, i: (h, 0, 0, 0),
                               pipeline_mode=once),
                  pl.BlockSpec((heads, n_blk, blk), lambda h, i: (h, 0, jnp.minimum(i, last)))],
        out_specs=pl.BlockSpec((blk, hw), lambda h, i: (jnp.maximum(i - 1, 0), h)),
        out_shape=jax.ShapeDtypeStruct((s, N_HEADS * HEAD_DIM), BF16),
        scratch_shapes=(per_head((n_chunk, group * blk, blk))
                        + per_head((n_chunk, group * blk, blk))
                        + per_head((2, n_blk, blk))
                        + per_head((2, 1, blk))
                        + per_head((2, blk, blk))
                        + per_head((vt_rows, blk))
                        + [pltpu.VMEM((group * blk, blk), BF16) for _ in range(heads)]),
        compiler_params=_params(("parallel", "arbitrary")),
        name="moba_attention",
    )(slopes, qt, qkv, vt, rb)


def _lru_kernel(xr_ref, yr_ref, cw_ref, cb_ref, wa_ref, ba_ref, wx_ref, bx_ref, lam_ref, o_ref,
                xbuf_ref, a_ref, b_ref, h_ref, wab_ref, wxb_ref, *, ts, tc):
    t = pl.program_id(1)
    pad = 8

    @pl.when(t == 0)
    def _():
        xbuf_ref[0:pad, :] = jnp.zeros((pad, tc), F32)
        h_ref[...] = jnp.zeros_like(h_ref)
        wab_ref[...] = wa_ref[...].astype(BF16)
        wxb_ref[...] = wx_ref[...].astype(BF16)

    xbuf_ref[pad:pad + ts, :] = xr_ref[...]
    cw = cw_ref[...]
    u = cb_ref[...]
    for tap in range(CONV_WIDTH):
        off = pad - (CONV_WIDTH - 1) + tap
        u = u + xbuf_ref[off:off + ts, :] * cw[tap:tap + 1, :]
    xbuf_ref[0:pad, :] = xbuf_ref[ts:ts + pad, :]

    ub = u.astype(BF16)
    ga, gx = [], []
    for n in range(tc // LRU_BLOCK_W):
        un = ub[:, n * LRU_BLOCK_W:(n + 1) * LRU_BLOCK_W]
        ga.append(jnp.dot(un, wab_ref[n], preferred_element_type=F32))
        gx.append(jnp.dot(un, wxb_ref[n], preferred_element_type=F32))
    r = jax.nn.sigmoid(jnp.concatenate(ga, axis=1) + ba_ref[...])
    ig = jax.nn.sigmoid(jnp.concatenate(gx, axis=1) + bx_ref[...])
    log_a = -LRU_C * r * jax.nn.softplus(-lam_ref[...])
    a = jnp.exp(log_a)
    a_ref[...] = a
    one_minus_a2 = -jnp.tanh(log_a) * (a * a + 1.0)
    mult = jnp.where(one_minus_a2 == 0.0, 0.0, one_minus_a2 * lax.rsqrt(one_minus_a2))
    b_ref[...] = mult * ig * u

    row = lax.broadcasted_iota(jnp.int32, (8, tc), 0)

    def group(g, hprev):
        r0 = pl.multiple_of(g * 8, 8)
        av = a_ref[pl.ds(r0, 8), :]
        bv = b_ref[pl.ds(r0, 8), :]
        for d in (1, 2, 4):
            keep = row >= d
            a_sh = pltpu.roll(av, d, 0)
            b_sh = pltpu.roll(bv, d, 0)
            bv = jnp.where(keep, av * b_sh + bv, bv)
            av = jnp.where(keep, av * a_sh, av)
        hv = av * hprev + bv
        b_ref[pl.ds(r0, 8), :] = hv
        return jnp.broadcast_to(hv[7:8, :], (8, tc))

    h_ref[...] = lax.fori_loop(0, ts // 8, group, h_ref[...])
    o_ref[...] = (b_ref[...] * jax.nn.gelu(yr_ref[...])).astype(o_ref.dtype)


def _rglru(rest, conv_w, conv_b, w_rg_a, b_rg_a, w_rg_x, b_rg_x, lru_lambda, width, ts=512, tc=512):
    s = rest.shape[0]
    nct = width // tc
    nb = tc // LRU_BLOCK_W
    vec = lambda v: v.reshape(1, width)
    vspec = pl.BlockSpec((1, tc), lambda c, t: (0, c))
    wspec = pl.BlockSpec((nb, LRU_BLOCK_W, LRU_BLOCK_W), lambda c, t: (c, 0, 0))
    return pl.pallas_call(
        functools.partial(_lru_kernel, ts=ts, tc=tc),
        grid=(nct, s // ts),
        in_specs=[pl.BlockSpec((ts, tc), lambda c, t: (t, c)),
                  pl.BlockSpec((ts, tc), lambda c, t: (t, nct + c)),
                  pl.BlockSpec((CONV_WIDTH, tc), lambda c, t: (0, c)),
                  vspec, wspec, vspec, wspec, vspec, vspec],
        out_specs=pl.BlockSpec((ts, tc), lambda c, t: (t, c)),
        out_shape=jax.ShapeDtypeStruct((s, width), BF16),
        scratch_shapes=[pltpu.VMEM((ts + 8, tc), F32),
                        pltpu.VMEM((ts, tc), F32),
                        pltpu.VMEM((ts, tc), F32),
                        pltpu.VMEM((8, tc), F32),
                        pltpu.VMEM((nb, LRU_BLOCK_W, LRU_BLOCK_W), BF16),
                        pltpu.VMEM((nb, LRU_BLOCK_W, LRU_BLOCK_W), BF16)],
        compiler_params=_params(("parallel", "arbitrary")),
        name="rglru",
    )(rest, rest, conv_w, vec(conv_b), w_rg_a, vec(b_rg_a), w_rg_x, vec(b_rg_x), vec(lru_lambda))


def _merge_kernel(att_ref, lru_ref, wa_ref, wl_ref, ga_ref, gl_ref, o_ref, wab_ref, wlb_ref):
    _load_weight(wa_ref, wab_ref)
    _load_weight(wl_ref, wlb_ref)
    bm = o_ref.shape[0]
    sub = bm // 2
    for r in range(2):
        rows = slice(r * sub, (r + 1) * sub)
        pa = jnp.dot(att_ref[rows, :], wab_ref[...], preferred_element_type=F32)
        plru = jnp.dot(lru_ref[rows, :], wlb_ref[...], preferred_element_type=F32)
        o_ref[rows, :] = (jax.nn.sigmoid(ga_ref[rows, :]) * pa
                          + jax.nn.sigmoid(gl_ref[rows, :]) * plru).astype(o_ref.dtype)


def _merge(att, lru, w_att, w_lru, rest, gate_col0, bm=512, bn=1024):
    m, k = att.shape
    n = w_att.shape[1]
    ga0 = gate_col0 // bn
    gl0 = (gate_col0 + n) // bn
    once = pl.Buffered(1)
    return pl.pallas_call(
        _merge_kernel,
        grid=(n // bn, m // bm),
        in_specs=[pl.BlockSpec((bm, k), lambda j, i: (i, 0)),
                  pl.BlockSpec((bm, k), lambda j, i: (i, 0)),
                  pl.BlockSpec((k, bn), lambda j, i: (0, j), pipeline_mode=once),
                  pl.BlockSpec((k, bn), lambda j, i: (0, j), pipeline_mode=once),
                  pl.BlockSpec((bm, bn), lambda j, i: (i, ga0 + j)),
                  pl.BlockSpec((bm, bn), lambda j, i: (i, gl0 + j))],
        out_specs=pl.BlockSpec((bm, bn), lambda j, i: (i, j)),
        out_shape=jax.ShapeDtypeStruct((m, n), BF16),
        scratch_shapes=[pltpu.VMEM((k, bn), BF16), pltpu.VMEM((k, bn), BF16)],
        compiler_params=_params(("parallel", "arbitrary")),
        name="merge",
    )(att, lru, w_att, w_lru, rest, rest)


def _outproj_kernel(a_ref, w_ref, x_ref, nw_ref, x1_ref, h2_ref, wb_ref):
    _load_weight(w_ref, wb_ref, row_axis=0)
    x1 = x_ref[...] + jnp.dot(a_ref[...], wb_ref[...], preferred_element_type=F32)
    x1_ref[...] = x1
    y = x1 * lax.rsqrt(jnp.mean(x1 * x1, axis=-1, keepdims=True) + EPS)
    h2_ref[...] = (y * nw_ref[...]).astype(h2_ref.dtype)


def _outproj(merged, w_out, x, norm2_w, bm=512):
    m, k = merged.shape
    d = w_out.shape[1]
    return pl.pallas_call(
        _outproj_kernel,
        grid=(m // bm,),
        in_specs=[pl.BlockSpec((bm, k), lambda i: (i, 0)),
                  pl.BlockSpec((k, d), lambda i: (0, 0), pipeline_mode=pl.Buffered(1)),
                  pl.BlockSpec((bm, d), lambda i: (i, 0)),
                  pl.BlockSpec((1, d), lambda i: (0, 0))],
        out_specs=[pl.BlockSpec((bm, d), lambda i: (i, 0)),
                   pl.BlockSpec((bm, d), lambda i: (i, 0))],
        out_shape=[jax.ShapeDtypeStruct((m, d), F32), jax.ShapeDtypeStruct((m, d), BF16)],
        scratch_shapes=[pltpu.VMEM((k, d), BF16)],
        compiler_params=_params(("arbitrary",)),
        name="outproj",
    )(merged, w_out, x, norm2_w.reshape(1, d))


def _ffn_up_kernel(h_ref, wg_ref, wu_ref, o_ref, wgb_ref, wub_ref):
    _load_weight(wg_ref, wgb_ref)
    _load_weight(wu_ref, wub_ref)
    splits = 4
    sub = o_ref.shape[0] // splits
    for r in range(splits):
        rows = slice(r * sub, (r + 1) * sub)
        h = h_ref[rows, :]
        g = jnp.dot(h, wgb_ref[...], preferred_element_type=F32)
        u = jnp.dot(h, wub_ref[...], preferred_element_type=F32)
        o_ref[rows, :] = (jax.nn.silu(g) * u).astype(o_ref.dtype)


def _ffn_up(h2, w_gate, w_up, bm=2048, bn=512):
    m, k = h2.shape
    n = w_gate.shape[1]
    return pl.pallas_call(
        _ffn_up_kernel,
        grid=(n // bn, m // bm),
        in_specs=[pl.BlockSpec((bm, k), lambda j, i: (i, 0)),
                  pl.BlockSpec((k, bn), lambda j, i: (0, j)),
                  pl.BlockSpec((k, bn), lambda j, i: (0, j))],
        out_specs=pl.BlockSpec((bm, bn), lambda j, i: (i, j)),
        out_shape=jax.ShapeDtypeStruct((m, n), BF16),
        scratch_shapes=[pltpu.VMEM((k, bn), BF16), pltpu.VMEM((k, bn), BF16)],
        compiler_params=_params(("parallel", "arbitrary")),
        name="ffn_up",
    )(h2, w_gate, w_up)


def _ffn_down_kernel(a_ref, w_ref, x_ref, o_ref, wb_ref):
    _load_weight(w_ref, wb_ref)
    o_ref[...] = x_ref[...] + jnp.dot(a_ref[...], wb_ref[...], preferred_element_type=F32)


def _ffn_down(act, w_down, x1, bm=512, bn=512):
    m, k = act.shape
    n = w_down.shape[1]
    return pl.pallas_call(
        _ffn_down_kernel,
        grid=(n // bn, m // bm),
        in_specs=[pl.BlockSpec((bm, k), lambda j, i: (i, 0)),
                  pl.BlockSpec((k, bn), lambda j, i: (0, j)),
                  pl.BlockSpec((bm, bn), lambda j, i: (i, j))],
        out_specs=pl.BlockSpec((bm, bn), lambda j, i: (i, j)),
        out_shape=jax.ShapeDtypeStruct((m, n), F32),
        scratch_shapes=[pltpu.VMEM((k, bn), BF16)],
        compiler_params=_params(("parallel", "arbitrary")),
        name="ffn_down",
    )(act, w_down, x1)


def _layer(x, norm1_w, w_in, q_norm_w, k_norm_w, conv_w, conv_b, w_rg_a, b_rg_a, w_rg_x, b_rg_x,
           lru_lambda, w_proj_attn, w_proj_lru, w_out, norm2_w, w_ffn_gate, w_ffn_up, w_ffn_down):
    d = x.shape[1]
    att_w = N_HEADS * HEAD_DIM
    lru_w = w_proj_lru.shape[0]

    h = _rmsnorm(x, norm1_w)
    qkv = _qkv_proj(h, w_in, q_norm_w, k_norm_w, 3 * att_w)
    rest = _rest_proj(h, w_in, 3 * att_w, 2 * lru_w + 2 * d)

    head = jnp.arange(1, N_HEADS + 1, dtype=F32)
    slopes = jnp.broadcast_to(jnp.exp2(-8.0 * head / N_HEADS)[:, None, None], (N_HEADS, 1, MOBA_BLOCK))
    att = _moba_attention(qkv, slopes)

    lru = _rglru(rest, conv_w, conv_b, w_rg_a, b_rg_a, w_rg_x, b_rg_x, lru_lambda, lru_w)

    merged = _merge(att, lru, w_proj_attn, w_proj_lru, rest, 2 * lru_w)
    x1, h2 = _outproj(merged, w_out, x, norm2_w)
    act = _ffn_up(h2, w_ffn_gate, w_ffn_up)
    return _ffn_down(act, w_ffn_down, x1)


def kernel(x, norm1_w, w_in, q_norm_w, k_norm_w, conv_w, conv_b, w_rg_a, b_rg_a, w_rg_x, b_rg_x,
           lru_lambda, w_proj_attn, w_proj_lru, w_out, norm2_w, w_ffn_gate, w_ffn_up, w_ffn_down):
    b, s, d = x.shape
    assert b == 1, "kernel handles the batch-1 prefill shape"
    y = x.reshape(s, d)
    for layer in range(norm1_w.shape[0]):
        y = _layer(y, norm1_w[layer], w_in[layer], q_norm_w[layer], k_norm_w[layer], conv_w[layer],
                   conv_b[layer], w_rg_a[layer], b_rg_a[layer], w_rg_x[layer], b_rg_x[layer],
                   lru_lambda[layer], w_proj_attn[layer], w_proj_lru[layer], w_out[layer],
                   norm2_w[layer], w_ffn_gate[layer], w_ffn_up[layer], w_ffn_down[layer])
    return y.reshape(b, s, d)
```

```python
import functools

import jax
import jax.numpy as jnp
from jax import lax
from jax.experimental import pallas as pl
from jax.experimental.pallas import tpu as pltpu

F32 = jnp.float32
BF16 = jnp.bfloat16

N_HEADS = 16
HEAD_DIM = 128
MOBA_BLOCK = 256
MOBA_TOPK = 3
LRU_BLOCK_W = 128
CONV_WIDTH = 4
LRU_C = 8.0
EPS = 1e-6
NEG_INF = -1e30
LOG2E = 1.4426950408889634
ONES_ROWS = 16

V7X_VMEM_BYTES = 64 * 1024 * 1024
VMEM_LIMIT = 56 * 1024 * 1024


def _params(semantics):
    return pltpu.CompilerParams(dimension_semantics=semantics, vmem_limit_bytes=VMEM_LIMIT)


def _rmsnorm_kernel(x_ref, w_ref, o_ref):
    x = x_ref[...]
    y = x * lax.rsqrt(jnp.mean(x * x, axis=-1, keepdims=True) + EPS)
    o_ref[...] = (y * w_ref[...]).astype(o_ref.dtype)


def _rmsnorm(x, w, tm=512):
    m, d = x.shape
    return pl.pallas_call(
        _rmsnorm_kernel,
        grid=(m // tm,),
        in_specs=[pl.BlockSpec((tm, d), lambda i: (i, 0)),
                  pl.BlockSpec((1, d), lambda i: (0, 0))],
        out_specs=pl.BlockSpec((tm, d), lambda i: (i, 0)),
        out_shape=jax.ShapeDtypeStruct((m, d), BF16),
        compiler_params=_params(("parallel",)),
        name="rmsnorm1",
    )(x, w.reshape(1, d))


def _load_weight(w_ref, wb_ref, row_axis=1):
    @pl.when(pl.program_id(row_axis) == 0)
    def _():
        wb_ref[...] = w_ref[...].astype(BF16)


def _qkv_kernel(h_ref, w_ref, qw_ref, kw_ref, o_ref, wb_ref, *, qk_tiles, row_splits):
    j = pl.program_id(0)
    _load_weight(w_ref, wb_ref)
    bm, bn = o_ref.shape
    sub = bm // row_splits

    @pl.when(j < qk_tiles)
    def _():
        nw = jnp.where(j < qk_tiles // 2, qw_ref[...], kw_ref[...])
        for r in range(row_splits):
            rows = slice(r * sub, (r + 1) * sub)
            acc = jnp.dot(h_ref[rows, :], wb_ref[...], preferred_element_type=F32)
            for hh in range(bn // HEAD_DIM):
                a = acc[:, hh * HEAD_DIM:(hh + 1) * HEAD_DIM]
                y = a * lax.rsqrt(jnp.mean(a * a, axis=-1, keepdims=True) + EPS)
                o_ref[rows, hh * HEAD_DIM:(hh + 1) * HEAD_DIM] = (y * nw).astype(o_ref.dtype)

    @pl.when(j >= qk_tiles)
    def _():
        o_ref[...] = jnp.dot(h_ref[...], wb_ref[...], preferred_element_type=F32).astype(o_ref.dtype)


def _qkv_proj(h, w_in, q_norm_w, k_norm_w, n_cols, bm=1024, bn=1024):
    m, k = h.shape
    return pl.pallas_call(
        functools.partial(_qkv_kernel, qk_tiles=(2 * N_HEADS * HEAD_DIM) // bn, row_splits=4),
        grid=(n_cols // bn, m // bm),
        in_specs=[pl.BlockSpec((bm, k), lambda j, i: (i, 0)),
                  pl.BlockSpec((k, bn), lambda j, i: (0, j)),
                  pl.BlockSpec((1, HEAD_DIM), lambda j, i: (0, 0)),
                  pl.BlockSpec((1, HEAD_DIM), lambda j, i: (0, 0))],
        out_specs=pl.BlockSpec((bm, bn), lambda j, i: (i, j)),
        out_shape=jax.ShapeDtypeStruct((m, n_cols), BF16),
        scratch_shapes=[pltpu.VMEM((k, bn), BF16)],
        compiler_params=_params(("parallel", "arbitrary")),
        name="qkv_proj",
    )(h, w_in, q_norm_w.reshape(1, HEAD_DIM), k_norm_w.reshape(1, HEAD_DIM))


def _matmul_kernel(a_ref, w_ref, o_ref, wb_ref):
    _load_weight(w_ref, wb_ref)
    o_ref[...] = jnp.dot(a_ref[...], wb_ref[...], preferred_element_type=F32).astype(o_ref.dtype)


def _rest_proj(h, w_in, col0, n_cols, bm=1024, bn=1024):
    m, k = h.shape
    jb = col0 // bn
    return pl.pallas_call(
        _matmul_kernel,
        grid=(n_cols // bn, m // bm),
        in_specs=[pl.BlockSpec((bm, k), lambda j, i: (i, 0)),
                  pl.BlockSpec((k, bn), lambda j, i: (0, jb + j))],
        out_specs=pl.BlockSpec((bm, bn), lambda j, i: (i, j)),
        out_shape=jax.ShapeDtypeStruct((m, n_cols), F32),
        scratch_shapes=[pltpu.VMEM((k, bn), BF16)],
        compiler_params=_params(("parallel", "arbitrary")),
        name="rest_proj",
    )(h, w_in)


def _attn_prep_kernel(slope_ref, q_ref, k_ref, v_ref, qt_ref, vt_ref, rb_ref, kmean_ref,
                      *, n_blk, group, cols):
    blk = MOBA_BLOCK
    for jb in range(n_blk):
        rows = slice(jb * blk, (jb + 1) * blk)
        qt_ref[:, rows] = q_ref[rows, :].astype(F32).T.astype(BF16)
        c, g = divmod(jb, group)
        vt_ref[c, 0:HEAD_DIM, g * blk:(g + 1) * blk] = v_ref[rows, :].astype(F32).T.astype(BF16)
        vt_ref[c, HEAD_DIM:, g * blk:(g + 1) * blk] = jnp.ones((ONES_ROWS, blk), BF16)
        kmean_ref[jb:jb + 1, :] = jnp.sum(k_ref[rows, :].astype(F32), axis=0, keepdims=True) * (1.0 / blk)
    kmean = kmean_ref[...].astype(BF16)
    slope = slope_ref[:, 0:1]
    row = lax.broadcasted_iota(jnp.int32, (n_blk, cols), 0)
    col = lax.broadcasted_iota(jnp.int32, (n_blk, cols), 1)
    for qc in range(qt_ref.shape[1] // cols):
        csl = slice(qc * cols, (qc + 1) * cols)
        gate = jnp.dot(kmean, qt_ref[:, csl], preferred_element_type=F32)
        qblk = lax.shift_right_logical(col + qc * cols, blk.bit_length() - 1)
        g = jnp.where(row < qblk, gate, NEG_INF)
        bias = jnp.full((n_blk, cols), NEG_INF, F32)
        for r in range(MOBA_TOPK):
            mx = jnp.max(g, axis=0, keepdims=True)
            idx = jnp.min(jnp.where(g == mx, row, n_blk), axis=0, keepdims=True)
            pick = row == idx
            bias = jnp.where(pick, jnp.where(qblk > r, 0.0, NEG_INF), bias)
            g = jnp.where(pick, -jnp.inf, g)
        past = bias - slope * ((qblk - row) * blk).astype(F32)
        rb = jnp.where(row == qblk, 0.0, jnp.where(row < qblk, past, NEG_INF))
        rb_ref[:, csl] = rb * LOG2E


def _attn_kernel(slope_ref, qt_ref, k_ref, vt_ref, rb_ref, rbp_ref, o_ref, *scratch,
                 group, n_blk, heads, nq):
    u_refs, m_refs, dmat_refs, acc_refs, p_refs = (scratch[n * heads:(n + 1) * heads] for n in range(5))
    blk = MOBA_BLOCK
    cb = group * blk
    lg = group.bit_length() - 1
    i = pl.program_id(1)

    @pl.when(i == 0)
    def _per_head_setup():
        kk = lax.broadcasted_iota(jnp.int32, (blk, blk), 0)
        qq = lax.broadcasted_iota(jnp.int32, (blk, blk), 1)
        for hh in range(heads):
            d = slope_ref[hh][:, 0:1] * (qq - kk).astype(F32) * LOG2E
            dmat_refs[hh][0] = d
            dmat_refs[hh][1] = jnp.where(kk <= qq, d, -NEG_INF)
            m_refs[hh][...] = jnp.zeros_like(m_refs[hh])

    k2 = (HEAD_DIM ** -0.5) * LOG2E
    n1 = jnp.where(i < n_blk // nq, lax.shift_right_logical(nq * (i + 1) + group - 1, lg), 0)
    n2 = lax.shift_right_logical(nq * i + group - 1, lg)
    common = jnp.minimum(n1, n2)
    m_prev = [m_refs[hh][...] for hh in range(heads)]
    for hh in range(heads):
        acc_refs[hh][...] = jnp.zeros_like(acc_refs[hh])
        p_refs[hh][...] = jnp.zeros_like(p_refs[hh])

    def pass1(hh, c, mx):
        r0 = pl.multiple_of(c * cb, cb)
        k_chunk = k_ref[pl.ds(r0, cb), hh * HEAD_DIM:(hh + 1) * HEAD_DIM]
        s = jnp.dot(k_chunk, qt_ref[hh], preferred_element_type=F32)
        for g in range(group):
            j = c * group + g
            rows = slice(g * blk, (g + 1) * blk)
            cands = []
            for qb in range(nq):
                cols = slice(qb * blk, (qb + 1) * blk)
                own = (j == nq * i + qb).astype(jnp.int32)
                u = s[rows, cols] * k2 - dmat_refs[hh][own]
                u_refs[hh][c, rows, cols] = u
                cands.append(jnp.max(u, axis=0, keepdims=True))
            mx = jnp.maximum(mx, jnp.concatenate(cands, axis=1) + rb_ref[hh, pl.ds(j, 1), :])
        return mx

    def probs(hh, c):
        for g in range(group):
            j = c * group + g
            rows = slice(g * blk, (g + 1) * blk)
            p = jnp.exp2(u_refs[hh][c, rows, :] + (rbp_ref[hh, pl.ds(j, 1), :] - m_prev[hh]))
            p_refs[hh][rows, :] = p.astype(BF16)

    def pv(hh, c):
        acc_refs[hh][...] += jnp.dot(vt_ref[hh, jnp.maximum(c - 1, 0)], p_refs[hh][...],
                                     preferred_element_type=F32)

    def only1(c, mxs):
        return tuple(pass1(hh, c, mxs[hh]) for hh in range(heads))

    def only2(c, carry):
        for hh in range(heads):
            pv(hh, c)
        for hh in range(heads):
            probs(hh, c)
        return carry

    def both(c, mxs):
        for hh in range(heads):
            pv(hh, c)
        out = []
        for hh in range(heads):
            probs(hh, c)
            out.append(pass1(hh, c, mxs[hh]))
        return tuple(out)

    mxs = lax.fori_loop(0, common, both,
                        tuple(jnp.full((1, nq * blk), -jnp.inf, F32) for _ in range(heads)))
    mxs = lax.fori_loop(common, n1, only1, mxs)
    lax.fori_loop(common, n2, only2, 0)
    for hh in range(heads):
        pv(hh, n2)
        m_refs[hh][...] = mxs[hh]

    @pl.when(i > 0)
    def _():
        for hh in range(heads):
            acc = acc_refs[hh][...]
            o = acc[0:HEAD_DIM, :] / acc[HEAD_DIM:HEAD_DIM + 1, :]
            for qb in range(nq):
                o_ref[qb * blk:(qb + 1) * blk, hh * HEAD_DIM:(hh + 1) * HEAD_DIM] = (
                    o[:, qb * blk:(qb + 1) * blk].T.astype(o_ref.dtype))


def _moba_attention(qkv, slopes, group=4, cols=2048, heads=2, nq=2):
    s = qkv.shape[0]
    blk = MOBA_BLOCK
    n_blk = s // blk
    n_chunk = n_blk // group
    vt_rows = HEAD_DIM + ONES_ROWS
    qt, vt, rb = pl.pallas_call(
        functools.partial(_attn_prep_kernel, n_blk=n_blk, group=group, cols=cols),
        grid=(N_HEADS,),
        in_specs=[pl.BlockSpec((None, 1, blk), lambda h: (h, 0, 0)),
                  pl.BlockSpec((s, HEAD_DIM), lambda h: (0, h)),
                  pl.BlockSpec((s, HEAD_DIM), lambda h: (0, N_HEADS + h)),
                  pl.BlockSpec((s, HEAD_DIM), lambda h: (0, 2 * N_HEADS + h))],
        out_specs=[pl.BlockSpec((None, HEAD_DIM, s), lambda h: (h, 0, 0)),
                   pl.BlockSpec((None, n_chunk, vt_rows, group * blk), lambda h: (h, 0, 0, 0)),
                   pl.BlockSpec((None, n_blk, s), lambda h: (h, 0, 0))],
        out_shape=[jax.ShapeDtypeStruct((N_HEADS, HEAD_DIM, s), BF16),
                   jax.ShapeDtypeStruct((N_HEADS, n_chunk, vt_rows, group * blk), BF16),
                   jax.ShapeDtypeStruct((N_HEADS, n_blk, s), F32)],
        scratch_shapes=[pltpu.VMEM((n_blk, HEAD_DIM), F32)],
        compiler_params=_params(("parallel",)),
        name="moba_prep",
    )(slopes, qkv, qkv, qkv)
    n_tiles = n_blk // nq
    last = n_tiles - 1
    qw = nq * blk
    hw = heads * HEAD_DIM
    k_col0 = (N_HEADS * HEAD_DIM) // hw
    once = pl.Buffered(1)
    per_head = lambda shape: [pltpu.VMEM(shape, F32) for _ in range(heads)]
    return pl.pallas_call(
        functools.partial(_attn_kernel, group=group, n_blk=n_blk, heads=heads, nq=nq),
        grid=(N_HEADS // heads, n_tiles + 1),
        in_specs=[pl.BlockSpec((heads, 1, blk), lambda h, i: (h, 0, 0)),
                  pl.BlockSpec((heads, HEAD_DIM, qw), lambda h, i: (h, 0, jnp.minimum(i, last))),
                  pl.BlockSpec((s, hw), lambda h, i: (0, k_col0 + h), pipeline_mode=once),
                  pl.BlockSpec((heads, n_chunk, vt_rows, group * blk), lambda h, i: (h, 0, 0, 0),
                               pipeline_mode=once),
                  pl.BlockSpec((heads, n_blk, qw), lambda h, i: (h, 0, jnp.minimum(i, last))),
                  pl.BlockSpec((heads, n_blk, qw), lambda h, i: (h, 0, jnp.maximum(i - 1, 0)))],
        out_specs=pl.BlockSpec((qw, hw), lambda h, i: (jnp.maximum(i - 1, 0), h)),
        out_shape=jax.ShapeDtypeStruct((s, N_HEADS * HEAD_DIM), BF16),
        scratch_shapes=(per_head((n_chunk, group * blk, qw))
                        + per_head((1, qw))
                        + per_head((2, blk, blk))
                        + per_head((vt_rows, qw))
                        + [pltpu.VMEM((group * blk, qw), BF16) for _ in range(heads)]),
        compiler_params=_params(("parallel", "arbitrary")),
        name="moba_attention",
    )(slopes, qt, qkv, vt, rb, rb)


def _lru_kernel(xr_ref, yr_ref, cw_ref, cb_ref, wa_ref, ba_ref, wx_ref, bx_ref, lam_ref, o_ref,
                xbuf_ref, a_ref, b_ref, h_ref, wab_ref, wxb_ref, *, ts, tc):
    t = pl.program_id(1)
    pad = 8

    @pl.when(t == 0)
    def _():
        xbuf_ref[0:pad, :] = jnp.zeros((pad, tc), F32)
        h_ref[...] = jnp.zeros_like(h_ref)
        wab_ref[...] = wa_ref[...].astype(BF16)
        wxb_ref[...] = wx_ref[...].astype(BF16)

    xbuf_ref[pad:pad + ts, :] = xr_ref[...]
    cw = cw_ref[...]
    u = cb_ref[...]
    for tap in range(CONV_WIDTH):
        off = pad - (CONV_WIDTH - 1) + tap
        u = u + xbuf_ref[off:off + ts, :] * cw[tap:tap + 1, :]
    xbuf_ref[0:pad, :] = xbuf_ref[ts:ts + pad, :]

    ub = u.astype(BF16)
    ga, gx = [], []
    for n in range(tc // LRU_BLOCK_W):
        un = ub[:, n * LRU_BLOCK_W:(n + 1) * LRU_BLOCK_W]
        ga.append(jnp.dot(un, wab_ref[n], preferred_element_type=F32))
        gx.append(jnp.dot(un, wxb_ref[n], preferred_element_type=F32))
    r = jax.nn.sigmoid(jnp.concatenate(ga, axis=1) + ba_ref[...])
    ig = jax.nn.sigmoid(jnp.concatenate(gx, axis=1) + bx_ref[...])
    log_a = -LRU_C * r * jax.nn.softplus(-lam_ref[...])
    a = jnp.exp(log_a)
    a_ref[...] = a
    one_minus_a2 = -jnp.tanh(log_a) * (a * a + 1.0)
    mult = jnp.where(one_minus_a2 == 0.0, 0.0, one_minus_a2 * lax.rsqrt(one_minus_a2))
    b_ref[...] = mult * ig * u

    row = lax.broadcasted_iota(jnp.int32, (8, tc), 0)

    def group(g, hprev):
        r0 = pl.multiple_of(g * 8, 8)
        av = a_ref[pl.ds(r0, 8), :]
        bv = b_ref[pl.ds(r0, 8), :]
        for d in (1, 2, 4):
            keep = row >= d
            a_sh = pltpu.roll(av, d, 0)
            b_sh = pltpu.roll(bv, d, 0)
            bv = jnp.where(keep, av * b_sh + bv, bv)
            av = jnp.where(keep, av * a_sh, av)
        hv = av * hprev + bv
        b_ref[pl.ds(r0, 8), :] = hv
        return jnp.broadcast_to(hv[7:8, :], (8, tc))

    h_ref[...] = lax.fori_loop(0, ts // 8, group, h_ref[...])
    o_ref[...] = (b_ref[...] * jax.nn.gelu(yr_ref[...])).astype(o_ref.dtype)


def _rglru(rest, conv_w, conv_b, w_rg_a, b_rg_a, w_rg_x, b_rg_x, lru_lambda, width, ts=512, tc=512):
    s = rest.shape[0]
    nct = width // tc
    nb = tc // LRU_BLOCK_W
    vec = lambda v: v.reshape(1, width)
    vspec = pl.BlockSpec((1, tc), lambda c, t: (0, c))
    wspec = pl.BlockSpec((nb, LRU_BLOCK_W, LRU_BLOCK_W), lambda c, t: (c, 0, 0))
    return pl.pallas_call(
        functools.partial(_lru_kernel, ts=ts, tc=tc),
        grid=(nct, s // ts),
        in_specs=[pl.BlockSpec((ts, tc), lambda c, t: (t, c)),
                  pl.BlockSpec((ts, tc), lambda c, t: (t, nct + c)),
                  pl.BlockSpec((CONV_WIDTH, tc), lambda c, t: (0, c)),
                  vspec, wspec, vspec, wspec, vspec, vspec],
        out_specs=pl.BlockSpec((ts, tc), lambda c, t: (t, c)),
        out_shape=jax.ShapeDtypeStruct((s, width), BF16),
        scratch_shapes=[pltpu.VMEM((ts + 8, tc), F32),
                        pltpu.VMEM((ts, tc), F32),
                        pltpu.VMEM((ts, tc), F32),
                        pltpu.VMEM((8, tc), F32),
                        pltpu.VMEM((nb, LRU_BLOCK_W, LRU_BLOCK_W), BF16),
                        pltpu.VMEM((nb, LRU_BLOCK_W, LRU_BLOCK_W), BF16)],
        compiler_params=_params(("parallel", "arbitrary")),
        name="rglru",
    )(rest, rest, conv_w, vec(conv_b), w_rg_a, vec(b_rg_a), w_rg_x, vec(b_rg_x), vec(lru_lambda))


def _merge_kernel(att_ref, lru_ref, wa_ref, wl_ref, ga_ref, gl_ref, o_ref, wab_ref, wlb_ref):
    _load_weight(wa_ref, wab_ref)
    _load_weight(wl_ref, wlb_ref)
    bm = o_ref.shape[0]
    sub = bm // 2
    for r in range(2):
        rows = slice(r * sub, (r + 1) * sub)
        pa = jnp.dot(att_ref[rows, :], wab_ref[...], preferred_element_type=F32)
        plru = jnp.dot(lru_ref[rows, :], wlb_ref[...], preferred_element_type=F32)
        o_ref[rows, :] = (jax.nn.sigmoid(ga_ref[rows, :]) * pa
                          + jax.nn.sigmoid(gl_ref[rows, :]) * plru).astype(o_ref.dtype)


def _merge(att, lru, w_att, w_lru, rest, gate_col0, bm=512, bn=1024):
    m, k = att.shape
    n = w_att.shape[1]
    ga0 = gate_col0 // bn
    gl0 = (gate_col0 + n) // bn
    once = pl.Buffered(1)
    return pl.pallas_call(
        _merge_kernel,
        grid=(n // bn, m // bm),
        in_specs=[pl.BlockSpec((bm, k), lambda j, i: (i, 0)),
                  pl.BlockSpec((bm, k), lambda j, i: (i, 0)),
                  pl.BlockSpec((k, bn), lambda j, i: (0, j), pipeline_mode=once),
                  pl.BlockSpec((k, bn), lambda j, i: (0, j), pipeline_mode=once),
                  pl.BlockSpec((bm, bn), lambda j, i: (i, ga0 + j)),
                  pl.BlockSpec((bm, bn), lambda j, i: (i, gl0 + j))],
        out_specs=pl.BlockSpec((bm, bn), lambda j, i: (i, j)),
        out_shape=jax.ShapeDtypeStruct((m, n), BF16),
        scratch_shapes=[pltpu.VMEM((k, bn), BF16), pltpu.VMEM((k, bn), BF16)],
        compiler_params=_params(("parallel", "arbitrary")),
        name="merge",
    )(att, lru, w_att, w_lru, rest, rest)


def _outproj_kernel(a_ref, w_ref, x_ref, nw_ref, x1_ref, h2_ref, wb_ref):
    _load_weight(w_ref, wb_ref, row_axis=0)
    x1 = x_ref[...] + jnp.dot(a_ref[...], wb_ref[...], preferred_element_type=F32)
    x1_ref[...] = x1
    y = x1 * lax.rsqrt(jnp.mean(x1 * x1, axis=-1, keepdims=True) + EPS)
    h2_ref[...] = (y * nw_ref[...]).astype(h2_ref.dtype)


def _outproj(merged, w_out, x, norm2_w, bm=512):
    m, k = merged.shape
    d = w_out.shape[1]
    return pl.pallas_call(
        _outproj_kernel,
        grid=(m // bm,),
        in_specs=[pl.BlockSpec((bm, k), lambda i: (i, 0)),
                  pl.BlockSpec((k, d), lambda i: (0, 0), pipeline_mode=pl.Buffered(1)),
                  pl.BlockSpec((bm, d), lambda i: (i, 0)),
                  pl.BlockSpec((1, d), lambda i: (0, 0))],
        out_specs=[pl.BlockSpec((bm, d), lambda i: (i, 0)),
                   pl.BlockSpec((bm, d), lambda i: (i, 0))],
        out_shape=[jax.ShapeDtypeStruct((m, d), F32), jax.ShapeDtypeStruct((m, d), BF16)],
        scratch_shapes=[pltpu.VMEM((k, d), BF16)],
        compiler_params=_params(("arbitrary",)),
        name="outproj",
    )(merged, w_out, x, norm2_w.reshape(1, d))


def _ffn_up_kernel(h_ref, wg_ref, wu_ref, o_ref, wgb_ref, wub_ref):
    _load_weight(wg_ref, wgb_ref)
    _load_weight(wu_ref, wub_ref)
    splits = 4
    sub = o_ref.shape[0] // splits
    for r in range(splits):
        rows = slice(r * sub, (r + 1) * sub)
        h = h_ref[rows, :]
        g = jnp.dot(h, wgb_ref[...], preferred_element_type=F32)
        u = jnp.dot(h, wub_ref[...], preferred_element_type=F32)
        o_ref[rows, :] = (jax.nn.silu(g) * u).astype(o_ref.dtype)


def _ffn_up(h2, w_gate, w_up, bm=2048, bn=512):
    m, k = h2.shape
    n = w_gate.shape[1]
    return pl.pallas_call(
        _ffn_up_kernel,
        grid=(n // bn, m // bm),
        in_specs=[pl.BlockSpec((bm, k), lambda j, i: (i, 0)),
                  pl.BlockSpec((k, bn), lambda j, i: (0, j)),
                  pl.BlockSpec((k, bn), lambda j, i: (0, j))],
        out_specs=pl.BlockSpec((bm, bn), lambda j, i: (i, j)),
        out_shape=jax.ShapeDtypeStruct((m, n), BF16),
        scratch_shapes=[pltpu.VMEM((k, bn), BF16), pltpu.VMEM((k, bn), BF16)],
        compiler_params=_params(("parallel", "arbitrary")),
        name="ffn_up",
    )(h2, w_gate, w_up)


def _ffn_down_kernel(a_ref, w_ref, x_ref, o_ref, wb_ref):
    _load_weight(w_ref, wb_ref)
    o_ref[...] = x_ref[...] + jnp.dot(a_ref[...], wb_ref[...], preferred_element_type=F32)


def _ffn_down(act, w_down, x1, bm=512, bn=512):
    m, k = act.shape
    n = w_down.shape[1]
    return pl.pallas_call(
        _ffn_down_kernel,
        grid=(n // bn, m // bm),
        in_specs=[pl.BlockSpec((bm, k), lambda j, i: (i, 0)),
                  pl.BlockSpec((k, bn), lambda j, i: (0, j)),
                  pl.BlockSpec((bm, bn), lambda j, i: (i, j))],
        out_specs=pl.BlockSpec((bm, bn), lambda j, i: (i, j)),
        out_shape=jax.ShapeDtypeStruct((m, n), F32),
        scratch_shapes=[pltpu.VMEM((k, bn), BF16)],
        compiler_params=_params(("parallel", "arbitrary")),
        name="ffn_down",
    )(act, w_down, x1)


def _layer(x, norm1_w, w_in, q_norm_w, k_norm_w, conv_w, conv_b, w_rg_a, b_rg_a, w_rg_x, b_rg_x,
           lru_lambda, w_proj_attn, w_proj_lru, w_out, norm2_w, w_ffn_gate, w_ffn_up, w_ffn_down):
    d = x.shape[1]
    att_w = N_HEADS * HEAD_DIM
    lru_w = w_proj_lru.shape[0]

    h = _rmsnorm(x, norm1_w)
    qkv = _qkv_proj(h, w_in, q_norm_w, k_norm_w, 3 * att_w)
    rest = _rest_proj(h, w_in, 3 * att_w, 2 * lru_w + 2 * d)

    head = jnp.arange(1, N_HEADS + 1, dtype=F32)
    slopes = jnp.broadcast_to(jnp.exp2(-8.0 * head / N_HEADS)[:, None, None], (N_HEADS, 1, MOBA_BLOCK))
    att = _moba_attention(qkv, slopes)

    lru = _rglru(rest, conv_w, conv_b, w_rg_a, b_rg_a, w_rg_x, b_rg_x, lru_lambda, lru_w)

    merged = _merge(att, lru, w_proj_attn, w_proj_lru, rest, 2 * lru_w)
    x1, h2 = _outproj(merged, w_out, x, norm2_w)
    act = _ffn_up(h2, w_ffn_gate, w_ffn_up)
    return _ffn_down(act, w_ffn_down, x1)


def kernel(x, norm1_w, w_in, q_norm_w, k_norm_w, conv_w, conv_b, w_rg_a, b_rg_a, w_rg_x, b_rg_x,
           lru_lambda, w_proj_attn, w_proj_lru, w_out, norm2_w, w_ffn_gate, w_ffn_up, w_ffn_down):
    b, s, d = x.shape
    assert b == 1, "kernel handles the batch-1 prefill shape"
    y = x.reshape(s, d)
    for layer in range(norm1_w.shape[0]):
        y = _layer(y, norm1_w[layer], w_in[layer], q_norm_w[layer], k_norm_w[layer], conv_w[layer],
                   conv_b[layer], w_rg_a[layer], b_rg_a[layer], w_rg_x[layer], b_rg_x[layer],
                   lru_lambda[layer], w_proj_attn[layer], w_proj_lru[layer], w_out[layer],
                   norm2_w[layer], w_ffn_gate[layer], w_ffn_up[layer], w_ffn_down[layer])
    return y.reshape(b, s, d)
```

```python
import functools

import jax
import jax.numpy as jnp
from jax import lax
from jax.experimental import pallas as pl
from jax.experimental.pallas import tpu as pltpu

F32 = jnp.float32
BF16 = jnp.bfloat16

N_HEADS = 16
HEAD_DIM = 128
MOBA_BLOCK = 256
MOBA_TOPK = 3
LRU_BLOCK_W = 128
CONV_WIDTH = 4
LRU_C = 8.0
EPS = 1e-6
NEG_INF = -1e30
LOG2E = 1.4426950408889634
ONES_ROWS = 16

V7X_VMEM_BYTES = 64 * 1024 * 1024
VMEM_LIMIT = 56 * 1024 * 1024


def _params(semantics):
    return pltpu.CompilerParams(dimension_semantics=semantics, vmem_limit_bytes=VMEM_LIMIT)


def _rmsnorm_kernel(x_ref, w_ref, o_ref):
    x = x_ref[...]
    y = x * lax.rsqrt(jnp.mean(x * x, axis=-1, keepdims=True) + EPS)
    o_ref[...] = (y * w_ref[...]).astype(o_ref.dtype)


def _rmsnorm(x, w, tm=512):
    m, d = x.shape
    return pl.pallas_call(
        _rmsnorm_kernel,
        grid=(m // tm,),
        in_specs=[pl.BlockSpec((tm, d), lambda i: (i, 0)),
                  pl.BlockSpec((1, d), lambda i: (0, 0))],
        out_specs=pl.BlockSpec((tm, d), lambda i: (i, 0)),
        out_shape=jax.ShapeDtypeStruct((m, d), BF16),
        compiler_params=_params(("parallel",)),
        name="rmsnorm1",
    )(x, w.reshape(1, d))


def _load_weight(w_ref, wb_ref, row_axis=1):
    @pl.when(pl.program_id(row_axis) == 0)
    def _():
        wb_ref[...] = w_ref[...].astype(BF16)


def _qkv_kernel(h_ref, w_ref, qw_ref, kw_ref, o_ref, wb_ref, *, qk_tiles, row_splits):
    j = pl.program_id(0)
    _load_weight(w_ref, wb_ref)
    bm, bn = o_ref.shape
    sub = bm // row_splits

    @pl.when(j < qk_tiles)
    def _():
        nw = jnp.where(j < qk_tiles // 2, qw_ref[...], kw_ref[...])
        for r in range(row_splits):
            rows = slice(r * sub, (r + 1) * sub)
            acc = jnp.dot(h_ref[rows, :], wb_ref[...], preferred_element_type=F32)
            for hh in range(bn // HEAD_DIM):
                a = acc[:, hh * HEAD_DIM:(hh + 1) * HEAD_DIM]
                y = a * lax.rsqrt(jnp.mean(a * a, axis=-1, keepdims=True) + EPS)
                o_ref[rows, hh * HEAD_DIM:(hh + 1) * HEAD_DIM] = (y * nw).astype(o_ref.dtype)

    @pl.when(j >= qk_tiles)
    def _():
        o_ref[...] = jnp.dot(h_ref[...], wb_ref[...], preferred_element_type=F32).astype(o_ref.dtype)


def _qkv_proj(h, w_in, q_norm_w, k_norm_w, n_cols, bm=1024, bn=1024):
    m, k = h.shape
    return pl.pallas_call(
        functools.partial(_qkv_kernel, qk_tiles=(2 * N_HEADS * HEAD_DIM) // bn, row_splits=4),
        grid=(n_cols // bn, m // bm),
        in_specs=[pl.BlockSpec((bm, k), lambda j, i: (i, 0)),
                  pl.BlockSpec((k, bn), lambda j, i: (0, j)),
                  pl.BlockSpec((1, HEAD_DIM), lambda j, i: (0, 0)),
                  pl.BlockSpec((1, HEAD_DIM), lambda j, i: (0, 0))],
        out_specs=pl.BlockSpec((bm, bn), lambda j, i: (i, j)),
        out_shape=jax.ShapeDtypeStruct((m, n_cols), BF16),
        scratch_shapes=[pltpu.VMEM((k, bn), BF16)],
        compiler_params=_params(("parallel", "arbitrary")),
        name="qkv_proj",
    )(h, w_in, q_norm_w.reshape(1, HEAD_DIM), k_norm_w.reshape(1, HEAD_DIM))


def _matmul_kernel(a_ref, w_ref, o_ref, wb_ref):
    _load_weight(w_ref, wb_ref)
    o_ref[...] = jnp.dot(a_ref[...], wb_ref[...], preferred_element_type=F32).astype(o_ref.dtype)


def _rest_proj(h, w_in, col0, n_cols, bm=1024, bn=1024):
    m, k = h.shape
    jb = col0 // bn
    return pl.pallas_call(
        _matmul_kernel,
        grid=(n_cols // bn, m // bm),
        in_specs=[pl.BlockSpec((bm, k), lambda j, i: (i, 0)),
                  pl.BlockSpec((k, bn), lambda j, i: (0, jb + j))],
        out_specs=pl.BlockSpec((bm, bn), lambda j, i: (i, j)),
        out_shape=jax.ShapeDtypeStruct((m, n_cols), F32),
        scratch_shapes=[pltpu.VMEM((k, bn), BF16)],
        compiler_params=_params(("parallel", "arbitrary")),
        name="rest_proj",
    )(h, w_in)


def _attn_prep_kernel(slope_ref, q_ref, k_ref, v_ref, qt_ref, vt_ref, rb_ref, kmean_ref,
                      *, n_blk, group, cols):
    blk = MOBA_BLOCK
    for jb in range(n_blk):
        rows = slice(jb * blk, (jb + 1) * blk)
        qt_ref[:, rows] = q_ref[rows, :].astype(F32).T.astype(BF16)
        c, g = divmod(jb, group)
        vt_ref[c, 0:HEAD_DIM, g * blk:(g + 1) * blk] = v_ref[rows, :].astype(F32).T.astype(BF16)
        vt_ref[c, HEAD_DIM:, g * blk:(g + 1) * blk] = jnp.ones((ONES_ROWS, blk), BF16)
        kmean_ref[jb:jb + 1, :] = jnp.sum(k_ref[rows, :].astype(F32), axis=0, keepdims=True) * (1.0 / blk)
    vt_ref[n_blk // group] = jnp.zeros(vt_ref.shape[1:], BF16)
    kmean = kmean_ref[...].astype(BF16)
    slope = slope_ref[:, 0:1]
    row = lax.broadcasted_iota(jnp.int32, (n_blk, cols), 0)
    col = lax.broadcasted_iota(jnp.int32, (n_blk, cols), 1)
    for qc in range(qt_ref.shape[1] // cols):
        csl = slice(qc * cols, (qc + 1) * cols)
        gate = jnp.dot(kmean, qt_ref[:, csl], preferred_element_type=F32)
        qblk = lax.shift_right_logical(col + qc * cols, blk.bit_length() - 1)
        g = jnp.where(row < qblk, gate, NEG_INF)
        bias = jnp.full((n_blk, cols), NEG_INF, F32)
        for r in range(MOBA_TOPK):
            mx = jnp.max(g, axis=0, keepdims=True)
            idx = jnp.min(jnp.where(g == mx, row, n_blk), axis=0, keepdims=True)
            pick = row == idx
            bias = jnp.where(pick, jnp.where(qblk > r, 0.0, NEG_INF), bias)
            g = jnp.where(pick, -jnp.inf, g)
        past = bias - slope * ((qblk - row) * blk).astype(F32)
        rb = jnp.where(row == qblk, 0.0, jnp.where(row < qblk, past, NEG_INF))
        rb_ref[:, csl] = rb * LOG2E


def _attn_kernel(slope_ref, qt_ref, k_ref, vt_ref, rb_ref, rbp_ref, o_ref, *scratch,
                 group, n_blk, heads, nq):
    u_refs, m_refs, dmat_refs, acc_refs, p_refs = (scratch[n * heads:(n + 1) * heads] for n in range(5))
    blk = MOBA_BLOCK
    cb = group * blk
    lg = group.bit_length() - 1
    i = pl.program_id(1)

    @pl.when(i == 0)
    def _per_head_setup():
        kk = lax.broadcasted_iota(jnp.int32, (blk, blk), 0)
        qq = lax.broadcasted_iota(jnp.int32, (blk, blk), 1)
        for hh in range(heads):
            d = slope_ref[hh][:, 0:1] * (qq - kk).astype(F32) * LOG2E
            dmat_refs[hh][0] = d
            dmat_refs[hh][1] = jnp.where(kk <= qq, d, -NEG_INF)
            m_refs[hh][...] = jnp.zeros_like(m_refs[hh])
            p_refs[hh][...] = jnp.zeros_like(p_refs[hh])

    k2 = (HEAD_DIM ** -0.5) * LOG2E
    n1 = jnp.where(i < n_blk // nq, lax.shift_right_logical(nq * (i + 1) + group - 1, lg), 0)
    n2 = lax.shift_right_logical(nq * i + group - 1, lg)
    common = jnp.minimum(n1, n2)
    m_prev = [m_refs[hh][...] for hh in range(heads)]
    for hh in range(heads):
        acc_refs[hh][...] = jnp.zeros_like(acc_refs[hh])

    def pass1(hh, c, mx):
        r0 = pl.multiple_of(c * cb, cb)
        k_chunk = k_ref[pl.ds(r0, cb), hh * HEAD_DIM:(hh + 1) * HEAD_DIM]
        s = jnp.dot(k_chunk, qt_ref[hh], preferred_element_type=F32)
        for g in range(group):
            j = c * group + g
            rows = slice(g * blk, (g + 1) * blk)
            cands = []
            for qb in range(nq):
                cols = slice(qb * blk, (qb + 1) * blk)
                own = (j == nq * i + qb).astype(jnp.int32)
                u = s[rows, cols] * k2 - dmat_refs[hh][own]
                u_refs[hh][c, rows, cols] = u
                cands.append(jnp.max(u, axis=0, keepdims=True))
            mx = jnp.maximum(mx, jnp.concatenate(cands, axis=1) + rb_ref[hh, pl.ds(j, 1), :])
        return mx

    def probs(hh, c):
        for g in range(group):
            j = c * group + g
            rows = slice(g * blk, (g + 1) * blk)
            p = jnp.exp2(u_refs[hh][c, rows, :] + (rbp_ref[hh, pl.ds(j, 1), :] - m_prev[hh]))
            p_refs[hh][rows, :] = p.astype(BF16)

    def pv(hh, c):
        n_chunk = vt_ref.shape[1] - 1
        acc_refs[hh][...] += jnp.dot(vt_ref[hh, jnp.where(c == 0, n_chunk, c - 1)], p_refs[hh][...],
                                     preferred_element_type=F32)

    def only1(c, mxs):
        return tuple(pass1(hh, c, mxs[hh]) for hh in range(heads))

    def only2(c, carry):
        for hh in range(heads):
            pv(hh, c)
        for hh in range(heads):
            probs(hh, c)
        return carry

    def both(c, mxs):
        for hh in range(heads):
            pv(hh, c)
        out = []
        for hh in range(heads):
            probs(hh, c)
            out.append(pass1(hh, c, mxs[hh]))
        return tuple(out)

    mxs = lax.fori_loop(0, common, both,
                        tuple(jnp.full((1, nq * blk), -jnp.inf, F32) for _ in range(heads)))
    mxs = lax.fori_loop(common, n1, only1, mxs)
    lax.fori_loop(common, n2, only2, 0)
    for hh in range(heads):
        pv(hh, n2)
        m_refs[hh][...] = mxs[hh]

    @pl.when(i > 0)
    def _():
        for hh in range(heads):
            acc = acc_refs[hh][...]
            o = acc[0:HEAD_DIM, :] / acc[HEAD_DIM:HEAD_DIM + 1, :]
            for qb in range(nq):
                o_ref[qb * blk:(qb + 1) * blk, hh * HEAD_DIM:(hh + 1) * HEAD_DIM] = (
                    o[:, qb * blk:(qb + 1) * blk].T.astype(o_ref.dtype))


def _moba_attention(qkv, slopes, group=4, cols=2048, heads=2, nq=2):
    s = qkv.shape[0]
    blk = MOBA_BLOCK
    n_blk = s // blk
    n_chunk = n_blk // group
    vt_rows = HEAD_DIM + ONES_ROWS
    qt, vt, rb = pl.pallas_call(
        functools.partial(_attn_prep_kernel, n_blk=n_blk, group=group, cols=cols),
        grid=(N_HEADS,),
        in_specs=[pl.BlockSpec((None, 1, blk), lambda h: (h, 0, 0)),
                  pl.BlockSpec((s, HEAD_DIM), lambda h: (0, h)),
                  pl.BlockSpec((s, HEAD_DIM), lambda h: (0, N_HEADS + h)),
                  pl.BlockSpec((s, HEAD_DIM), lambda h: (0, 2 * N_HEADS + h))],
        out_specs=[pl.BlockSpec((None, HEAD_DIM, s), lambda h: (h, 0, 0)),
                   pl.BlockSpec((None, n_chunk + 1, vt_rows, group * blk), lambda h: (h, 0, 0, 0)),
                   pl.BlockSpec((None, n_blk, s), lambda h: (h, 0, 0))],
        out_shape=[jax.ShapeDtypeStruct((N_HEADS, HEAD_DIM, s), BF16),
                   jax.ShapeDtypeStruct((N_HEADS, n_chunk + 1, vt_rows, group * blk), BF16),
                   jax.ShapeDtypeStruct((N_HEADS, n_blk, s), F32)],
        scratch_shapes=[pltpu.VMEM((n_blk, HEAD_DIM), F32)],
        compiler_params=_params(("parallel",)),
        name="moba_prep",
    )(slopes, qkv, qkv, qkv)
    n_tiles = n_blk // nq
    last = n_tiles - 1
    qw = nq * blk
    hw = heads * HEAD_DIM
    k_col0 = (N_HEADS * HEAD_DIM) // hw
    once = pl.Buffered(1)
    per_head = lambda shape: [pltpu.VMEM(shape, F32) for _ in range(heads)]
    return pl.pallas_call(
        functools.partial(_attn_kernel, group=group, n_blk=n_blk, heads=heads, nq=nq),
        grid=(N_HEADS // heads, n_tiles + 1),
        in_specs=[pl.BlockSpec((heads, 1, blk), lambda h, i: (h, 0, 0)),
                  pl.BlockSpec((heads, HEAD_DIM, qw), lambda h, i: (h, 0, jnp.minimum(i, last))),
                  pl.BlockSpec((s, hw), lambda h, i: (0, k_col0 + h)),
                  pl.BlockSpec((heads, n_chunk + 1, vt_rows, group * blk), lambda h, i: (h, 0, 0, 0),
                               pipeline_mode=once),
                  pl.BlockSpec((heads, n_blk, qw), lambda h, i: (h, 0, jnp.minimum(i, last))),
                  pl.BlockSpec((heads, n_blk, qw), lambda h, i: (h, 0, jnp.maximum(i - 1, 0)))],
        out_specs=pl.BlockSpec((qw, hw), lambda h, i: (jnp.maximum(i - 1, 0), h)),
        out_shape=jax.ShapeDtypeStruct((s, N_HEADS * HEAD_DIM), BF16),
        scratch_shapes=(per_head((n_chunk, group * blk, qw))
                        + per_head((1, qw))
                        + per_head((2, blk, blk))
                        + per_head((vt_rows, qw))
                        + [pltpu.VMEM((group * blk, qw), BF16) for _ in range(heads)]),
        compiler_params=_params(("parallel", "arbitrary")),
        name="moba_attention",
    )(slopes, qt, qkv, vt, rb, rb)


def _lru_kernel(xr_ref, yr_ref, cw_ref, cb_ref, wa_ref, ba_ref, wx_ref, bx_ref, lam_ref, o_ref,
                xbuf_ref, a_ref, b_ref, h_ref, wab_ref, wxb_ref, *, ts, tc):
    t = pl.program_id(1)
    pad = 8

    @pl.when(t == 0)
    def _():
        xbuf_ref[0:pad, :] = jnp.zeros((pad, tc), F32)
        h_ref[...] = jnp.zeros_like(h_ref)
        wab_ref[...] = wa_ref[...].astype(BF16)
        wxb_ref[...] = wx_ref[...].astype(BF16)

    xbuf_ref[pad:pad + ts, :] = xr_ref[...]
    cw = cw_ref[...]
    u = cb_ref[...]
    for tap in range(CONV_WIDTH):
        off = pad - (CONV_WIDTH - 1) + tap
        u = u + xbuf_ref[off:off + ts, :] * cw[tap:tap + 1, :]
    xbuf_ref[0:pad, :] = xbuf_ref[ts:ts + pad, :]

    ub = u.astype(BF16)
    ga, gx = [], []
    for n in range(tc // LRU_BLOCK_W):
        un = ub[:, n * LRU_BLOCK_W:(n + 1) * LRU_BLOCK_W]
        ga.append(jnp.dot(un, wab_ref[n], preferred_element_type=F32))
        gx.append(jnp.dot(un, wxb_ref[n], preferred_element_type=F32))
    r = jax.nn.sigmoid(jnp.concatenate(ga, axis=1) + ba_ref[...])
    ig = jax.nn.sigmoid(jnp.concatenate(gx, axis=1) + bx_ref[...])
    log_a = -LRU_C * r * jax.nn.softplus(-lam_ref[...])
    a = jnp.exp(log_a)
    a_ref[...] = a
    one_minus_a2 = -jnp.tanh(log_a) * (a * a + 1.0)
    mult = jnp.where(one_minus_a2 == 0.0, 0.0, one_minus_a2 * lax.rsqrt(one_minus_a2))
    b_ref[...] = mult * ig * u

    row = lax.broadcasted_iota(jnp.int32, (8, tc), 0)

    def group(g, hprev):
        r0 = pl.multiple_of(g * 8, 8)
        av = a_ref[pl.ds(r0, 8), :]
        bv = b_ref[pl.ds(r0, 8), :]
        for d in (1, 2, 4):
            keep = row >= d
            a_sh = pltpu.roll(av, d, 0)
            b_sh = pltpu.roll(bv, d, 0)
            bv = jnp.where(keep, av * b_sh + bv, bv)
            av = jnp.where(keep, av * a_sh, av)
        hv = av * hprev + bv
        b_ref[pl.ds(r0, 8), :] = hv
        return jnp.broadcast_to(hv[7:8, :], (8, tc))

    h_ref[...] = lax.fori_loop(0, ts // 8, group, h_ref[...])
    o_ref[...] = (b_ref[...] * jax.nn.gelu(yr_ref[...])).astype(o_ref.dtype)


def _rglru(rest, conv_w, conv_b, w_rg_a, b_rg_a, w_rg_x, b_rg_x, lru_lambda, width, ts=512, tc=512):
    s = rest.shape[0]
    nct = width // tc
    nb = tc // LRU_BLOCK_W
    vec = lambda v: v.reshape(1, width)
    vspec = pl.BlockSpec((1, tc), lambda c, t: (0, c))
    wspec = pl.BlockSpec((nb, LRU_BLOCK_W, LRU_BLOCK_W), lambda c, t: (c, 0, 0))
    return pl.pallas_call(
        functools.partial(_lru_kernel, ts=ts, tc=tc),
        grid=(nct, s // ts),
        in_specs=[pl.BlockSpec((ts, tc), lambda c, t: (t, c)),
                  pl.BlockSpec((ts, tc), lambda c, t: (t, nct + c)),
                  pl.BlockSpec((CONV_WIDTH, tc), lambda c, t: (0, c)),
                  vspec, wspec, vspec, wspec, vspec, vspec],
        out_specs=pl.BlockSpec((ts, tc), lambda c, t: (t, c)),
        out_shape=jax.ShapeDtypeStruct((s, width), BF16),
        scratch_shapes=[pltpu.VMEM((ts + 8, tc), F32),
                        pltpu.VMEM((ts, tc), F32),
                        pltpu.VMEM((ts, tc), F32),
                        pltpu.VMEM((8, tc), F32),
                        pltpu.VMEM((nb, LRU_BLOCK_W, LRU_BLOCK_W), BF16),
                        pltpu.VMEM((nb, LRU_BLOCK_W, LRU_BLOCK_W), BF16)],
        compiler_params=_params(("parallel", "arbitrary")),
        name="rglru",
    )(rest, rest, conv_w, vec(conv_b), w_rg_a, vec(b_rg_a), w_rg_x, vec(b_rg_x), vec(lru_lambda))


def _merge_kernel(att_ref, lru_ref, wa_ref, wl_ref, ga_ref, gl_ref, o_ref, wab_ref, wlb_ref):
    _load_weight(wa_ref, wab_ref)
    _load_weight(wl_ref, wlb_ref)
    bm = o_ref.shape[0]
    sub = bm // 2
    for r in range(2):
        rows = slice(r * sub, (r + 1) * sub)
        pa = jnp.dot(att_ref[rows, :], wab_ref[...], preferred_element_type=F32)
        plru = jnp.dot(lru_ref[rows, :], wlb_ref[...], preferred_element_type=F32)
        o_ref[rows, :] = (jax.nn.sigmoid(ga_ref[rows, :]) * pa
                          + jax.nn.sigmoid(gl_ref[rows, :]) * plru).astype(o_ref.dtype)


def _merge(att, lru, w_att, w_lru, rest, gate_col0, bm=512, bn=1024):
    m, k = att.shape
    n = w_att.shape[1]
    ga0 = gate_col0 // bn
    gl0 = (gate_col0 + n) // bn
    once = pl.Buffered(1)
    return pl.pallas_call(
        _merge_kernel,
        grid=(n // bn, m // bm),
        in_specs=[pl.BlockSpec((bm, k), lambda j, i: (i, 0)),
                  pl.BlockSpec((bm, k), lambda j, i: (i, 0)),
                  pl.BlockSpec((k, bn), lambda j, i: (0, j), pipeline_mode=once),
                  pl.BlockSpec((k, bn), lambda j, i: (0, j), pipeline_mode=once),
                  pl.BlockSpec((bm, bn), lambda j, i: (i, ga0 + j)),
                  pl.BlockSpec((bm, bn), lambda j, i: (i, gl0 + j))],
        out_specs=pl.BlockSpec((bm, bn), lambda j, i: (i, j)),
        out_shape=jax.ShapeDtypeStruct((m, n), BF16),
        scratch_shapes=[pltpu.VMEM((k, bn), BF16), pltpu.VMEM((k, bn), BF16)],
        compiler_params=_params(("parallel", "arbitrary")),
        name="merge",
    )(att, lru, w_att, w_lru, rest, rest)


def _outproj_kernel(a_ref, w_ref, x_ref, nw_ref, x1_ref, h2_ref, wb_ref):
    _load_weight(w_ref, wb_ref, row_axis=0)
    x1 = x_ref[...] + jnp.dot(a_ref[...], wb_ref[...], preferred_element_type=F32)
    x1_ref[...] = x1
    y = x1 * lax.rsqrt(jnp.mean(x1 * x1, axis=-1, keepdims=True) + EPS)
    h2_ref[...] = (y * nw_ref[...]).astype(h2_ref.dtype)


def _outproj(merged, w_out, x, norm2_w, bm=512):
    m, k = merged.shape
    d = w_out.shape[1]
    return pl.pallas_call(
        _outproj_kernel,
        grid=(m // bm,),
        in_specs=[pl.BlockSpec((bm, k), lambda i: (i, 0)),
                  pl.BlockSpec((k, d), lambda i: (0, 0), pipeline_mode=pl.Buffered(1)),
                  pl.BlockSpec((bm, d), lambda i: (i, 0)),
                  pl.BlockSpec((1, d), lambda i: (0, 0))],
        out_specs=[pl.BlockSpec((bm, d), lambda i: (i, 0)),
                   pl.BlockSpec((bm, d), lambda i: (i, 0))],
        out_shape=[jax.ShapeDtypeStruct((m, d), F32), jax.ShapeDtypeStruct((m, d), BF16)],
        scratch_shapes=[pltpu.VMEM((k, d), BF16)],
        compiler_params=_params(("arbitrary",)),
        name="outproj",
    )(merged, w_out, x, norm2_w.reshape(1, d))


def _ffn_up_kernel(h_ref, wg_ref, wu_ref, o_ref, wgb_ref, wub_ref):
    _load_weight(wg_ref, wgb_ref)
    _load_weight(wu_ref, wub_ref)
    splits = 4
    sub = o_ref.shape[0] // splits
    for r in range(splits):
        rows = slice(r * sub, (r + 1) * sub)
        h = h_ref[rows, :]
        g = jnp.dot(h, wgb_ref[...], preferred_element_type=F32)
        u = jnp.dot(h, wub_ref[...], preferred_element_type=F32)
        o_ref[rows, :] = (jax.nn.silu(g) * u).astype(o_ref.dtype)


def _ffn_up(h2, w_gate, w_up, bm=2048, bn=512):
    m, k = h2.shape
    n = w_gate.shape[1]
    return pl.pallas_call(
        _ffn_up_kernel,
        grid=(n // bn, m // bm),
        in_specs=[pl.BlockSpec((bm, k), lambda j, i: (i, 0)),
                  pl.BlockSpec((k, bn), lambda j, i: (0, j)),
                  pl.BlockSpec((k, bn), lambda j, i: (0, j))],
        out_specs=pl.BlockSpec((bm, bn), lambda j, i: (i, j)),
        out_shape=jax.ShapeDtypeStruct((m, n), BF16),
        scratch_shapes=[pltpu.VMEM((k, bn), BF16), pltpu.VMEM((k, bn), BF16)],
        compiler_params=_params(("parallel", "arbitrary")),
        name="ffn_up",
    )(h2, w_gate, w_up)


def _ffn_down_kernel(a_ref, w_ref, x_ref, o_ref, wb_ref):
    _load_weight(w_ref, wb_ref)
    o_ref[...] = x_ref[...] + jnp.dot(a_ref[...], wb_ref[...], preferred_element_type=F32)


def _ffn_down(act, w_down, x1, bm=512, bn=512):
    m, k = act.shape
    n = w_down.shape[1]
    return pl.pallas_call(
        _ffn_down_kernel,
        grid=(n // bn, m // bm),
        in_specs=[pl.BlockSpec((bm, k), lambda j, i: (i, 0)),
                  pl.BlockSpec((k, bn), lambda j, i: (0, j)),
                  pl.BlockSpec((bm, bn), lambda j, i: (i, j))],
        out_specs=pl.BlockSpec((bm, bn), lambda j, i: (i, j)),
        out_shape=jax.ShapeDtypeStruct((m, n), F32),
        scratch_shapes=[pltpu.VMEM((k, bn), BF16)],
        compiler_params=_params(("parallel", "arbitrary")),
        name="ffn_down",
    )(act, w_down, x1)


def _layer(x, norm1_w, w_in, q_norm_w, k_norm_w, conv_w, conv_b, w_rg_a, b_rg_a, w_rg_x, b_rg_x,
           lru_lambda, w_proj_attn, w_proj_lru, w_out, norm2_w, w_ffn_gate, w_ffn_up, w_ffn_down):
    d = x.shape[1]
    att_w = N_HEADS * HEAD_DIM
    lru_w = w_proj_lru.shape[0]

    h = _rmsnorm(x, norm1_w)
    qkv = _qkv_proj(h, w_in, q_norm_w, k_norm_w, 3 * att_w)
    rest = _rest_proj(h, w_in, 3 * att_w, 2 * lru_w + 2 * d)

    head = jnp.arange(1, N_HEADS + 1, dtype=F32)
    slopes = jnp.broadcast_to(jnp.exp2(-8.0 * head / N_HEADS)[:, None, None], (N_HEADS, 1, MOBA_BLOCK))
    att = _moba_attention(qkv, slopes)

    lru = _rglru(rest, conv_w, conv_b, w_rg_a, b_rg_a, w_rg_x, b_rg_x, lru_lambda, lru_w)

    merged = _merge(att, lru, w_proj_attn, w_proj_lru, rest, 2 * lru_w)
    x1, h2 = _outproj(merged, w_out, x, norm2_w)
    act = _ffn_up(h2, w_ffn_gate, w_ffn_up)
    return _ffn_down(act, w_ffn_down, x1)


def kernel(x, norm1_w, w_in, q_norm_w, k_norm_w, conv_w, conv_b, w_rg_a, b_rg_a, w_rg_x, b_rg_x,
           lru_lambda, w_proj_attn, w_proj_lru, w_out, norm2_w, w_ffn_gate, w_ffn_up, w_ffn_down):
    b, s, d = x.shape
    assert b == 1, "kernel handles the batch-1 prefill shape"
    y = x.reshape(s, d)
    for layer in range(norm1_w.shape[0]):
        y = _layer(y, norm1_w[layer], w_in[layer], q_norm_w[layer], k_norm_w[layer], conv_w[layer],
                   conv_b[layer], w_rg_a[layer], b_rg_a[layer], w_rg_x[layer], b_rg_x[layer],
                   lru_lambda[layer], w_proj_attn[layer], w_proj_lru[layer], w_out[layer],
                   norm2_w[layer], w_ffn_gate[layer], w_ffn_up[layer], w_ffn_down[layer])
    return y.reshape(b, s, d)
```

```python
import functools

import jax
import jax.numpy as jnp
from jax import lax
from jax.experimental import pallas as pl
from jax.experimental.pallas import tpu as pltpu

F32 = jnp.float32
BF16 = jnp.bfloat16

N_HEADS = 16
HEAD_DIM = 128
MOBA_BLOCK = 256
MOBA_TOPK = 3
LRU_BLOCK_W = 128
CONV_WIDTH = 4
LRU_C = 8.0
EPS = 1e-6
NEG_INF = -1e30
LOG2E = 1.4426950408889634
ONES_ROWS = 16

V7X_VMEM_BYTES = 64 * 1024 * 1024
VMEM_LIMIT = 56 * 1024 * 1024


def _params(semantics):
    return pltpu.CompilerParams(dimension_semantics=semantics, vmem_limit_bytes=VMEM_LIMIT)


def _rmsnorm_kernel(x_ref, w_ref, o_ref):
    x = x_ref[...]
    y = x * lax.rsqrt(jnp.mean(x * x, axis=-1, keepdims=True) + EPS)
    o_ref[...] = (y * w_ref[...]).astype(o_ref.dtype)


def _rmsnorm(x, w, tm=512):
    m, d = x.shape
    return pl.pallas_call(
        _rmsnorm_kernel,
        grid=(m // tm,),
        in_specs=[pl.BlockSpec((tm, d), lambda i: (i, 0)),
                  pl.BlockSpec((1, d), lambda i: (0, 0))],
        out_specs=pl.BlockSpec((tm, d), lambda i: (i, 0)),
        out_shape=jax.ShapeDtypeStruct((m, d), BF16),
        compiler_params=_params(("parallel",)),
        name="rmsnorm1",
    )(x, w.reshape(1, d))


def _load_weight(w_ref, wb_ref, row_axis=1):
    @pl.when(pl.program_id(row_axis) == 0)
    def _():
        wb_ref[...] = w_ref[...].astype(BF16)


def _qkv_kernel(h_ref, w_ref, qw_ref, kw_ref, o_ref, wb_ref, *, qk_tiles, row_splits):
    j = pl.program_id(0)
    _load_weight(w_ref, wb_ref)
    bm, bn = o_ref.shape
    sub = bm // row_splits

    @pl.when(j < qk_tiles)
    def _():
        nw = jnp.where(j < qk_tiles // 2, qw_ref[...], kw_ref[...])
        for r in range(row_splits):
            rows = slice(r * sub, (r + 1) * sub)
            acc = jnp.dot(h_ref[rows, :], wb_ref[...], preferred_element_type=F32)
            for hh in range(bn // HEAD_DIM):
                a = acc[:, hh * HEAD_DIM:(hh + 1) * HEAD_DIM]
                y = a * lax.rsqrt(jnp.mean(a * a, axis=-1, keepdims=True) + EPS)
                o_ref[rows, hh * HEAD_DIM:(hh + 1) * HEAD_DIM] = (y * nw).astype(o_ref.dtype)

    @pl.when(j >= qk_tiles)
    def _():
        o_ref[...] = jnp.dot(h_ref[...], wb_ref[...], preferred_element_type=F32).astype(o_ref.dtype)


def _qkv_proj(h, w_in, q_norm_w, k_norm_w, n_cols, bm=1024, bn=1024):
    m, k = h.shape
    return pl.pallas_call(
        functools.partial(_qkv_kernel, qk_tiles=(2 * N_HEADS * HEAD_DIM) // bn, row_splits=4),
        grid=(n_cols // bn, m // bm),
        in_specs=[pl.BlockSpec((bm, k), lambda j, i: (i, 0)),
                  pl.BlockSpec((k, bn), lambda j, i: (0, j)),
                  pl.BlockSpec((1, HEAD_DIM), lambda j, i: (0, 0)),
                  pl.BlockSpec((1, HEAD_DIM), lambda j, i: (0, 0))],
        out_specs=pl.BlockSpec((bm, bn), lambda j, i: (i, j)),
        out_shape=jax.ShapeDtypeStruct((m, n_cols), BF16),
        scratch_shapes=[pltpu.VMEM((k, bn), BF16)],
        compiler_params=_params(("parallel", "arbitrary")),
        name="qkv_proj",
    )(h, w_in, q_norm_w.reshape(1, HEAD_DIM), k_norm_w.reshape(1, HEAD_DIM))


def _matmul_kernel(a_ref, w_ref, o_ref, wb_ref):
    _load_weight(w_ref, wb_ref)
    o_ref[...] = jnp.dot(a_ref[...], wb_ref[...], preferred_element_type=F32).astype(o_ref.dtype)


def _rest_proj(h, w_in, col0, n_cols, bm=1024, bn=1024):
    m, k = h.shape
    jb = col0 // bn
    return pl.pallas_call(
        _matmul_kernel,
        grid=(n_cols // bn, m // bm),
        in_specs=[pl.BlockSpec((bm, k), lambda j, i: (i, 0)),
                  pl.BlockSpec((k, bn), lambda j, i: (0, jb + j))],
        out_specs=pl.BlockSpec((bm, bn), lambda j, i: (i, j)),
        out_shape=jax.ShapeDtypeStruct((m, n_cols), F32),
        scratch_shapes=[pltpu.VMEM((k, bn), BF16)],
        compiler_params=_params(("parallel", "arbitrary")),
        name="rest_proj",
    )(h, w_in)


def _attn_prep_kernel(slope_ref, q_ref, k_ref, v_ref, qt_ref, vt_ref, rb_ref, kmean_ref,
                      *, n_blk, group, cols):
    blk = MOBA_BLOCK
    for jb in range(n_blk):
        rows = slice(jb * blk, (jb + 1) * blk)
        qt_ref[:, rows] = q_ref[rows, :].astype(F32).T.astype(BF16)
        c, g = divmod(jb, group)
        vt_ref[c, 0:HEAD_DIM, g * blk:(g + 1) * blk] = v_ref[rows, :].astype(F32).T.astype(BF16)
        vt_ref[c, HEAD_DIM:, g * blk:(g + 1) * blk] = jnp.ones((ONES_ROWS, blk), BF16)
        kmean_ref[jb:jb + 1, :] = jnp.sum(k_ref[rows, :].astype(F32), axis=0, keepdims=True) * (1.0 / blk)
    vt_ref[n_blk // group] = jnp.zeros(vt_ref.shape[1:], BF16)
    kmean = kmean_ref[...].astype(BF16)
    slope = slope_ref[:, 0:1]
    row = lax.broadcasted_iota(jnp.int32, (n_blk, cols), 0)
    col = lax.broadcasted_iota(jnp.int32, (n_blk, cols), 1)
    for qc in range(qt_ref.shape[1] // cols):
        csl = slice(qc * cols, (qc + 1) * cols)
        gate = jnp.dot(kmean, qt_ref[:, csl], preferred_element_type=F32)
        qblk = lax.shift_right_logical(col + qc * cols, blk.bit_length() - 1)
        g = jnp.where(row < qblk, gate, NEG_INF)
        bias = jnp.full((n_blk, cols), NEG_INF, F32)
        for r in range(MOBA_TOPK):
            mx = jnp.max(g, axis=0, keepdims=True)
            idx = jnp.min(jnp.where(g == mx, row, n_blk), axis=0, keepdims=True)
            pick = row == idx
            bias = jnp.where(pick, jnp.where(qblk > r, 0.0, NEG_INF), bias)
            g = jnp.where(pick, -jnp.inf, g)
        past = bias - slope * ((qblk - row) * blk).astype(F32)
        rb = jnp.where(row == qblk, 0.0, jnp.where(row < qblk, past, NEG_INF))
        rb_ref[:, csl] = rb * LOG2E


def _attn_kernel(slope_ref, qt_ref, k_ref, vt_ref, rb_ref, rbp_ref, o_ref, *scratch,
                 group, n_blk, heads, nq):
    u_refs, m_refs, dmat_refs, acc_refs, p_refs = (scratch[n * heads:(n + 1) * heads] for n in range(5))
    blk = MOBA_BLOCK
    cb = group * blk
    lg = group.bit_length() - 1
    i = pl.program_id(1)

    @pl.when(i == 0)
    def _per_head_setup():
        kk = lax.broadcasted_iota(jnp.int32, (blk, blk), 0)
        qq = lax.broadcasted_iota(jnp.int32, (blk, blk), 1)
        for hh in range(heads):
            d = slope_ref[hh][:, 0:1] * (qq - kk).astype(F32) * LOG2E
            dmat_refs[hh][0] = d
            dmat_refs[hh][1] = jnp.where(kk <= qq, d, -NEG_INF)
            m_refs[hh][...] = jnp.zeros_like(m_refs[hh])
            p_refs[hh][...] = jnp.zeros_like(p_refs[hh])

    k2 = (HEAD_DIM ** -0.5) * LOG2E
    n1 = jnp.where(i < n_blk // nq, lax.shift_right_logical(nq * (i + 1) + group - 1, lg), 0)
    n2 = lax.shift_right_logical(nq * i + group - 1, lg)
    common = jnp.minimum(n1, n2)
    m_prev = [m_refs[hh][...] for hh in range(heads)]
    for hh in range(heads):
        acc_refs[hh][...] = jnp.zeros_like(acc_refs[hh])

    def pass1(hh, c, mx):
        r0 = pl.multiple_of(c * cb, cb)
        k_chunk = k_ref[pl.ds(r0, cb), hh * HEAD_DIM:(hh + 1) * HEAD_DIM]
        s = jnp.dot(k_chunk, qt_ref[hh], preferred_element_type=F32)
        for g in range(group):
            j = c * group + g
            rows = slice(g * blk, (g + 1) * blk)
            cands = []
            for qb in range(nq):
                cols = slice(qb * blk, (qb + 1) * blk)
                own = (j == nq * i + qb).astype(jnp.int32)
                u = s[rows, cols] * k2 - dmat_refs[hh][own]
                u_refs[hh][c, rows, cols] = u
                cands.append(jnp.max(u, axis=0, keepdims=True))
            mx = jnp.maximum(mx, jnp.concatenate(cands, axis=1) + rb_ref[hh, pl.ds(j, 1), :])
        return mx

    def probs(hh, c):
        for g in range(group):
            j = c * group + g
            rows = slice(g * blk, (g + 1) * blk)
            p = jnp.exp2(u_refs[hh][c, rows, :] + (rbp_ref[hh, pl.ds(j, 1), :] - m_prev[hh]))
            p_refs[hh][rows, :] = p.astype(BF16)

    def pv(hh, c):
        n_chunk = vt_ref.shape[1] - 1
        acc_refs[hh][...] += jnp.dot(vt_ref[hh, jnp.where(c == 0, n_chunk, c - 1)], p_refs[hh][...],
                                     preferred_element_type=F32)

    def only1(c, mxs):
        return tuple(pass1(hh, c, mxs[hh]) for hh in range(heads))

    def only2(c, carry):
        for hh in range(heads):
            pv(hh, c)
        for hh in range(heads):
            probs(hh, c)
        return carry

    def both(c, mxs):
        for hh in range(heads):
            pv(hh, c)
        out = []
        for hh in range(heads):
            probs(hh, c)
            out.append(pass1(hh, c, mxs[hh]))
        return tuple(out)

    def both_twice(c2, mxs):
        return both(2 * c2 + 1, both(2 * c2, mxs))

    pairs = lax.shift_right_logical(common, 1)
    mxs = lax.fori_loop(0, pairs, both_twice,
                        tuple(jnp.full((1, nq * blk), -jnp.inf, F32) for _ in range(heads)))
    mxs = lax.fori_loop(2 * pairs, common, both, mxs)
    mxs = lax.fori_loop(common, n1, only1, mxs)
    lax.fori_loop(common, n2, only2, 0)
    for hh in range(heads):
        pv(hh, n2)
        m_refs[hh][...] = mxs[hh]

    @pl.when(i > 0)
    def _():
        for hh in range(heads):
            acc = acc_refs[hh][...]
            o = acc[0:HEAD_DIM, :] / acc[HEAD_DIM:HEAD_DIM + 1, :]
            for qb in range(nq):
                o_ref[qb * blk:(qb + 1) * blk, hh * HEAD_DIM:(hh + 1) * HEAD_DIM] = (
                    o[:, qb * blk:(qb + 1) * blk].T.astype(o_ref.dtype))


def _moba_attention(qkv, slopes, group=4, cols=2048, heads=2, nq=2):
    s = qkv.shape[0]
    blk = MOBA_BLOCK
    n_blk = s // blk
    n_chunk = n_blk // group
    vt_rows = HEAD_DIM + ONES_ROWS
    qt, vt, rb = pl.pallas_call(
        functools.partial(_attn_prep_kernel, n_blk=n_blk, group=group, cols=cols),
        grid=(N_HEADS,),
        in_specs=[pl.BlockSpec((None, 1, blk), lambda h: (h, 0, 0)),
                  pl.BlockSpec((s, HEAD_DIM), lambda h: (0, h)),
                  pl.BlockSpec((s, HEAD_DIM), lambda h: (0, N_HEADS + h)),
                  pl.BlockSpec((s, HEAD_DIM), lambda h: (0, 2 * N_HEADS + h))],
        out_specs=[pl.BlockSpec((None, HEAD_DIM, s), lambda h: (h, 0, 0)),
                   pl.BlockSpec((None, n_chunk + 1, vt_rows, group * blk), lambda h: (h, 0, 0, 0)),
                   pl.BlockSpec((None, n_blk, s), lambda h: (h, 0, 0))],
        out_shape=[jax.ShapeDtypeStruct((N_HEADS, HEAD_DIM, s), BF16),
                   jax.ShapeDtypeStruct((N_HEADS, n_chunk + 1, vt_rows, group * blk), BF16),
                   jax.ShapeDtypeStruct((N_HEADS, n_blk, s), F32)],
        scratch_shapes=[pltpu.VMEM((n_blk, HEAD_DIM), F32)],
        compiler_params=_params(("parallel",)),
        name="moba_prep",
    )(slopes, qkv, qkv, qkv)
    n_tiles = n_blk // nq
    last = n_tiles - 1
    qw = nq * blk
    hw = heads * HEAD_DIM
    k_col0 = (N_HEADS * HEAD_DIM) // hw
    once = pl.Buffered(1)
    per_head = lambda shape: [pltpu.VMEM(shape, F32) for _ in range(heads)]
    return pl.pallas_call(
        functools.partial(_attn_kernel, group=group, n_blk=n_blk, heads=heads, nq=nq),
        grid=(N_HEADS // heads, n_tiles + 1),
        in_specs=[pl.BlockSpec((heads, 1, blk), lambda h, i: (h, 0, 0)),
                  pl.BlockSpec((heads, HEAD_DIM, qw), lambda h, i: (h, 0, jnp.minimum(i, last))),
                  pl.BlockSpec((s, hw), lambda h, i: (0, k_col0 + h)),
                  pl.BlockSpec((heads, n_chunk + 1, vt_rows, group * blk), lambda h, i: (h, 0, 0, 0),
                               pipeline_mode=once),
                  pl.BlockSpec((heads, n_blk, qw), lambda h, i: (h, 0, jnp.minimum(i, last))),
                  pl.BlockSpec((heads, n_blk, qw), lambda h, i: (h, 0, jnp.maximum(i - 1, 0)))],
        out_specs=pl.BlockSpec((qw, hw), lambda h, i: (jnp.maximum(i - 1, 0), h)),
        out_shape=jax.ShapeDtypeStruct((s, N_HEADS * HEAD_DIM), BF16),
        scratch_shapes=(per_head((n_chunk, group * blk, qw))
                        + per_head((1, qw))
                        + per_head((2, blk, blk))
                        + per_head((vt_rows, qw))
                        + [pltpu.VMEM((group * blk, qw), BF16) for _ in range(heads)]),
        compiler_params=_params(("parallel", "arbitrary")),
        name="moba_attention",
    )(slopes, qt, qkv, vt, rb, rb)


def _lru_kernel(xr_ref, yr_ref, cw_ref, cb_ref, wa_ref, ba_ref, wx_ref, bx_ref, lam_ref, o_ref,
                xbuf_ref, a_ref, b_ref, h_ref, wab_ref, wxb_ref, *, ts, tc):
    t = pl.program_id(1)
    pad = 8

    @pl.when(t == 0)
    def _():
        xbuf_ref[0:pad, :] = jnp.zeros((pad, tc), F32)
        h_ref[...] = jnp.zeros_like(h_ref)
        wab_ref[...] = wa_ref[...].astype(BF16)
        wxb_ref[...] = wx_ref[...].astype(BF16)

    xbuf_ref[pad:pad + ts, :] = xr_ref[...]
    cw = cw_ref[...]
    u = cb_ref[...]
    for tap in range(CONV_WIDTH):
        off = pad - (CONV_WIDTH - 1) + tap
        u = u + xbuf_ref[off:off + ts, :] * cw[tap:tap + 1, :]
    xbuf_ref[0:pad, :] = xbuf_ref[ts:ts + pad, :]

    ub = u.astype(BF16)
    ga, gx = [], []
    for n in range(tc // LRU_BLOCK_W):
        un = ub[:, n * LRU_BLOCK_W:(n + 1) * LRU_BLOCK_W]
        ga.append(jnp.dot(un, wab_ref[n], preferred_element_type=F32))
        gx.append(jnp.dot(un, wxb_ref[n], preferred_element_type=F32))
    r = jax.nn.sigmoid(jnp.concatenate(ga, axis=1) + ba_ref[...])
    ig = jax.nn.sigmoid(jnp.concatenate(gx, axis=1) + bx_ref[...])
    log_a = -LRU_C * r * jax.nn.softplus(-lam_ref[...])
    a = jnp.exp(log_a)
    a_ref[...] = a
    one_minus_a2 = -jnp.tanh(log_a) * (a * a + 1.0)
    mult = jnp.where(one_minus_a2 == 0.0, 0.0, one_minus_a2 * lax.rsqrt(one_minus_a2))
    b_ref[...] = mult * ig * u

    row = lax.broadcasted_iota(jnp.int32, (8, tc), 0)

    def group(g, hprev):
        r0 = pl.multiple_of(g * 8, 8)
        av = a_ref[pl.ds(r0, 8), :]
        bv = b_ref[pl.ds(r0, 8), :]
        for d in (1, 2, 4):
            keep = row >= d
            a_sh = pltpu.roll(av, d, 0)
            b_sh = pltpu.roll(bv, d, 0)
            bv = jnp.where(keep, av * b_sh + bv, bv)
            av = jnp.where(keep, av * a_sh, av)
        hv = av * hprev + bv
        b_ref[pl.ds(r0, 8), :] = hv
        return jnp.broadcast_to(hv[7:8, :], (8, tc))

    h_ref[...] = lax.fori_loop(0, ts // 8, group, h_ref[...])
    o_ref[...] = (b_ref[...] * jax.nn.gelu(yr_ref[...])).astype(o_ref.dtype)


def _rglru(rest, conv_w, conv_b, w_rg_a, b_rg_a, w_rg_x, b_rg_x, lru_lambda, width, ts=512, tc=512):
    s = rest.shape[0]
    nct = width // tc
    nb = tc // LRU_BLOCK_W
    vec = lambda v: v.reshape(1, width)
    vspec = pl.BlockSpec((1, tc), lambda c, t: (0, c))
    wspec = pl.BlockSpec((nb, LRU_BLOCK_W, LRU_BLOCK_W), lambda c, t: (c, 0, 0))
    return pl.pallas_call(
        functools.partial(_lru_kernel, ts=ts, tc=tc),
        grid=(nct, s // ts),
        in_specs=[pl.BlockSpec((ts, tc), lambda c, t: (t, c)),
                  pl.BlockSpec((ts, tc), lambda c, t: (t, nct + c)),
                  pl.BlockSpec((CONV_WIDTH, tc), lambda c, t: (0, c)),
                  vspec, wspec, vspec, wspec, vspec, vspec],
        out_specs=pl.BlockSpec((ts, tc), lambda c, t: (t, c)),
        out_shape=jax.ShapeDtypeStruct((s, width), BF16),
        scratch_shapes=[pltpu.VMEM((ts + 8, tc), F32),
                        pltpu.VMEM((ts, tc), F32),
                        pltpu.VMEM((ts, tc), F32),
                        pltpu.VMEM((8, tc), F32),
                        pltpu.VMEM((nb, LRU_BLOCK_W, LRU_BLOCK_W), BF16),
                        pltpu.VMEM((nb, LRU_BLOCK_W, LRU_BLOCK_W), BF16)],
        compiler_params=_params(("parallel", "arbitrary")),
        name="rglru",
    )(rest, rest, conv_w, vec(conv_b), w_rg_a, vec(b_rg_a), w_rg_x, vec(b_rg_x), vec(lru_lambda))


def _merge_kernel(att_ref, lru_ref, wa_ref, wl_ref, ga_ref, gl_ref, o_ref, wab_ref, wlb_ref):
    _load_weight(wa_ref, wab_ref)
    _load_weight(wl_ref, wlb_ref)
    bm = o_ref.shape[0]
    sub = bm // 2
    for r in range(2):
        rows = slice(r * sub, (r + 1) * sub)
        pa = jnp.dot(att_ref[rows, :], wab_ref[...], preferred_element_type=F32)
        plru = jnp.dot(lru_ref[rows, :], wlb_ref[...], preferred_element_type=F32)
        o_ref[rows, :] = (jax.nn.sigmoid(ga_ref[rows, :]) * pa
                          + jax.nn.sigmoid(gl_ref[rows, :]) * plru).astype(o_ref.dtype)


def _merge(att, lru, w_att, w_lru, rest, gate_col0, bm=512, bn=1024):
    m, k = att.shape
    n = w_att.shape[1]
    ga0 = gate_col0 // bn
    gl0 = (gate_col0 + n) // bn
    once = pl.Buffered(1)
    return pl.pallas_call(
        _merge_kernel,
        grid=(n // bn, m // bm),
        in_specs=[pl.BlockSpec((bm, k), lambda j, i: (i, 0)),
                  pl.BlockSpec((bm, k), lambda j, i: (i, 0)),
                  pl.BlockSpec((k, bn), lambda j, i: (0, j), pipeline_mode=once),
                  pl.BlockSpec((k, bn), lambda j, i: (0, j), pipeline_mode=once),
                  pl.BlockSpec((bm, bn), lambda j, i: (i, ga0 + j)),
                  pl.BlockSpec((bm, bn), lambda j, i: (i, gl0 + j))],
        out_specs=pl.BlockSpec((bm, bn), lambda j, i: (i, j)),
        out_shape=jax.ShapeDtypeStruct((m, n), BF16),
        scratch_shapes=[pltpu.VMEM((k, bn), BF16), pltpu.VMEM((k, bn), BF16)],
        compiler_params=_params(("parallel", "arbitrary")),
        name="merge",
    )(att, lru, w_att, w_lru, rest, rest)


def _outproj_kernel(a_ref, w_ref, x_ref, nw_ref, x1_ref, h2_ref, wb_ref):
    _load_weight(w_ref, wb_ref, row_axis=0)
    x1 = x_ref[...] + jnp.dot(a_ref[...], wb_ref[...], preferred_element_type=F32)
    x1_ref[...] = x1
    y = x1 * lax.rsqrt(jnp.mean(x1 * x1, axis=-1, keepdims=True) + EPS)
    h2_ref[...] = (y * nw_ref[...]).astype(h2_ref.dtype)


def _outproj(merged, w_out, x, norm2_w, bm=512):
    m, k = merged.shape
    d = w_out.shape[1]
    return pl.pallas_call(
        _outproj_kernel,
        grid=(m // bm,),
        in_specs=[pl.BlockSpec((bm, k), lambda i: (i, 0)),
                  pl.BlockSpec((k, d), lambda i: (0, 0), pipeline_mode=pl.Buffered(1)),
                  pl.BlockSpec((bm, d), lambda i: (i, 0)),
                  pl.BlockSpec((1, d), lambda i: (0, 0))],
        out_specs=[pl.BlockSpec((bm, d), lambda i: (i, 0)),
                   pl.BlockSpec((bm, d), lambda i: (i, 0))],
        out_shape=[jax.ShapeDtypeStruct((m, d), F32), jax.ShapeDtypeStruct((m, d), BF16)],
        scratch_shapes=[pltpu.VMEM((k, d), BF16)],
        compiler_params=_params(("arbitrary",)),
        name="outproj",
    )(merged, w_out, x, norm2_w.reshape(1, d))


def _ffn_up_kernel(h_ref, wg_ref, wu_ref, o_ref, wgb_ref, wub_ref):
    _load_weight(wg_ref, wgb_ref)
    _load_weight(wu_ref, wub_ref)
    splits = 4
    sub = o_ref.shape[0] // splits
    for r in range(splits):
        rows = slice(r * sub, (r + 1) * sub)
        h = h_ref[rows, :]
        g = jnp.dot(h, wgb_ref[...], preferred_element_type=F32)
        u = jnp.dot(h, wub_ref[...], preferred_element_type=F32)
        o_ref[rows, :] = (jax.nn.silu(g) * u).astype(o_ref.dtype)


def _ffn_up(h2, w_gate, w_up, bm=2048, bn=512):
    m, k = h2.shape
    n = w_gate.shape[1]
    return pl.pallas_call(
        _ffn_up_kernel,
        grid=(n // bn, m // bm),
        in_specs=[pl.BlockSpec((bm, k), lambda j, i: (i, 0)),
                  pl.BlockSpec((k, bn), lambda j, i: (0, j)),
                  pl.BlockSpec((k, bn), lambda j, i: (0, j))],
        out_specs=pl.BlockSpec((bm, bn), lambda j, i: (i, j)),
        out_shape=jax.ShapeDtypeStruct((m, n), BF16),
        scratch_shapes=[pltpu.VMEM((k, bn), BF16), pltpu.VMEM((k, bn), BF16)],
        compiler_params=_params(("parallel", "arbitrary")),
        name="ffn_up",
    )(h2, w_gate, w_up)


def _ffn_down_kernel(a_ref, w_ref, x_ref, o_ref, wb_ref):
    _load_weight(w_ref, wb_ref)
    o_ref[...] = x_ref[...] + jnp.dot(a_ref[...], wb_ref[...], preferred_element_type=F32)


def _ffn_down(act, w_down, x1, bm=512, bn=512):
    m, k = act.shape
    n = w_down.shape[1]
    return pl.pallas_call(
        _ffn_down_kernel,
        grid=(n // bn, m // bm),
        in_specs=[pl.BlockSpec((bm, k), lambda j, i: (i, 0)),
                  pl.BlockSpec((k, bn), lambda j, i: (0, j)),
                  pl.BlockSpec((bm, bn), lambda j, i: (i, j))],
        out_specs=pl.BlockSpec((bm, bn), lambda j, i: (i, j)),
        out_shape=jax.ShapeDtypeStruct((m, n), F32),
        scratch_shapes=[pltpu.VMEM((k, bn), BF16)],
        compiler_params=_params(("parallel", "arbitrary")),
        name="ffn_down",
    )(act, w_down, x1)


def _layer(x, norm1_w, w_in, q_norm_w, k_norm_w, conv_w, conv_b, w_rg_a, b_rg_a, w_rg_x, b_rg_x,
           lru_lambda, w_proj_attn, w_proj_lru, w_out, norm2_w, w_ffn_gate, w_ffn_up, w_ffn_down):
    d = x.shape[1]
    att_w = N_HEADS * HEAD_DIM
    lru_w = w_proj_lru.shape[0]

    h = _rmsnorm(x, norm1_w)
    qkv = _qkv_proj(h, w_in, q_norm_w, k_norm_w, 3 * att_w)
    rest = _rest_proj(h, w_in, 3 * att_w, 2 * lru_w + 2 * d)

    head = jnp.arange(1, N_HEADS + 1, dtype=F32)
    slopes = jnp.broadcast_to(jnp.exp2(-8.0 * head / N_HEADS)[:, None, None], (N_HEADS, 1, MOBA_BLOCK))
    att = _moba_attention(qkv, slopes)

    lru = _rglru(rest, conv_w, conv_b, w_rg_a, b_rg_a, w_rg_x, b_rg_x, lru_lambda, lru_w)

    merged = _merge(att, lru, w_proj_attn, w_proj_lru, rest, 2 * lru_w)
    x1, h2 = _outproj(merged, w_out, x, norm2_w)
    act = _ffn_up(h2, w_ffn_gate, w_ffn_up)
    return _ffn_down(act, w_ffn_down, x1)


def kernel(x, norm1_w, w_in, q_norm_w, k_norm_w, conv_w, conv_b, w_rg_a, b_rg_a, w_rg_x, b_rg_x,
           lru_lambda, w_proj_attn, w_proj_lru, w_out, norm2_w, w_ffn_gate, w_ffn_up, w_ffn_down):
    b, s, d = x.shape
    assert b == 1, "kernel handles the batch-1 prefill shape"
    y = x.reshape(s, d)
    for layer in range(norm1_w.shape[0]):
        y = _layer(y, norm1_w[layer], w_in[layer], q_norm_w[layer], k_norm_w[layer], conv_w[layer],
                   conv_b[layer], w_rg_a[layer], b_rg_a[layer], w_rg_x[layer], b_rg_x[layer],
                   lru_lambda[layer], w_proj_attn[layer], w_proj_lru[layer], w_out[layer],
                   norm2_w[layer], w_ffn_gate[layer], w_ffn_up[layer], w_ffn_down[layer])
    return y.reshape(b, s, d)
```

```python
import functools

import jax
import jax.numpy as jnp
from jax import lax
from jax.experimental import pallas as pl
from jax.experimental.pallas import tpu as pltpu

F32 = jnp.float32
BF16 = jnp.bfloat16

N_HEADS = 16
HEAD_DIM = 128
MOBA_BLOCK = 256
MOBA_TOPK = 3
LRU_BLOCK_W = 128
CONV_WIDTH = 4
LRU_C = 8.0
EPS = 1e-6
NEG_INF = -1e30
LOG2E = 1.4426950408889634
ONES_ROWS = 16

V7X_VMEM_BYTES = 64 * 1024 * 1024
VMEM_LIMIT = 56 * 1024 * 1024


def _params(semantics):
    return pltpu.CompilerParams(dimension_semantics=semantics, vmem_limit_bytes=VMEM_LIMIT)


def _rmsnorm_kernel(x_ref, w_ref, o_ref):
    x = x_ref[...]
    y = x * lax.rsqrt(jnp.mean(x * x, axis=-1, keepdims=True) + EPS)
    o_ref[...] = (y * w_ref[...]).astype(o_ref.dtype)


def _rmsnorm(x, w, tm=512):
    m, d = x.shape
    return pl.pallas_call(
        _rmsnorm_kernel,
        grid=(m // tm,),
        in_specs=[pl.BlockSpec((tm, d), lambda i: (i, 0)),
                  pl.BlockSpec((1, d), lambda i: (0, 0))],
        out_specs=pl.BlockSpec((tm, d), lambda i: (i, 0)),
        out_shape=jax.ShapeDtypeStruct((m, d), BF16),
        compiler_params=_params(("parallel",)),
        name="rmsnorm1",
    )(x, w.reshape(1, d))


def _load_weight(w_ref, wb_ref, row_axis=1):
    @pl.when(pl.program_id(row_axis) == 0)
    def _():
        wb_ref[...] = w_ref[...].astype(BF16)


def _qkv_kernel(h_ref, w_ref, qw_ref, kw_ref, o_ref, wb_ref, *, qk_tiles, row_splits):
    j = pl.program_id(0)
    _load_weight(w_ref, wb_ref)
    bm, bn = o_ref.shape
    sub = bm // row_splits

    @pl.when(j < qk_tiles)
    def _():
        nw = jnp.where(j < qk_tiles // 2, qw_ref[...], kw_ref[...])
        for r in range(row_splits):
            rows = slice(r * sub, (r + 1) * sub)
            acc = jnp.dot(h_ref[rows, :], wb_ref[...], preferred_element_type=F32)
            for hh in range(bn // HEAD_DIM):
                a = acc[:, hh * HEAD_DIM:(hh + 1) * HEAD_DIM]
                y = a * lax.rsqrt(jnp.mean(a * a, axis=-1, keepdims=True) + EPS)
                o_ref[rows, hh * HEAD_DIM:(hh + 1) * HEAD_DIM] = (y * nw).astype(o_ref.dtype)

    @pl.when(j >= qk_tiles)
    def _():
        o_ref[...] = jnp.dot(h_ref[...], wb_ref[...], preferred_element_type=F32).astype(o_ref.dtype)


def _qkv_proj(h, w_in, q_norm_w, k_norm_w, n_cols, bm=1024, bn=1024):
    m, k = h.shape
    return pl.pallas_call(
        functools.partial(_qkv_kernel, qk_tiles=(2 * N_HEADS * HEAD_DIM) // bn, row_splits=4),
        grid=(n_cols // bn, m // bm),
        in_specs=[pl.BlockSpec((bm, k), lambda j, i: (i, 0)),
                  pl.BlockSpec((k, bn), lambda j, i: (0, j)),
                  pl.BlockSpec((1, HEAD_DIM), lambda j, i: (0, 0)),
                  pl.BlockSpec((1, HEAD_DIM), lambda j, i: (0, 0))],
        out_specs=pl.BlockSpec((bm, bn), lambda j, i: (i, j)),
        out_shape=jax.ShapeDtypeStruct((m, n_cols), BF16),
        scratch_shapes=[pltpu.VMEM((k, bn), BF16)],
        compiler_params=_params(("parallel", "arbitrary")),
        name="qkv_proj",
    )(h, w_in, q_norm_w.reshape(1, HEAD_DIM), k_norm_w.reshape(1, HEAD_DIM))


def _matmul_kernel(a_ref, w_ref, o_ref, wb_ref):
    _load_weight(w_ref, wb_ref)
    o_ref[...] = jnp.dot(a_ref[...], wb_ref[...], preferred_element_type=F32).astype(o_ref.dtype)


def _rest_proj(h, w_in, col0, n_cols, bm=1024, bn=1024):
    m, k = h.shape
    jb = col0 // bn
    return pl.pallas_call(
        _matmul_kernel,
        grid=(n_cols // bn, m // bm),
        in_specs=[pl.BlockSpec((bm, k), lambda j, i: (i, 0)),
                  pl.BlockSpec((k, bn), lambda j, i: (0, jb + j))],
        out_specs=pl.BlockSpec((bm, bn), lambda j, i: (i, j)),
        out_shape=jax.ShapeDtypeStruct((m, n_cols), F32),
        scratch_shapes=[pltpu.VMEM((k, bn), BF16)],
        compiler_params=_params(("parallel", "arbitrary")),
        name="rest_proj",
    )(h, w_in)


def _attn_prep_kernel(slope_ref, q_ref, k_ref, v_ref, qt_ref, vt_ref, rb_ref, kmean_ref,
                      *, n_blk, group, cols):
    blk = MOBA_BLOCK
    for jb in range(n_blk):
        rows = slice(jb * blk, (jb + 1) * blk)
        qt_ref[:, rows] = q_ref[rows, :].astype(F32).T.astype(BF16)
        c, g = divmod(jb, group)
        vt_ref[c, 0:HEAD_DIM, g * blk:(g + 1) * blk] = v_ref[rows, :].astype(F32).T.astype(BF16)
        vt_ref[c, HEAD_DIM:, g * blk:(g + 1) * blk] = jnp.ones((ONES_ROWS, blk), BF16)
        kmean_ref[jb:jb + 1, :] = jnp.sum(k_ref[rows, :].astype(F32), axis=0, keepdims=True) * (1.0 / blk)
    vt_ref[n_blk // group] = jnp.zeros(vt_ref.shape[1:], BF16)
    kmean = kmean_ref[...].astype(BF16)
    slope = slope_ref[:, 0:1]
    row = lax.broadcasted_iota(jnp.int32, (n_blk, cols), 0)
    col = lax.broadcasted_iota(jnp.int32, (n_blk, cols), 1)
    for qc in range(qt_ref.shape[1] // cols):
        csl = slice(qc * cols, (qc + 1) * cols)
        gate = jnp.dot(kmean, qt_ref[:, csl], preferred_element_type=F32)
        qblk = lax.shift_right_logical(col + qc * cols, blk.bit_length() - 1)
        g = jnp.where(row < qblk, gate, NEG_INF)
        bias = jnp.full((n_blk, cols), NEG_INF, F32)
        for r in range(MOBA_TOPK):
            mx = jnp.max(g, axis=0, keepdims=True)
            idx = jnp.min(jnp.where(g == mx, row, n_blk), axis=0, keepdims=True)
            pick = row == idx
            bias = jnp.where(pick, jnp.where(qblk > r, 0.0, NEG_INF), bias)
            g = jnp.where(pick, -jnp.inf, g)
        past = bias - slope * ((qblk - row) * blk).astype(F32)
        rb = jnp.where(row == qblk, 0.0, jnp.where(row < qblk, past, NEG_INF))
        rb_ref[:, csl] = rb * LOG2E


def _attn_kernel(slope_ref, qt_ref, k_ref, vt_ref, rb_ref, rbp_ref, o_ref, *scratch,
                 group, n_blk, heads, nq):
    u_refs, m_refs, dmat_refs, acc_refs, p_refs = (scratch[n * heads:(n + 1) * heads] for n in range(5))
    blk = MOBA_BLOCK
    cb = group * blk
    lg = group.bit_length() - 1
    i = pl.program_id(1)

    @pl.when(i == 0)
    def _per_head_setup():
        kk = lax.broadcasted_iota(jnp.int32, (blk, blk), 0)
        qq = lax.broadcasted_iota(jnp.int32, (blk, blk), 1)
        for hh in range(heads):
            d = slope_ref[hh][:, 0:1] * (qq - kk).astype(F32) * LOG2E
            dmat_refs[hh][0] = d
            dmat_refs[hh][1] = jnp.where(kk <= qq, d, -NEG_INF)
            m_refs[hh][...] = jnp.zeros_like(m_refs[hh])
            p_refs[hh][...] = jnp.zeros_like(p_refs[hh])

    k2 = (HEAD_DIM ** -0.5) * LOG2E
    n1 = jnp.where(i < n_blk // nq, lax.shift_right_logical(nq * (i + 1) + group - 1, lg), 0)
    n2 = lax.shift_right_logical(nq * i + group - 1, lg)
    common = jnp.minimum(n1, n2)
    m_prev = [m_refs[hh][...] for hh in range(heads)]
    for hh in range(heads):
        acc_refs[hh][...] = jnp.zeros_like(acc_refs[hh])

    def pass1(hh, c, mx):
        r0 = pl.multiple_of(c * cb, cb)
        k_chunk = k_ref[pl.ds(r0, cb), hh * HEAD_DIM:(hh + 1) * HEAD_DIM]
        s = jnp.dot(k_chunk, qt_ref[hh], preferred_element_type=F32)
        for g in range(group):
            j = c * group + g
            rows = slice(g * blk, (g + 1) * blk)
            cands = []
            for qb in range(nq):
                cols = slice(qb * blk, (qb + 1) * blk)
                own = (j == nq * i + qb).astype(jnp.int32)
                u = s[rows, cols] * k2 - dmat_refs[hh][own]
                u_refs[hh][c, rows, cols] = u
                cands.append(jnp.max(u, axis=0, keepdims=True))
            mx = jnp.maximum(mx, jnp.concatenate(cands, axis=1) + rb_ref[hh, pl.ds(j, 1), :])
        return mx

    def probs(hh, c):
        for g in range(group):
            j = c * group + g
            rows = slice(g * blk, (g + 1) * blk)
            p = jnp.exp2(u_refs[hh][c, rows, :] + (rbp_ref[hh, pl.ds(j, 1), :] - m_prev[hh]))
            p_refs[hh][rows, :] = p.astype(BF16)

    def pv(hh, c):
        n_chunk = vt_ref.shape[1] - 1
        acc_refs[hh][...] += jnp.dot(vt_ref[hh, jnp.where(c == 0, n_chunk, c - 1)], p_refs[hh][...],
                                     preferred_element_type=F32)

    def only1(c, mxs):
        return tuple(pass1(hh, c, mxs[hh]) for hh in range(heads))

    def only2(c, carry):
        for hh in range(heads):
            pv(hh, c)
        for hh in range(heads):
            probs(hh, c)
        return carry

    def both(c, mxs):
        for hh in range(heads):
            pv(hh, c)
        out = []
        for hh in range(heads):
            probs(hh, c)
            out.append(pass1(hh, c, mxs[hh]))
        return tuple(out)

    def both_twice(c2, mxs):
        return both(2 * c2 + 1, both(2 * c2, mxs))

    pairs = lax.shift_right_logical(common, 1)
    mxs = lax.fori_loop(0, pairs, both_twice,
                        tuple(jnp.full((1, nq * blk), -jnp.inf, F32) for _ in range(heads)))
    mxs = lax.fori_loop(2 * pairs, common, both, mxs)
    mxs = lax.fori_loop(common, n1, only1, mxs)
    lax.fori_loop(common, n2, only2, 0)
    for hh in range(heads):
        pv(hh, n2)
        m_refs[hh][...] = mxs[hh]

    @pl.when(i > 0)
    def _():
        for hh in range(heads):
            acc = acc_refs[hh][...]
            o = acc[0:HEAD_DIM, :] / acc[HEAD_DIM:HEAD_DIM + 1, :]
            for qb in range(nq):
                o_ref[qb * blk:(qb + 1) * blk, hh * HEAD_DIM:(hh + 1) * HEAD_DIM] = (
                    o[:, qb * blk:(qb + 1) * blk].T.astype(o_ref.dtype))


def _moba_attention(qkv, slopes, group=4, cols=2048, heads=2, nq=2):
    s = qkv.shape[0]
    blk = MOBA_BLOCK
    n_blk = s // blk
    n_chunk = n_blk // group
    vt_rows = HEAD_DIM + ONES_ROWS
    qt, vt, rb = pl.pallas_call(
        functools.partial(_attn_prep_kernel, n_blk=n_blk, group=group, cols=cols),
        grid=(N_HEADS,),
        in_specs=[pl.BlockSpec((None, 1, blk), lambda h: (h, 0, 0)),
                  pl.BlockSpec((s, HEAD_DIM), lambda h: (0, h)),
                  pl.BlockSpec((s, HEAD_DIM), lambda h: (0, N_HEADS + h)),
                  pl.BlockSpec((s, HEAD_DIM), lambda h: (0, 2 * N_HEADS + h))],
        out_specs=[pl.BlockSpec((None, HEAD_DIM, s), lambda h: (h, 0, 0)),
                   pl.BlockSpec((None, n_chunk + 1, vt_rows, group * blk), lambda h: (h, 0, 0, 0)),
                   pl.BlockSpec((None, n_blk, s), lambda h: (h, 0, 0))],
        out_shape=[jax.ShapeDtypeStruct((N_HEADS, HEAD_DIM, s), BF16),
                   jax.ShapeDtypeStruct((N_HEADS, n_chunk + 1, vt_rows, group * blk), BF16),
                   jax.ShapeDtypeStruct((N_HEADS, n_blk, s), F32)],
        scratch_shapes=[pltpu.VMEM((n_blk, HEAD_DIM), F32)],
        compiler_params=_params(("parallel",)),
        name="moba_prep",
    )(slopes, qkv, qkv, qkv)
    n_tiles = n_blk // nq
    last = n_tiles - 1
    qw = nq * blk
    hw = heads * HEAD_DIM
    k_col0 = (N_HEADS * HEAD_DIM) // hw
    once = pl.Buffered(1)
    per_head = lambda shape: [pltpu.VMEM(shape, F32) for _ in range(heads)]
    return pl.pallas_call(
        functools.partial(_attn_kernel, group=group, n_blk=n_blk, heads=heads, nq=nq),
        grid=(N_HEADS // heads, n_tiles + 1),
        in_specs=[pl.BlockSpec((heads, 1, blk), lambda h, i: (h, 0, 0)),
                  pl.BlockSpec((heads, HEAD_DIM, qw), lambda h, i: (h, 0, jnp.minimum(i, last))),
                  pl.BlockSpec((s, hw), lambda h, i: (0, k_col0 + h)),
                  pl.BlockSpec((heads, n_chunk + 1, vt_rows, group * blk), lambda h, i: (h, 0, 0, 0),
                               pipeline_mode=once),
                  pl.BlockSpec((heads, n_blk, qw), lambda h, i: (h, 0, jnp.minimum(i, last))),
                  pl.BlockSpec((heads, n_blk, qw), lambda h, i: (h, 0, jnp.maximum(i - 1, 0)))],
        out_specs=pl.BlockSpec((qw, hw), lambda h, i: (jnp.maximum(i - 1, 0), h)),
        out_shape=jax.ShapeDtypeStruct((s, N_HEADS * HEAD_DIM), BF16),
        scratch_shapes=(per_head((n_chunk, group * blk, qw))
                        + per_head((1, qw))
                        + per_head((2, blk, blk))
                        + per_head((vt_rows, qw))
                        + [pltpu.VMEM((group * blk, qw), BF16) for _ in range(heads)]),
        compiler_params=_params(("parallel", "arbitrary")),
        name="moba_attention",
    )(slopes, qt, qkv, vt, rb, rb)


def _lru_kernel(xr_ref, yr_ref, cw_ref, cb_ref, wa_ref, ba_ref, wx_ref, bx_ref, lam_ref, o_ref,
                xbuf_ref, a_ref, b_ref, h_ref, wab_ref, wxb_ref, hl_ref, ac_ref, *, ts, tc):
    t = pl.program_id(1)
    pad = 8

    @pl.when(t == 0)
    def _():
        xbuf_ref[0:pad, :] = jnp.zeros((pad, tc), F32)
        h_ref[...] = jnp.zeros_like(h_ref)
        wab_ref[...] = wa_ref[...].astype(BF16)
        wxb_ref[...] = wx_ref[...].astype(BF16)

    xbuf_ref[pad:pad + ts, :] = xr_ref[...]
    cw = cw_ref[...]
    u = cb_ref[...]
    for tap in range(CONV_WIDTH):
        off = pad - (CONV_WIDTH - 1) + tap
        u = u + xbuf_ref[off:off + ts, :] * cw[tap:tap + 1, :]
    xbuf_ref[0:pad, :] = xbuf_ref[ts:ts + pad, :]

    ub = u.astype(BF16)
    ga, gx = [], []
    for n in range(tc // LRU_BLOCK_W):
        un = ub[:, n * LRU_BLOCK_W:(n + 1) * LRU_BLOCK_W]
        ga.append(jnp.dot(un, wab_ref[n], preferred_element_type=F32))
        gx.append(jnp.dot(un, wxb_ref[n], preferred_element_type=F32))
    r = jax.nn.sigmoid(jnp.concatenate(ga, axis=1) + ba_ref[...])
    ig = jax.nn.sigmoid(jnp.concatenate(gx, axis=1) + bx_ref[...])
    log_a = -LRU_C * r * jax.nn.softplus(-lam_ref[...])
    a = jnp.exp(log_a)
    lane_tiles = tc // 128
    ng = ts // 8
    pitch = ng + 1

    def put(ref, val):
        for lt in range(lane_tiles):
            for sg in range(8):
                ref[lt, sg * pitch:sg * pitch + ng, :] = val[sg * ng:(sg + 1) * ng, lt * 128:(lt + 1) * 128]

    put(a_ref, a)
    one_minus_a2 = -jnp.tanh(log_a) * (a * a + 1.0)
    mult = jnp.where(one_minus_a2 == 0.0, 0.0, one_minus_a2 * lax.rsqrt(one_minus_a2))
    put(b_ref, mult * ig * u)

    row = lax.broadcasted_iota(jnp.int32, (8, tc), 0)

    def local(g, carry):
        hloc, acum = carry
        av = jnp.concatenate([a_ref[lt, pl.ds(g, 8, stride=pitch), :] for lt in range(lane_tiles)], axis=1)
        bv = jnp.concatenate([b_ref[lt, pl.ds(g, 8, stride=pitch), :] for lt in range(lane_tiles)], axis=1)
        hloc = av * hloc + bv
        acum = av * acum
        hl_ref[g] = hloc
        ac_ref[g] = acum
        return hloc, acum

    q, p = lax.fori_loop(0, ng, local, (jnp.zeros((8, tc), F32), jnp.ones((8, tc), F32)))
    for d in (1, 2, 4):
        keep = row >= d
        p_sh = pltpu.roll(p, d, 0)
        q_sh = pltpu.roll(q, d, 0)
        q = jnp.where(keep, p * q_sh + q, q)
        p = jnp.where(keep, p * p_sh, p)
    h_in = h_ref[...]
    after = p * h_in + q
    enter = jnp.where(row == 0, h_in, pltpu.roll(after, 1, 0))
    h_ref[...] = jnp.broadcast_to(after[7:8, :], (8, tc))

    def fix(g, c):
        hv = hl_ref[g] + ac_ref[g] * enter
        for lt in range(lane_tiles):
            b_ref[lt, pl.ds(g, 8, stride=pitch), :] = hv[:, lt * 128:(lt + 1) * 128]
        return c

    lax.fori_loop(0, ng, fix, 0)
    h_all = jnp.concatenate(
        [jnp.concatenate([b_ref[lt, sg * pitch:sg * pitch + ng, :] for sg in range(8)], axis=0)
         for lt in range(lane_tiles)], axis=1)
    o_ref[...] = (h_all * jax.nn.gelu(yr_ref[...])).astype(o_ref.dtype)


def _rglru(rest, conv_w, conv_b, w_rg_a, b_rg_a, w_rg_x, b_rg_x, lru_lambda, width, ts=512, tc=512):
    s = rest.shape[0]
    nct = width // tc
    nb = tc // LRU_BLOCK_W
    vec = lambda v: v.reshape(1, width)
    vspec = pl.BlockSpec((1, tc), lambda c, t: (0, c))
    wspec = pl.BlockSpec((nb, LRU_BLOCK_W, LRU_BLOCK_W), lambda c, t: (c, 0, 0))
    return pl.pallas_call(
        functools.partial(_lru_kernel, ts=ts, tc=tc),
        grid=(nct, s // ts),
        in_specs=[pl.BlockSpec((ts, tc), lambda c, t: (t, c)),
                  pl.BlockSpec((ts, tc), lambda c, t: (t, nct + c)),
                  pl.BlockSpec((CONV_WIDTH, tc), lambda c, t: (0, c)),
                  vspec, wspec, vspec, wspec, vspec, vspec],
        out_specs=pl.BlockSpec((ts, tc), lambda c, t: (t, c)),
        out_shape=jax.ShapeDtypeStruct((s, width), BF16),
        scratch_shapes=[pltpu.VMEM((ts + 8, tc), F32),
                        pltpu.VMEM((tc // 128, ts + 8, 128), F32),
                        pltpu.VMEM((tc // 128, ts + 8, 128), F32),
                        pltpu.VMEM((8, tc), F32),
                        pltpu.VMEM((nb, LRU_BLOCK_W, LRU_BLOCK_W), BF16),
                        pltpu.VMEM((nb, LRU_BLOCK_W, LRU_BLOCK_W), BF16),
                        pltpu.VMEM((ts // 8, 8, tc), F32),
                        pltpu.VMEM((ts // 8, 8, tc), F32)],
        compiler_params=_params(("parallel", "arbitrary")),
        name="rglru",
    )(rest, rest, conv_w, vec(conv_b), w_rg_a, vec(b_rg_a), w_rg_x, vec(b_rg_x), vec(lru_lambda))


def _merge_kernel(att_ref, lru_ref, wa_ref, wl_ref, ga_ref, gl_ref, o_ref, wab_ref, wlb_ref):
    _load_weight(wa_ref, wab_ref)
    _load_weight(wl_ref, wlb_ref)
    bm = o_ref.shape[0]
    sub = bm // 2
    for r in range(2):
        rows = slice(r * sub, (r + 1) * sub)
        pa = jnp.dot(att_ref[rows, :], wab_ref[...], preferred_element_type=F32)
        plru = jnp.dot(lru_ref[rows, :], wlb_ref[...], preferred_element_type=F32)
        o_ref[rows, :] = (jax.nn.sigmoid(ga_ref[rows, :]) * pa
                          + jax.nn.sigmoid(gl_ref[rows, :]) * plru).astype(o_ref.dtype)


def _merge(att, lru, w_att, w_lru, rest, gate_col0, bm=512, bn=1024):
    m, k = att.shape
    n = w_att.shape[1]
    ga0 = gate_col0 // bn
    gl0 = (gate_col0 + n) // bn
    once = pl.Buffered(1)
    return pl.pallas_call(
        _merge_kernel,
        grid=(n // bn, m // bm),
        in_specs=[pl.BlockSpec((bm, k), lambda j, i: (i, 0)),
                  pl.BlockSpec((bm, k), lambda j, i: (i, 0)),
                  pl.BlockSpec((k, bn), lambda j, i: (0, j), pipeline_mode=once),
                  pl.BlockSpec((k, bn), lambda j, i: (0, j), pipeline_mode=once),
                  pl.BlockSpec((bm, bn), lambda j, i: (i, ga0 + j)),
                  pl.BlockSpec((bm, bn), lambda j, i: (i, gl0 + j))],
        out_specs=pl.BlockSpec((bm, bn), lambda j, i: (i, j)),
        out_shape=jax.ShapeDtypeStruct((m, n), BF16),
        scratch_shapes=[pltpu.VMEM((k, bn), BF16), pltpu.VMEM((k, bn), BF16)],
        compiler_params=_params(("parallel", "arbitrary")),
        name="merge",
    )(att, lru, w_att, w_lru, rest, rest)


def _outproj_kernel(a_ref, w_ref, x_ref, nw_ref, x1_ref, h2_ref, wb_ref):
    _load_weight(w_ref, wb_ref, row_axis=0)
    x1 = x_ref[...] + jnp.dot(a_ref[...], wb_ref[...], preferred_element_type=F32)
    x1_ref[...] = x1
    y = x1 * lax.rsqrt(jnp.mean(x1 * x1, axis=-1, keepdims=True) + EPS)
    h2_ref[...] = (y * nw_ref[...]).astype(h2_ref.dtype)


def _outproj(merged, w_out, x, norm2_w, bm=512):
    m, k = merged.shape
    d = w_out.shape[1]
    return pl.pallas_call(
        _outproj_kernel,
        grid=(m // bm,),
        in_specs=[pl.BlockSpec((bm, k), lambda i: (i, 0)),
                  pl.BlockSpec((k, d), lambda i: (0, 0), pipeline_mode=pl.Buffered(1)),
                  pl.BlockSpec((bm, d), lambda i: (i, 0)),
                  pl.BlockSpec((1, d), lambda i: (0, 0))],
        out_specs=[pl.BlockSpec((bm, d), lambda i: (i, 0)),
                   pl.BlockSpec((bm, d), lambda i: (i, 0))],
        out_shape=[jax.ShapeDtypeStruct((m, d), F32), jax.ShapeDtypeStruct((m, d), BF16)],
        scratch_shapes=[pltpu.VMEM((k, d), BF16)],
        compiler_params=_params(("arbitrary",)),
        name="outproj",
    )(merged, w_out, x, norm2_w.reshape(1, d))


def _ffn_up_kernel(h_ref, wg_ref, wu_ref, o_ref, wgb_ref, wub_ref):
    _load_weight(wg_ref, wgb_ref)
    _load_weight(wu_ref, wub_ref)
    splits = 4
    sub = o_ref.shape[0] // splits
    for r in range(splits):
        rows = slice(r * sub, (r + 1) * sub)
        h = h_ref[rows, :]
        g = jnp.dot(h, wgb_ref[...], preferred_element_type=F32)
        u = jnp.dot(h, wub_ref[...], preferred_element_type=F32)
        o_ref[rows, :] = (jax.nn.silu(g) * u).astype(o_ref.dtype)


def _ffn_up(h2, w_gate, w_up, bm=2048, bn=512):
    m, k = h2.shape
    n = w_gate.shape[1]
    return pl.pallas_call(
        _ffn_up_kernel,
        grid=(n // bn, m // bm),
        in_specs=[pl.BlockSpec((bm, k), lambda j, i: (i, 0)),
                  pl.BlockSpec((k, bn), lambda j, i: (0, j)),
                  pl.BlockSpec((k, bn), lambda j, i: (0, j))],
        out_specs=pl.BlockSpec((bm, bn), lambda j, i: (i, j)),
        out_shape=jax.ShapeDtypeStruct((m, n), BF16),
        scratch_shapes=[pltpu.VMEM((k, bn), BF16), pltpu.VMEM((k, bn), BF16)],
        compiler_params=_params(("parallel", "arbitrary")),
        name="ffn_up",
    )(h2, w_gate, w_up)


def _ffn_down_kernel(a_ref, w_ref, x_ref, o_ref, wb_ref):
    _load_weight(w_ref, wb_ref)
    o_ref[...] = x_ref[...] + jnp.dot(a_ref[...], wb_ref[...], preferred_element_type=F32)


def _ffn_down(act, w_down, x1, bm=512, bn=512):
    m, k = act.shape
    n = w_down.shape[1]
    return pl.pallas_call(
        _ffn_down_kernel,
        grid=(n // bn, m // bm),
        in_specs=[pl.BlockSpec((bm, k), lambda j, i: (i, 0)),
                  pl.BlockSpec((k, bn), lambda j, i: (0, j)),
                  pl.BlockSpec((bm, bn), lambda j, i: (i, j))],
        out_specs=pl.BlockSpec((bm, bn), lambda j, i: (i, j)),
        out_shape=jax.ShapeDtypeStruct((m, n), F32),
        scratch_shapes=[pltpu.VMEM((k, bn), BF16)],
        compiler_params=_params(("parallel", "arbitrary")),
        name="ffn_down",
    )(act, w_down, x1)


def _layer(x, norm1_w, w_in, q_norm_w, k_norm_w, conv_w, conv_b, w_rg_a, b_rg_a, w_rg_x, b_rg_x,
           lru_lambda, w_proj_attn, w_proj_lru, w_out, norm2_w, w_ffn_gate, w_ffn_up, w_ffn_down):
    d = x.shape[1]
    att_w = N_HEADS * HEAD_DIM
    lru_w = w_proj_lru.shape[0]

    h = _rmsnorm(x, norm1_w)
    qkv = _qkv_proj(h, w_in, q_norm_w, k_norm_w, 3 * att_w)
    rest = _rest_proj(h, w_in, 3 * att_w, 2 * lru_w + 2 * d)

    head = jnp.arange(1, N_HEADS + 1, dtype=F32)
    slopes = jnp.broadcast_to(jnp.exp2(-8.0 * head / N_HEADS)[:, None, None], (N_HEADS, 1, MOBA_BLOCK))
    att = _moba_attention(qkv, slopes)

    lru = _rglru(rest, conv_w, conv_b, w_rg_a, b_rg_a, w_rg_x, b_rg_x, lru_lambda, lru_w)

    merged = _merge(att, lru, w_proj_attn, w_proj_lru, rest, 2 * lru_w)
    x1, h2 = _outproj(merged, w_out, x, norm2_w)
    act = _ffn_up(h2, w_ffn_gate, w_ffn_up)
    return _ffn_down(act, w_ffn_down, x1)


def kernel(x, norm1_w, w_in, q_norm_w, k_norm_w, conv_w, conv_b, w_rg_a, b_rg_a, w_rg_x, b_rg_x,
           lru_lambda, w_proj_attn, w_proj_lru, w_out, norm2_w, w_ffn_gate, w_ffn_up, w_ffn_down):
    b, s, d = x.shape
    assert b == 1, "kernel handles the batch-1 prefill shape"
    y = x.reshape(s, d)
    for layer in range(norm1_w.shape[0]):
        y = _layer(y, norm1_w[layer], w_in[layer], q_norm_w[layer], k_norm_w[layer], conv_w[layer],
                   conv_b[layer], w_rg_a[layer], b_rg_a[layer], w_rg_x[layer], b_rg_x[layer],
                   lru_lambda[layer], w_proj_attn[layer], w_proj_lru[layer], w_out[layer],
                   norm2_w[layer], w_ffn_gate[layer], w_ffn_up[layer], w_ffn_down[layer])
    return y.reshape(b, s, d)
```

```python
import functools

import jax
import jax.numpy as jnp
from jax import lax
from jax.experimental import pallas as pl
from jax.experimental.pallas import tpu as pltpu

F32 = jnp.float32
BF16 = jnp.bfloat16

N_HEADS = 16
HEAD_DIM = 128
MOBA_BLOCK = 256
MOBA_TOPK = 3
LRU_BLOCK_W = 128
CONV_WIDTH = 4
LRU_C = 8.0
EPS = 1e-6
NEG_INF = -1e30
LOG2E = 1.4426950408889634
ONES_ROWS = 16

V7X_VMEM_BYTES = 64 * 1024 * 1024
VMEM_LIMIT = 56 * 1024 * 1024


def _params(semantics):
    return pltpu.CompilerParams(dimension_semantics=semantics, vmem_limit_bytes=VMEM_LIMIT)


def _rmsnorm_kernel(x_ref, w_ref, o_ref):
    x = x_ref[...]
    y = x * lax.rsqrt(jnp.mean(x * x, axis=-1, keepdims=True) + EPS)
    o_ref[...] = (y * w_ref[...]).astype(o_ref.dtype)


def _rmsnorm(x, w, tm=512):
    m, d = x.shape
    return pl.pallas_call(
        _rmsnorm_kernel,
        grid=(m // tm,),
        in_specs=[pl.BlockSpec((tm, d), lambda i: (i, 0)),
                  pl.BlockSpec((1, d), lambda i: (0, 0))],
        out_specs=pl.BlockSpec((tm, d), lambda i: (i, 0)),
        out_shape=jax.ShapeDtypeStruct((m, d), BF16),
        compiler_params=_params(("parallel",)),
        name="rmsnorm1",
    )(x, w.reshape(1, d))


def _load_weight(w_ref, wb_ref, row_axis=1):
    @pl.when(pl.program_id(row_axis) == 0)
    def _():
        wb_ref[...] = w_ref[...].astype(BF16)


def _qkv_kernel(h_ref, w_ref, qw_ref, kw_ref, o_ref, wb_ref, *, qk_tiles, row_splits):
    j = pl.program_id(0)
    _load_weight(w_ref, wb_ref)
    bm, bn = o_ref.shape
    sub = bm // row_splits

    @pl.when(j < qk_tiles)
    def _():
        nw = jnp.where(j < qk_tiles // 2, qw_ref[...], kw_ref[...])
        for r in range(row_splits):
            rows = slice(r * sub, (r + 1) * sub)
            acc = jnp.dot(h_ref[rows, :], wb_ref[...], preferred_element_type=F32)
            for hh in range(bn // HEAD_DIM):
                a = acc[:, hh * HEAD_DIM:(hh + 1) * HEAD_DIM]
                y = a * lax.rsqrt(jnp.mean(a * a, axis=-1, keepdims=True) + EPS)
                o_ref[rows, hh * HEAD_DIM:(hh + 1) * HEAD_DIM] = (y * nw).astype(o_ref.dtype)

    @pl.when(j >= qk_tiles)
    def _():
        o_ref[...] = jnp.dot(h_ref[...], wb_ref[...], preferred_element_type=F32).astype(o_ref.dtype)


def _qkv_proj(h, w_in, q_norm_w, k_norm_w, n_cols, bm=1024, bn=1024):
    m, k = h.shape
    return pl.pallas_call(
        functools.partial(_qkv_kernel, qk_tiles=(2 * N_HEADS * HEAD_DIM) // bn, row_splits=4),
        grid=(n_cols // bn, m // bm),
        in_specs=[pl.BlockSpec((bm, k), lambda j, i: (i, 0)),
                  pl.BlockSpec((k, bn), lambda j, i: (0, j)),
                  pl.BlockSpec((1, HEAD_DIM), lambda j, i: (0, 0)),
                  pl.BlockSpec((1, HEAD_DIM), lambda j, i: (0, 0))],
        out_specs=pl.BlockSpec((bm, bn), lambda j, i: (i, j)),
        out_shape=jax.ShapeDtypeStruct((m, n_cols), BF16),
        scratch_shapes=[pltpu.VMEM((k, bn), BF16)],
        compiler_params=_params(("parallel", "arbitrary")),
        name="qkv_proj",
    )(h, w_in, q_norm_w.reshape(1, HEAD_DIM), k_norm_w.reshape(1, HEAD_DIM))


def _matmul_kernel(a_ref, w_ref, o_ref, wb_ref):
    _load_weight(w_ref, wb_ref)
    o_ref[...] = jnp.dot(a_ref[...], wb_ref[...], preferred_element_type=F32).astype(o_ref.dtype)


def _rest_proj(h, w_in, col0, n_cols, bm=1024, bn=1024):
    m, k = h.shape
    jb = col0 // bn
    return pl.pallas_call(
        _matmul_kernel,
        grid=(n_cols // bn, m // bm),
        in_specs=[pl.BlockSpec((bm, k), lambda j, i: (i, 0)),
                  pl.BlockSpec((k, bn), lambda j, i: (0, jb + j))],
        out_specs=pl.BlockSpec((bm, bn), lambda j, i: (i, j)),
        out_shape=jax.ShapeDtypeStruct((m, n_cols), F32),
        scratch_shapes=[pltpu.VMEM((k, bn), BF16)],
        compiler_params=_params(("parallel", "arbitrary")),
        name="rest_proj",
    )(h, w_in)


def _attn_prep_kernel(slope_ref, q_ref, k_ref, v_ref, qt_ref, vt_ref, rb_ref, kmean_ref,
                      *, n_blk, group, cols):
    blk = MOBA_BLOCK
    for jb in range(n_blk):
        rows = slice(jb * blk, (jb + 1) * blk)
        qt_ref[:, rows] = q_ref[rows, :].astype(F32).T.astype(BF16)
        c, g = divmod(jb, group)
        vt_ref[c, 0:HEAD_DIM, g * blk:(g + 1) * blk] = v_ref[rows, :].astype(F32).T.astype(BF16)
        vt_ref[c, HEAD_DIM:, g * blk:(g + 1) * blk] = jnp.ones((ONES_ROWS, blk), BF16)
        kmean_ref[jb:jb + 1, :] = jnp.sum(k_ref[rows, :].astype(F32), axis=0, keepdims=True) * (1.0 / blk)
    vt_ref[n_blk // group] = jnp.zeros(vt_ref.shape[1:], BF16)
    kmean = kmean_ref[...].astype(BF16)
    slope = slope_ref[:, 0:1]
    row = lax.broadcasted_iota(jnp.int32, (n_blk, cols), 0)
    col = lax.broadcasted_iota(jnp.int32, (n_blk, cols), 1)
    for qc in range(qt_ref.shape[1] // cols):
        csl = slice(qc * cols, (qc + 1) * cols)
        gate = jnp.dot(kmean, qt_ref[:, csl], preferred_element_type=F32)
        qblk = lax.shift_right_logical(col + qc * cols, blk.bit_length() - 1)
        g = jnp.where(row < qblk, gate, NEG_INF)
        bias = jnp.full((n_blk, cols), NEG_INF, F32)
        for r in range(MOBA_TOPK):
            mx = jnp.max(g, axis=0, keepdims=True)
            idx = jnp.min(jnp.where(g == mx, row, n_blk), axis=0, keepdims=True)
            pick = row == idx
            bias = jnp.where(pick, jnp.where(qblk > r, 0.0, NEG_INF), bias)
            g = jnp.where(pick, -jnp.inf, g)
        past = bias - slope * ((qblk - row) * blk).astype(F32)
        rb = jnp.where(row == qblk, 0.0, jnp.where(row < qblk, past, NEG_INF))
        rb_ref[:, csl] = rb * LOG2E


def _attn_kernel(slope_ref, qt_ref, k_ref, vt_ref, rb_ref, rbp_ref, o_ref, *scratch,
                 group, n_blk, heads, nq):
    u_refs, m_refs, dmat_refs, acc_refs, p_refs = (scratch[n * heads:(n + 1) * heads] for n in range(5))
    blk = MOBA_BLOCK
    cb = group * blk
    lg = group.bit_length() - 1
    i = pl.program_id(1)

    @pl.when(i == 0)
    def _per_head_setup():
        kk = lax.broadcasted_iota(jnp.int32, (blk, blk), 0)
        qq = lax.broadcasted_iota(jnp.int32, (blk, blk), 1)
        for hh in range(heads):
            d = slope_ref[hh][:, 0:1] * (qq - kk).astype(F32) * LOG2E
            dmat_refs[hh][0] = d
            dmat_refs[hh][1] = jnp.where(kk <= qq, d, -NEG_INF)
            m_refs[hh][...] = jnp.zeros_like(m_refs[hh])
            p_refs[hh][...] = jnp.zeros_like(p_refs[hh])

    k2 = (HEAD_DIM ** -0.5) * LOG2E
    n1 = jnp.where(i < n_blk // nq, lax.shift_right_logical(nq * (i + 1) + group - 1, lg), 0)
    n2 = lax.shift_right_logical(nq * i + group - 1, lg)
    common = jnp.minimum(n1, n2)
    m_prev = [m_refs[hh][...] for hh in range(heads)]
    for hh in range(heads):
        acc_refs[hh][...] = jnp.zeros_like(acc_refs[hh])

    def pass1(hh, c, mx):
        r0 = pl.multiple_of(c * cb, cb)
        k_chunk = k_ref[pl.ds(r0, cb), hh * HEAD_DIM:(hh + 1) * HEAD_DIM]
        s = jnp.dot(k_chunk, qt_ref[hh], preferred_element_type=F32)
        for g in range(group):
            j = c * group + g
            rows = slice(g * blk, (g + 1) * blk)
            cands = []
            for qb in range(nq):
                cols = slice(qb * blk, (qb + 1) * blk)
                own = (j == nq * i + qb).astype(jnp.int32)
                u = s[rows, cols] * k2 - dmat_refs[hh][own]
                u_refs[hh][c, rows, cols] = u
                cands.append(jnp.max(u, axis=0, keepdims=True))
            mx = jnp.maximum(mx, jnp.concatenate(cands, axis=1) + rb_ref[hh, pl.ds(j, 1), :])
        return mx

    def probs(hh, c):
        for g in range(group):
            j = c * group + g
            rows = slice(g * blk, (g + 1) * blk)
            p = jnp.exp2(u_refs[hh][c, rows, :] + (rbp_ref[hh, pl.ds(j, 1), :] - m_prev[hh]))
            p_refs[hh][rows, :] = p.astype(BF16)

    def pv(hh, c):
        n_chunk = vt_ref.shape[1] - 1
        acc_refs[hh][...] += jnp.dot(vt_ref[hh, jnp.where(c == 0, n_chunk, c - 1)], p_refs[hh][...],
                                     preferred_element_type=F32)

    def only1(c, mxs):
        return tuple(pass1(hh, c, mxs[hh]) for hh in range(heads))

    def only2(c, carry):
        for hh in range(heads):
            pv(hh, c)
        for hh in range(heads):
            probs(hh, c)
        return carry

    def both(c, mxs):
        for hh in range(heads):
            pv(hh, c)
        out = []
        for hh in range(heads):
            probs(hh, c)
            out.append(pass1(hh, c, mxs[hh]))
        return tuple(out)

    def both_twice(c2, mxs):
        return both(2 * c2 + 1, both(2 * c2, mxs))

    pairs = lax.shift_right_logical(common, 1)
    mxs = lax.fori_loop(0, pairs, both_twice,
                        tuple(jnp.full((1, nq * blk), -jnp.inf, F32) for _ in range(heads)))
    mxs = lax.fori_loop(2 * pairs, common, both, mxs)
    mxs = lax.fori_loop(common, n1, only1, mxs)
    lax.fori_loop(common, n2, only2, 0)
    for hh in range(heads):
        pv(hh, n2)
        m_refs[hh][...] = mxs[hh]

    @pl.when(i > 0)
    def _():
        for hh in range(heads):
            acc = acc_refs[hh][...]
            o = acc[0:HEAD_DIM, :] / acc[HEAD_DIM:HEAD_DIM + 1, :]
            for qb in range(nq):
                o_ref[qb * blk:(qb + 1) * blk, hh * HEAD_DIM:(hh + 1) * HEAD_DIM] = (
                    o[:, qb * blk:(qb + 1) * blk].T.astype(o_ref.dtype))


def _moba_attention(qkv, slopes, group=4, cols=2048, heads=2, nq=2):
    s = qkv.shape[0]
    blk = MOBA_BLOCK
    n_blk = s // blk
    n_chunk = n_blk // group
    vt_rows = HEAD_DIM + ONES_ROWS
    qt, vt, rb = pl.pallas_call(
        functools.partial(_attn_prep_kernel, n_blk=n_blk, group=group, cols=cols),
        grid=(N_HEADS,),
        in_specs=[pl.BlockSpec((None, 1, blk), lambda h: (h, 0, 0)),
                  pl.BlockSpec((s, HEAD_DIM), lambda h: (0, h)),
                  pl.BlockSpec((s, HEAD_DIM), lambda h: (0, N_HEADS + h)),
                  pl.BlockSpec((s, HEAD_DIM), lambda h: (0, 2 * N_HEADS + h))],
        out_specs=[pl.BlockSpec((None, HEAD_DIM, s), lambda h: (h, 0, 0)),
                   pl.BlockSpec((None, n_chunk + 1, vt_rows, group * blk), lambda h: (h, 0, 0, 0)),
                   pl.BlockSpec((None, n_blk, s), lambda h: (h, 0, 0))],
        out_shape=[jax.ShapeDtypeStruct((N_HEADS, HEAD_DIM, s), BF16),
                   jax.ShapeDtypeStruct((N_HEADS, n_chunk + 1, vt_rows, group * blk), BF16),
                   jax.ShapeDtypeStruct((N_HEADS, n_blk, s), F32)],
        scratch_shapes=[pltpu.VMEM((n_blk, HEAD_DIM), F32)],
        compiler_params=_params(("parallel",)),
        name="moba_prep",
    )(slopes, qkv, qkv, qkv)
    n_tiles = n_blk // nq
    last = n_tiles - 1
    qw = nq * blk
    hw = heads * HEAD_DIM
    k_col0 = (N_HEADS * HEAD_DIM) // hw
    once = pl.Buffered(1)
    per_head = lambda shape: [pltpu.VMEM(shape, F32) for _ in range(heads)]
    return pl.pallas_call(
        functools.partial(_attn_kernel, group=group, n_blk=n_blk, heads=heads, nq=nq),
        grid=(N_HEADS // heads, n_tiles + 1),
        in_specs=[pl.BlockSpec((heads, 1, blk), lambda h, i: (h, 0, 0)),
                  pl.BlockSpec((heads, HEAD_DIM, qw), lambda h, i: (h, 0, jnp.minimum(i, last))),
                  pl.BlockSpec((s, hw), lambda h, i: (0, k_col0 + h)),
                  pl.BlockSpec((heads, n_chunk + 1, vt_rows, group * blk), lambda h, i: (h, 0, 0, 0),
                               pipeline_mode=once),
                  pl.BlockSpec((heads, n_blk, qw), lambda h, i: (h, 0, jnp.minimum(i, last))),
                  pl.BlockSpec((heads, n_blk, qw), lambda h, i: (h, 0, jnp.maximum(i - 1, 0)))],
        out_specs=pl.BlockSpec((qw, hw), lambda h, i: (jnp.maximum(i - 1, 0), h)),
        out_shape=jax.ShapeDtypeStruct((s, N_HEADS * HEAD_DIM), BF16),
        scratch_shapes=(per_head((n_chunk, group * blk, qw))
                        + per_head((1, qw))
                        + per_head((2, blk, blk))
                        + per_head((vt_rows, qw))
                        + [pltpu.VMEM((group * blk, qw), BF16) for _ in range(heads)]),
        compiler_params=_params(("parallel", "arbitrary")),
        name="moba_attention",
    )(slopes, qt, qkv, vt, rb, rb)


def _lru_kernel(xr_ref, yr_ref, cw_ref, cb_ref, wa_ref, ba_ref, wx_ref, bx_ref, lam_ref, o_ref,
                xs_ref, xp_ref, hx_ref, a_ref, b_ref, h_ref, wab_ref, wxb_ref, *, ts, tc):
    t = pl.program_id(1)
    lane_tiles = tc // 128
    ng = ts // 8
    pitch = ng + 1
    lead = 8
    hist = CONV_WIDTH - 1
    row = lax.broadcasted_iota(jnp.int32, (8, tc), 0)

    @pl.when(t == 0)
    def _():
        hx_ref[...] = jnp.zeros_like(hx_ref)
        h_ref[...] = jnp.zeros_like(h_ref)
        wab_ref[...] = wa_ref[...].astype(BF16)
        wxb_ref[...] = wx_ref[...].astype(BF16)

    x = xr_ref[...]
    for lt in range(lane_tiles):
        for sg in range(8):
            xs_ref[lt, sg * pitch:sg * pitch + ng, :] = x[sg * ng:(sg + 1) * ng, lt * 128:(lt + 1) * 128]

    def gather(g, c):
        r0 = pl.multiple_of((lead + g) * 8, 8)
        xp_ref[pl.ds(r0, 8), :] = jnp.concatenate(
            [xs_ref[lt, pl.ds(g, 8, stride=pitch), :] for lt in range(lane_tiles)], axis=1)
        return c

    lax.fori_loop(0, ng, gather, 0, unroll=8)
    for k in range(1, hist + 1):
        tail = xp_ref[(lead + ng - k) * 8:(lead + ng - k + 1) * 8, :]
        before_tile = jnp.broadcast_to(hx_ref[k - 1][7:8, :], (8, tc))
        xp_ref[(lead - k) * 8:(lead - k + 1) * 8, :] = jnp.where(row == 0, before_tile, pltpu.roll(tail, 1, 0))
        hx_ref[k - 1] = tail

    cw = cw_ref[...]
    u = cb_ref[...]
    for tap in range(CONV_WIDTH):
        off = (lead - hist + tap) * 8
        u = u + xp_ref[off:off + ts, :] * cw[tap:tap + 1, :]

    ub = u.astype(BF16)
    ga, gx = [], []
    for n in range(tc // LRU_BLOCK_W):
        un = ub[:, n * LRU_BLOCK_W:(n + 1) * LRU_BLOCK_W]
        ga.append(jnp.dot(un, wab_ref[n], preferred_element_type=F32))
        gx.append(jnp.dot(un, wxb_ref[n], preferred_element_type=F32))
    r = jax.nn.sigmoid(jnp.concatenate(ga, axis=1) + ba_ref[...])
    ig = jax.nn.sigmoid(jnp.concatenate(gx, axis=1) + bx_ref[...])
    log_a = -LRU_C * r * jax.nn.softplus(-lam_ref[...])
    a = jnp.exp(log_a)
    a_ref[...] = a
    one_minus_a2 = -jnp.tanh(log_a) * (a * a + 1.0)
    mult = jnp.where(one_minus_a2 == 0.0, 0.0, one_minus_a2 * lax.rsqrt(one_minus_a2))
    b_ref[...] = mult * ig * u

    def local(g, carry):
        hloc, acum = carry
        r0 = pl.multiple_of(g * 8, 8)
        av = a_ref[pl.ds(r0, 8), :]
        hloc = av * hloc + b_ref[pl.ds(r0, 8), :]
        acum = av * acum
        b_ref[pl.ds(r0, 8), :] = hloc
        a_ref[pl.ds(r0, 8), :] = acum
        return hloc, acum

    q, p = lax.fori_loop(0, ng, local, (jnp.zeros((8, tc), F32), jnp.ones((8, tc), F32)), unroll=8)
    for d in (1, 2, 4):
        keep = row >= d
        p_sh = pltpu.roll(p, d, 0)
        q_sh = pltpu.roll(q, d, 0)
        q = jnp.where(keep, p * q_sh + q, q)
        p = jnp.where(keep, p * p_sh, p)
    h_in = h_ref[...]
    after = p * h_in + q
    enter = jnp.where(row == 0, h_in, pltpu.roll(after, 1, 0))
    h_ref[...] = jnp.broadcast_to(after[7:8, :], (8, tc))

    def scatter(g, c):
        r0 = pl.multiple_of(g * 8, 8)
        hv = b_ref[pl.ds(r0, 8), :] + a_ref[pl.ds(r0, 8), :] * enter
        for lt in range(lane_tiles):
            xs_ref[lt, pl.ds(g, 8, stride=pitch), :] = hv[:, lt * 128:(lt + 1) * 128]
        return c

    lax.fori_loop(0, ng, scatter, 0, unroll=8)
    h_all = jnp.concatenate(
        [jnp.concatenate([xs_ref[lt, sg * pitch:sg * pitch + ng, :] for sg in range(8)], axis=0)
         for lt in range(lane_tiles)], axis=1)
    o_ref[...] = (h_all * jax.nn.gelu(yr_ref[...])).astype(o_ref.dtype)


def _rglru(rest, conv_w, conv_b, w_rg_a, b_rg_a, w_rg_x, b_rg_x, lru_lambda, width, ts=512, tc=512):
    s = rest.shape[0]
    nct = width // tc
    nb = tc // LRU_BLOCK_W
    vec = lambda v: v.reshape(1, width)
    vspec = pl.BlockSpec((1, tc), lambda c, t: (0, c))
    wspec = pl.BlockSpec((nb, LRU_BLOCK_W, LRU_BLOCK_W), lambda c, t: (c, 0, 0))
    return pl.pallas_call(
        functools.partial(_lru_kernel, ts=ts, tc=tc),
        grid=(nct, s // ts),
        in_specs=[pl.BlockSpec((ts, tc), lambda c, t: (t, c)),
                  pl.BlockSpec((ts, tc), lambda c, t: (t, nct + c)),
                  pl.BlockSpec((CONV_WIDTH, tc), lambda c, t: (0, c)),
                  vspec, wspec, vspec, wspec, vspec, vspec],
        out_specs=pl.BlockSpec((ts, tc), lambda c, t: (t, c)),
        out_shape=jax.ShapeDtypeStruct((s, width), BF16),
        scratch_shapes=[pltpu.VMEM((tc // 128, ts + 8, 128), F32),
                        pltpu.VMEM((ts + 64, tc), F32),
                        pltpu.VMEM((CONV_WIDTH - 1, 8, tc), F32),
                        pltpu.VMEM((ts, tc), F32),
                        pltpu.VMEM((ts, tc), F32),
                        pltpu.VMEM((8, tc), F32),
                        pltpu.VMEM((nb, LRU_BLOCK_W, LRU_BLOCK_W), BF16),
                        pltpu.VMEM((nb, LRU_BLOCK_W, LRU_BLOCK_W), BF16)],
        compiler_params=_params(("parallel", "arbitrary")),
        name="rglru",
    )(rest, rest, conv_w, vec(conv_b), w_rg_a, vec(b_rg_a), w_rg_x, vec(b_rg_x), vec(lru_lambda))


def _merge_kernel(att_ref, lru_ref, wa_ref, wl_ref, ga_ref, gl_ref, o_ref, wab_ref, wlb_ref):
    _load_weight(wa_ref, wab_ref)
    _load_weight(wl_ref, wlb_ref)
    bm = o_ref.shape[0]
    sub = bm // 2
    for r in range(2):
        rows = slice(r * sub, (r + 1) * sub)
        pa = jnp.dot(att_ref[rows, :], wab_ref[...], preferred_element_type=F32)
        plru = jnp.dot(lru_ref[rows, :], wlb_ref[...], preferred_element_type=F32)
        o_ref[rows, :] = (jax.nn.sigmoid(ga_ref[rows, :]) * pa
                          + jax.nn.sigmoid(gl_ref[rows, :]) * plru).astype(o_ref.dtype)


def _merge(att, lru, w_att, w_lru, rest, gate_col0, bm=512, bn=1024):
    m, k = att.shape
    n = w_att.shape[1]
    ga0 = gate_col0 // bn
    gl0 = (gate_col0 + n) // bn
    once = pl.Buffered(1)
    return pl.pallas_call(
        _merge_kernel,
        grid=(n // bn, m // bm),
        in_specs=[pl.BlockSpec((bm, k), lambda j, i: (i, 0)),
                  pl.BlockSpec((bm, k), lambda j, i: (i, 0)),
                  pl.BlockSpec((k, bn), lambda j, i: (0, j), pipeline_mode=once),
                  pl.BlockSpec((k, bn), lambda j, i: (0, j), pipeline_mode=once),
                  pl.BlockSpec((bm, bn), lambda j, i: (i, ga0 + j)),
                  pl.BlockSpec((bm, bn), lambda j, i: (i, gl0 + j))],
        out_specs=pl.BlockSpec((bm, bn), lambda j, i: (i, j)),
        out_shape=jax.ShapeDtypeStruct((m, n), BF16),
        scratch_shapes=[pltpu.VMEM((k, bn), BF16), pltpu.VMEM((k, bn), BF16)],
        compiler_params=_params(("parallel", "arbitrary")),
        name="merge",
    )(att, lru, w_att, w_lru, rest, rest)


def _outproj_kernel(a_ref, w_ref, x_ref, nw_ref, x1_ref, h2_ref, wb_ref):
    _load_weight(w_ref, wb_ref, row_axis=0)
    x1 = x_ref[...] + jnp.dot(a_ref[...], wb_ref[...], preferred_element_type=F32)
    x1_ref[...] = x1
    y = x1 * lax.rsqrt(jnp.mean(x1 * x1, axis=-1, keepdims=True) + EPS)
    h2_ref[...] = (y * nw_ref[...]).astype(h2_ref.dtype)


def _outproj(merged, w_out, x, norm2_w, bm=512):
    m, k = merged.shape
    d = w_out.shape[1]
    return pl.pallas_call(
        _outproj_kernel,
        grid=(m // bm,),
        in_specs=[pl.BlockSpec((bm, k), lambda i: (i, 0)),
                  pl.BlockSpec((k, d), lambda i: (0, 0), pipeline_mode=pl.Buffered(1)),
                  pl.BlockSpec((bm, d), lambda i: (i, 0)),
                  pl.BlockSpec((1, d), lambda i: (0, 0))],
        out_specs=[pl.BlockSpec((bm, d), lambda i: (i, 0)),
                   pl.BlockSpec((bm, d), lambda i: (i, 0))],
        out_shape=[jax.ShapeDtypeStruct((m, d), F32), jax.ShapeDtypeStruct((m, d), BF16)],
        scratch_shapes=[pltpu.VMEM((k, d), BF16)],
        compiler_params=_params(("arbitrary",)),
        name="outproj",
    )(merged, w_out, x, norm2_w.reshape(1, d))


def _ffn_up_kernel(h_ref, wg_ref, wu_ref, o_ref, wgb_ref, wub_ref):
    _load_weight(wg_ref, wgb_ref)
    _load_weight(wu_ref, wub_ref)
    splits = 4
    sub = o_ref.shape[0] // splits
    for r in range(splits):
        rows = slice(r * sub, (r + 1) * sub)
        h = h_ref[rows, :]
        g = jnp.dot(h, wgb_ref[...], preferred_element_type=F32)
        u = jnp.dot(h, wub_ref[...], preferred_element_type=F32)
        o_ref[rows, :] = (jax.nn.silu(g) * u).astype(o_ref.dtype)


def _ffn_up(h2, w_gate, w_up, bm=2048, bn=512):
    m, k = h2.shape
    n = w_gate.shape[1]
    return pl.pallas_call(
        _ffn_up_kernel,
        grid=(n // bn, m // bm),
        in_specs=[pl.BlockSpec((bm, k), lambda j, i: (i, 0)),
                  pl.BlockSpec((k, bn), lambda j, i: (0, j)),
                  pl.BlockSpec((k, bn), lambda j, i: (0, j))],
        out_specs=pl.BlockSpec((bm, bn), lambda j, i: (i, j)),
        out_shape=jax.ShapeDtypeStruct((m, n), BF16),
        scratch_shapes=[pltpu.VMEM((k, bn), BF16), pltpu.VMEM((k, bn), BF16)],
        compiler_params=_params(("parallel", "arbitrary")),
        name="ffn_up",
    )(h2, w_gate, w_up)


def _ffn_down_kernel(a_ref, w_ref, x_ref, o_ref, wb_ref):
    _load_weight(w_ref, wb_ref)
    o_ref[...] = x_ref[...] + jnp.dot(a_ref[...], wb_ref[...], preferred_element_type=F32)


def _ffn_down(act, w_down, x1, bm=512, bn=512):
    m, k = act.shape
    n = w_down.shape[1]
    return pl.pallas_call(
        _ffn_down_kernel,
        grid=(n // bn, m // bm),
        in_specs=[pl.BlockSpec((bm, k), lambda j, i: (i, 0)),
                  pl.BlockSpec((k, bn), lambda j, i: (0, j)),
                  pl.BlockSpec((bm, bn), lambda j, i: (i, j))],
        out_specs=pl.BlockSpec((bm, bn), lambda j, i: (i, j)),
        out_shape=jax.ShapeDtypeStruct((m, n), F32),
        scratch_shapes=[pltpu.VMEM((k, bn), BF16)],
        compiler_params=_params(("parallel", "arbitrary")),
        name="ffn_down",
    )(act, w_down, x1)


def _layer(x, norm1_w, w_in, q_norm_w, k_norm_w, conv_w, conv_b, w_rg_a, b_rg_a, w_rg_x, b_rg_x,
           lru_lambda, w_proj_attn, w_proj_lru, w_out, norm2_w, w_ffn_gate, w_ffn_up, w_ffn_down):
    d = x.shape[1]
    att_w = N_HEADS * HEAD_DIM
    lru_w = w_proj_lru.shape[0]

    h = _rmsnorm(x, norm1_w)
    qkv = _qkv_proj(h, w_in, q_norm_w, k_norm_w, 3 * att_w)
    rest = _rest_proj(h, w_in, 3 * att_w, 2 * lru_w + 2 * d)

    head = jnp.arange(1, N_HEADS + 1, dtype=F32)
    slopes = jnp.broadcast_to(jnp.exp2(-8.0 * head / N_HEADS)[:, None, None], (N_HEADS, 1, MOBA_BLOCK))
    att = _moba_attention(qkv, slopes)

    lru = _rglru(rest, conv_w, conv_b, w_rg_a, b_rg_a, w_rg_x, b_rg_x, lru_lambda, lru_w)

    merged = _merge(att, lru, w_proj_attn, w_proj_lru, rest, 2 * lru_w)
    x1, h2 = _outproj(merged, w_out, x, norm2_w)
    act = _ffn_up(h2, w_ffn_gate, w_ffn_up)
    return _ffn_down(act, w_ffn_down, x1)


def kernel(x, norm1_w, w_in, q_norm_w, k_norm_w, conv_w, conv_b, w_rg_a, b_rg_a, w_rg_x, b_rg_x,
           lru_lambda, w_proj_attn, w_proj_lru, w_out, norm2_w, w_ffn_gate, w_ffn_up, w_ffn_down):
    b, s, d = x.shape
    assert b == 1, "kernel handles the batch-1 prefill shape"
    y = x.reshape(s, d)
    for layer in range(norm1_w.shape[0]):
        y = _layer(y, norm1_w[layer], w_in[layer], q_norm_w[layer], k_norm_w[layer], conv_w[layer],
                   conv_b[layer], w_rg_a[layer], b_rg_a[layer], w_rg_x[layer], b_rg_x[layer],
                   lru_lambda[layer], w_proj_attn[layer], w_proj_lru[layer], w_out[layer],
                   norm2_w[layer], w_ffn_gate[layer], w_ffn_up[layer], w_ffn_down[layer])
    return y.reshape(b, s, d)
```

```python
import functools

import jax
import jax.numpy as jnp
from jax import lax
from jax.experimental import pallas as pl
from jax.experimental.pallas import tpu as pltpu

F32 = jnp.float32
BF16 = jnp.bfloat16

N_HEADS = 16
HEAD_DIM = 128
MOBA_BLOCK = 256
MOBA_TOPK = 3
LRU_BLOCK_W = 128
CONV_WIDTH = 4
LRU_C = 8.0
EPS = 1e-6
NEG_INF = -1e30
LOG2E = 1.4426950408889634
ONES_ROWS = 16

V7X_VMEM_BYTES = 64 * 1024 * 1024
VMEM_LIMIT = 56 * 1024 * 1024


def _params(semantics):
    return pltpu.CompilerParams(dimension_semantics=semantics, vmem_limit_bytes=VMEM_LIMIT)


def _rmsnorm_kernel(x_ref, w_ref, o_ref):
    x = x_ref[...]
    y = x * lax.rsqrt(jnp.mean(x * x, axis=-1, keepdims=True) + EPS)
    o_ref[...] = (y * w_ref[...]).astype(o_ref.dtype)


def _rmsnorm(x, w, tm=512):
    m, d = x.shape
    return pl.pallas_call(
        _rmsnorm_kernel,
        grid=(m // tm,),
        in_specs=[pl.BlockSpec((tm, d), lambda i: (i, 0)),
                  pl.BlockSpec((1, d), lambda i: (0, 0))],
        out_specs=pl.BlockSpec((tm, d), lambda i: (i, 0)),
        out_shape=jax.ShapeDtypeStruct((m, d), BF16),
        compiler_params=_params(("parallel",)),
        name="rmsnorm1",
    )(x, w.reshape(1, d))


def _load_weight(w_ref, wb_ref, row_axis=1):
    @pl.when(pl.program_id(row_axis) == 0)
    def _():
        wb_ref[...] = w_ref[...].astype(BF16)


def _qkv_kernel(h_ref, w_ref, qw_ref, kw_ref, o_ref, wb_ref, *, qk_tiles, row_splits):
    j = pl.program_id(0)
    _load_weight(w_ref, wb_ref)
    bm, bn = o_ref.shape
    sub = bm // row_splits

    @pl.when(j < qk_tiles)
    def _():
        nw = jnp.where(j < qk_tiles // 2, qw_ref[...], kw_ref[...])
        for r in range(row_splits):
            rows = slice(r * sub, (r + 1) * sub)
            acc = jnp.dot(h_ref[rows, :], wb_ref[...], preferred_element_type=F32)
            for hh in range(bn // HEAD_DIM):
                a = acc[:, hh * HEAD_DIM:(hh + 1) * HEAD_DIM]
                y = a * lax.rsqrt(jnp.mean(a * a, axis=-1, keepdims=True) + EPS)
                o_ref[rows, hh * HEAD_DIM:(hh + 1) * HEAD_DIM] = (y * nw).astype(o_ref.dtype)

    @pl.when(j >= qk_tiles)
    def _():
        o_ref[...] = jnp.dot(h_ref[...], wb_ref[...], preferred_element_type=F32).astype(o_ref.dtype)


def _qkv_proj(h, w_in, q_norm_w, k_norm_w, n_cols, bm=1024, bn=1024):
    m, k = h.shape
    return pl.pallas_call(
        functools.partial(_qkv_kernel, qk_tiles=(2 * N_HEADS * HEAD_DIM) // bn, row_splits=4),
        grid=(n_cols // bn, m // bm),
        in_specs=[pl.BlockSpec((bm, k), lambda j, i: (i, 0)),
                  pl.BlockSpec((k, bn), lambda j, i: (0, j)),
                  pl.BlockSpec((1, HEAD_DIM), lambda j, i: (0, 0)),
                  pl.BlockSpec((1, HEAD_DIM), lambda j, i: (0, 0))],
        out_specs=pl.BlockSpec((bm, bn), lambda j, i: (i, j)),
        out_shape=jax.ShapeDtypeStruct((m, n_cols), BF16),
        scratch_shapes=[pltpu.VMEM((k, bn), BF16)],
        compiler_params=_params(("parallel", "arbitrary")),
        name="qkv_proj",
    )(h, w_in, q_norm_w.reshape(1, HEAD_DIM), k_norm_w.reshape(1, HEAD_DIM))


def _matmul_kernel(a_ref, w_ref, o_ref, wb_ref):
    _load_weight(w_ref, wb_ref)
    o_ref[...] = jnp.dot(a_ref[...], wb_ref[...], preferred_element_type=F32).astype(o_ref.dtype)


def _rest_proj(h, w_in, col0, n_cols, bm=1024, bn=1024):
    m, k = h.shape
    jb = col0 // bn
    return pl.pallas_call(
        _matmul_kernel,
        grid=(n_cols // bn, m // bm),
        in_specs=[pl.BlockSpec((bm, k), lambda j, i: (i, 0)),
                  pl.BlockSpec((k, bn), lambda j, i: (0, jb + j))],
        out_specs=pl.BlockSpec((bm, bn), lambda j, i: (i, j)),
        out_shape=jax.ShapeDtypeStruct((m, n_cols), F32),
        scratch_shapes=[pltpu.VMEM((k, bn), BF16)],
        compiler_params=_params(("parallel", "arbitrary")),
        name="rest_proj",
    )(h, w_in)


def _attn_prep_kernel(slope_ref, q_ref, k_ref, v_ref, qt_ref, vt_ref, rb_ref, kmean_ref,
                      *, n_blk, group, cols):
    blk = MOBA_BLOCK
    for jb in range(n_blk):
        rows = slice(jb * blk, (jb + 1) * blk)
        qt_ref[:, rows] = q_ref[rows, :].astype(F32).T.astype(BF16)
        c, g = divmod(jb, group)
        vt_ref[c, 0:HEAD_DIM, g * blk:(g + 1) * blk] = v_ref[rows, :].astype(F32).T.astype(BF16)
        vt_ref[c, HEAD_DIM:, g * blk:(g + 1) * blk] = jnp.ones((ONES_ROWS, blk), BF16)
        kmean_ref[jb:jb + 1, :] = jnp.sum(k_ref[rows, :].astype(F32), axis=0, keepdims=True) * (1.0 / blk)
    vt_ref[n_blk // group] = jnp.zeros(vt_ref.shape[1:], BF16)
    kmean = kmean_ref[...].astype(BF16)
    slope = slope_ref[:, 0:1]
    row = lax.broadcasted_iota(jnp.int32, (n_blk, cols), 0)
    col = lax.broadcasted_iota(jnp.int32, (n_blk, cols), 1)
    for qc in range(qt_ref.shape[1] // cols):
        csl = slice(qc * cols, (qc + 1) * cols)
        gate = jnp.dot(kmean, qt_ref[:, csl], preferred_element_type=F32)
        qblk = lax.shift_right_logical(col + qc * cols, blk.bit_length() - 1)
        g = jnp.where(row < qblk, gate, NEG_INF)
        bias = jnp.full((n_blk, cols), NEG_INF, F32)
        for r in range(MOBA_TOPK):
            mx = jnp.max(g, axis=0, keepdims=True)
            idx = jnp.min(jnp.where(g == mx, row, n_blk), axis=0, keepdims=True)
            pick = row == idx
            bias = jnp.where(pick, jnp.where(qblk > r, 0.0, NEG_INF), bias)
            g = jnp.where(pick, -jnp.inf, g)
        past = bias - slope * ((qblk - row) * blk).astype(F32)
        rb = jnp.where(row == qblk, 0.0, jnp.where(row < qblk, past, NEG_INF))
        rb_ref[:, csl] = rb * LOG2E


def _attn_kernel(slope_ref, qt_ref, k_ref, vt_ref, rb_ref, rbp_ref, o_ref, *scratch,
                 group, n_blk, heads, nq):
    u_refs, m_refs, dmat_refs, acc_refs, p_refs = (scratch[n * heads:(n + 1) * heads] for n in range(5))
    blk = MOBA_BLOCK
    cb = group * blk
    lg = group.bit_length() - 1
    i = pl.program_id(1)

    @pl.when(i == 0)
    def _per_head_setup():
        kk = lax.broadcasted_iota(jnp.int32, (blk, blk), 0)
        qq = lax.broadcasted_iota(jnp.int32, (blk, blk), 1)
        for hh in range(heads):
            d = slope_ref[hh][:, 0:1] * (qq - kk).astype(F32) * LOG2E
            dmat_refs[hh][0] = d
            dmat_refs[hh][1] = jnp.where(kk <= qq, d, -NEG_INF)
            m_refs[hh][...] = jnp.zeros_like(m_refs[hh])
            p_refs[hh][...] = jnp.zeros_like(p_refs[hh])

    k2 = (HEAD_DIM ** -0.5) * LOG2E
    n1 = jnp.where(i < n_blk // nq, lax.shift_right_logical(nq * (i + 1) + group - 1, lg), 0)
    n2 = lax.shift_right_logical(nq * i + group - 1, lg)
    common = jnp.minimum(n1, n2)
    m_prev = [m_refs[hh][...] for hh in range(heads)]
    for hh in range(heads):
        acc_refs[hh][...] = jnp.zeros_like(acc_refs[hh])

    def pass1(hh, c, mx):
        r0 = pl.multiple_of(c * cb, cb)
        k_chunk = k_ref[pl.ds(r0, cb), hh * HEAD_DIM:(hh + 1) * HEAD_DIM]
        s = jnp.dot(k_chunk, qt_ref[hh], preferred_element_type=F32)
        for g in range(group):
            j = c * group + g
            rows = slice(g * blk, (g + 1) * blk)
            cands = []
            for qb in range(nq):
                cols = slice(qb * blk, (qb + 1) * blk)
                own = (j == nq * i + qb).astype(jnp.int32)
                u = s[rows, cols] * k2 - dmat_refs[hh][own]
                u_refs[hh][c, rows, cols] = u
                cands.append(jnp.max(u, axis=0, keepdims=True))
            mx = jnp.maximum(mx, jnp.concatenate(cands, axis=1) + rb_ref[hh, pl.ds(j, 1), :])
        return mx

    def probs(hh, c):
        for g in range(group):
            j = c * group + g
            rows = slice(g * blk, (g + 1) * blk)
            p = jnp.exp2(u_refs[hh][c, rows, :] + (rbp_ref[hh, pl.ds(j, 1), :] - m_prev[hh]))
            p_refs[hh][rows, :] = p.astype(BF16)

    def pv(hh, c):
        n_chunk = vt_ref.shape[1] - 1
        acc_refs[hh][...] += jnp.dot(vt_ref[hh, jnp.where(c == 0, n_chunk, c - 1)], p_refs[hh][...],
                                     preferred_element_type=F32)

    def only1(c, mxs):
        return tuple(pass1(hh, c, mxs[hh]) for hh in range(heads))

    def only2(c, carry):
        for hh in range(heads):
            pv(hh, c)
        for hh in range(heads):
            probs(hh, c)
        return carry

    def both(c, mxs):
        for hh in range(heads):
            pv(hh, c)
        out = []
        for hh in range(heads):
            probs(hh, c)
            out.append(pass1(hh, c, mxs[hh]))
        return tuple(out)

    def both_twice(c2, mxs):
        return both(2 * c2 + 1, both(2 * c2, mxs))

    pairs = lax.shift_right_logical(common, 1)
    mxs = lax.fori_loop(0, pairs, both_twice,
                        tuple(jnp.full((1, nq * blk), -jnp.inf, F32) for _ in range(heads)))
    mxs = lax.fori_loop(2 * pairs, common, both, mxs)
    mxs = lax.fori_loop(common, n1, only1, mxs)
    lax.fori_loop(common, n2, only2, 0)
    for hh in range(heads):
        pv(hh, n2)
        m_refs[hh][...] = mxs[hh]

    @pl.when(i > 0)
    def _():
        for hh in range(heads):
            acc = acc_refs[hh][...]
            o = acc[0:HEAD_DIM, :] / acc[HEAD_DIM:HEAD_DIM + 1, :]
            for qb in range(nq):
                o_ref[qb * blk:(qb + 1) * blk, hh * HEAD_DIM:(hh + 1) * HEAD_DIM] = (
                    o[:, qb * blk:(qb + 1) * blk].T.astype(o_ref.dtype))


def _moba_attention(qkv, slopes, group=4, cols=2048, heads=2, nq=2):
    s = qkv.shape[0]
    blk = MOBA_BLOCK
    n_blk = s // blk
    n_chunk = n_blk // group
    vt_rows = HEAD_DIM + ONES_ROWS
    qt, vt, rb = pl.pallas_call(
        functools.partial(_attn_prep_kernel, n_blk=n_blk, group=group, cols=cols),
        grid=(N_HEADS,),
        in_specs=[pl.BlockSpec((None, 1, blk), lambda h: (h, 0, 0)),
                  pl.BlockSpec((s, HEAD_DIM), lambda h: (0, h)),
                  pl.BlockSpec((s, HEAD_DIM), lambda h: (0, N_HEADS + h)),
                  pl.BlockSpec((s, HEAD_DIM), lambda h: (0, 2 * N_HEADS + h))],
        out_specs=[pl.BlockSpec((None, HEAD_DIM, s), lambda h: (h, 0, 0)),
                   pl.BlockSpec((None, n_chunk + 1, vt_rows, group * blk), lambda h: (h, 0, 0, 0)),
                   pl.BlockSpec((None, n_blk, s), lambda h: (h, 0, 0))],
        out_shape=[jax.ShapeDtypeStruct((N_HEADS, HEAD_DIM, s), BF16),
                   jax.ShapeDtypeStruct((N_HEADS, n_chunk + 1, vt_rows, group * blk), BF16),
                   jax.ShapeDtypeStruct((N_HEADS, n_blk, s), F32)],
        scratch_shapes=[pltpu.VMEM((n_blk, HEAD_DIM), F32)],
        compiler_params=_params(("parallel",)),
        name="moba_prep",
    )(slopes, qkv, qkv, qkv)
    n_tiles = n_blk // nq
    last = n_tiles - 1
    qw = nq * blk
    hw = heads * HEAD_DIM
    k_col0 = (N_HEADS * HEAD_DIM) // hw
    once = pl.Buffered(1)
    per_head = lambda shape: [pltpu.VMEM(shape, F32) for _ in range(heads)]
    return pl.pallas_call(
        functools.partial(_attn_kernel, group=group, n_blk=n_blk, heads=heads, nq=nq),
        grid=(N_HEADS // heads, n_tiles + 1),
        in_specs=[pl.BlockSpec((heads, 1, blk), lambda h, i: (h, 0, 0)),
                  pl.BlockSpec((heads, HEAD_DIM, qw), lambda h, i: (h, 0, jnp.minimum(i, last))),
                  pl.BlockSpec((s, hw), lambda h, i: (0, k_col0 + h)),
                  pl.BlockSpec((heads, n_chunk + 1, vt_rows, group * blk), lambda h, i: (h, 0, 0, 0),
                               pipeline_mode=once),
                  pl.BlockSpec((heads, n_blk, qw), lambda h, i: (h, 0, jnp.minimum(i, last))),
                  pl.BlockSpec((heads, n_blk, qw), lambda h, i: (h, 0, jnp.maximum(i - 1, 0)))],
        out_specs=pl.BlockSpec((qw, hw), lambda h, i: (jnp.maximum(i - 1, 0), h)),
        out_shape=jax.ShapeDtypeStruct((s, N_HEADS * HEAD_DIM), BF16),
        scratch_shapes=(per_head((n_chunk, group * blk, qw))
                        + per_head((1, qw))
                        + per_head((2, blk, blk))
                        + per_head((vt_rows, qw))
                        + [pltpu.VMEM((group * blk, qw), BF16) for _ in range(heads)]),
        compiler_params=_params(("parallel", "arbitrary")),
        name="moba_attention",
    )(slopes, qt, qkv, vt, rb, rb)


def _lru_kernel(xr_ref, yr_ref, cw_ref, cb_ref, wa_ref, ba_ref, wx_ref, bx_ref, lam_ref, o_ref,
                xs_ref, xp_ref, hx_ref, a_ref, b_ref, hl_ref, ac_ref, h_ref, wab_ref, wxb_ref, *, ts, tc):
    t = pl.program_id(1)
    lane_tiles = tc // 128
    ng = ts // 8
    pitch = ng + 1
    lead = 8
    hist = CONV_WIDTH - 1
    row = lax.broadcasted_iota(jnp.int32, (8, tc), 0)

    @pl.when(t == 0)
    def _():
        hx_ref[...] = jnp.zeros_like(hx_ref)
        h_ref[...] = jnp.zeros_like(h_ref)
        wab_ref[...] = wa_ref[...].astype(BF16)
        wxb_ref[...] = wx_ref[...].astype(BF16)

    x = xr_ref[...]
    for lt in range(lane_tiles):
        for sg in range(8):
            xs_ref[lt, sg * pitch:sg * pitch + ng, :] = x[sg * ng:(sg + 1) * ng, lt * 128:(lt + 1) * 128]

    def gather(g, c):
        r0 = pl.multiple_of((lead + g) * 8, 8)
        xp_ref[pl.ds(r0, 8), :] = jnp.concatenate(
            [xs_ref[lt, pl.ds(g, 8, stride=pitch), :] for lt in range(lane_tiles)], axis=1)
        return c

    lax.fori_loop(0, ng, gather, 0, unroll=8)
    for k in range(1, hist + 1):
        tail = xp_ref[(lead + ng - k) * 8:(lead + ng - k + 1) * 8, :]
        before_tile = jnp.broadcast_to(hx_ref[k - 1][7:8, :], (8, tc))
        xp_ref[(lead - k) * 8:(lead - k + 1) * 8, :] = jnp.where(row == 0, before_tile, pltpu.roll(tail, 1, 0))
        hx_ref[k - 1] = tail

    cw = cw_ref[...]
    u = cb_ref[...]
    for tap in range(CONV_WIDTH):
        off = (lead - hist + tap) * 8
        u = u + xp_ref[off:off + ts, :] * cw[tap:tap + 1, :]

    ub = u.astype(BF16)
    ga, gx = [], []
    for n in range(tc // LRU_BLOCK_W):
        un = ub[:, n * LRU_BLOCK_W:(n + 1) * LRU_BLOCK_W]
        ga.append(jnp.dot(un, wab_ref[n], preferred_element_type=F32))
        gx.append(jnp.dot(un, wxb_ref[n], preferred_element_type=F32))
    r = jax.nn.sigmoid(jnp.concatenate(ga, axis=1) + ba_ref[...])
    ig = jax.nn.sigmoid(jnp.concatenate(gx, axis=1) + bx_ref[...])
    log_a = -LRU_C * r * jax.nn.softplus(-lam_ref[...])
    a = jnp.exp(log_a)
    a_ref[...] = a
    one_minus_a2 = -jnp.tanh(log_a) * (a * a + 1.0)
    mult = jnp.where(one_minus_a2 == 0.0, 0.0, one_minus_a2 * lax.rsqrt(one_minus_a2))
    b_ref[...] = mult * ig * u

    half = ng // 2

    def local(g, carry):
        out = []
        for hf in range(2):
            hloc, acum = carry[hf]
            r0 = pl.multiple_of((g + hf * half) * 8, 8)
            av = a_ref[pl.ds(r0, 8), :]
            hloc = av * hloc + b_ref[pl.ds(r0, 8), :]
            acum = av * acum
            hl_ref[pl.ds(r0, 8), :] = hloc
            ac_ref[pl.ds(r0, 8), :] = acum
            out.append((hloc, acum))
        return tuple(out)

    start = (jnp.zeros((8, tc), F32), jnp.ones((8, tc), F32))
    (q0, p0), (q1, p1) = lax.fori_loop(0, half, local, (start, start), unroll=8)
    p = p1 * p0
    q = p1 * q0 + q1
    for d in (1, 2, 4):
        keep = row >= d
        p_sh = pltpu.roll(p, d, 0)
        q_sh = pltpu.roll(q, d, 0)
        q = jnp.where(keep, p * q_sh + q, q)
        p = jnp.where(keep, p * p_sh, p)
    h_in = h_ref[...]
    after = p * h_in + q
    enter = jnp.where(row == 0, h_in, pltpu.roll(after, 1, 0))
    h_ref[...] = jnp.broadcast_to(after[7:8, :], (8, tc))

    enter_half = (enter, p0 * enter + q0)

    def scatter(g, c):
        for hf in range(2):
            gg = g + hf * half
            r0 = pl.multiple_of(gg * 8, 8)
            hv = hl_ref[pl.ds(r0, 8), :] + ac_ref[pl.ds(r0, 8), :] * enter_half[hf]
            for lt in range(lane_tiles):
                xs_ref[lt, pl.ds(gg, 8, stride=pitch), :] = hv[:, lt * 128:(lt + 1) * 128]
        return c

    lax.fori_loop(0, half, scatter, 0, unroll=4)
    h_all = jnp.concatenate(
        [jnp.concatenate([xs_ref[lt, sg * pitch:sg * pitch + ng, :] for sg in range(8)], axis=0)
         for lt in range(lane_tiles)], axis=1)
    o_ref[...] = (h_all * jax.nn.gelu(yr_ref[...])).astype(o_ref.dtype)


def _rglru(rest, conv_w, conv_b, w_rg_a, b_rg_a, w_rg_x, b_rg_x, lru_lambda, width, ts=512, tc=512):
    s = rest.shape[0]
    nct = width // tc
    nb = tc // LRU_BLOCK_W
    vec = lambda v: v.reshape(1, width)
    vspec = pl.BlockSpec((1, tc), lambda c, t: (0, c))
    wspec = pl.BlockSpec((nb, LRU_BLOCK_W, LRU_BLOCK_W), lambda c, t: (c, 0, 0))
    return pl.pallas_call(
        functools.partial(_lru_kernel, ts=ts, tc=tc),
        grid=(nct, s // ts),
        in_specs=[pl.BlockSpec((ts, tc), lambda c, t: (t, c)),
                  pl.BlockSpec((ts, tc), lambda c, t: (t, nct + c)),
                  pl.BlockSpec((CONV_WIDTH, tc), lambda c, t: (0, c)),
                  vspec, wspec, vspec, wspec, vspec, vspec],
        out_specs=pl.BlockSpec((ts, tc), lambda c, t: (t, c)),
        out_shape=jax.ShapeDtypeStruct((s, width), BF16),
        scratch_shapes=[pltpu.VMEM((tc // 128, ts + 8, 128), F32),
                        pltpu.VMEM((ts + 64, tc), F32),
                        pltpu.VMEM((CONV_WIDTH - 1, 8, tc), F32),
                        pltpu.VMEM((ts, tc), F32),
                        pltpu.VMEM((ts, tc), F32),
                        pltpu.VMEM((ts, tc), F32),
                        pltpu.VMEM((ts, tc), F32),
                        pltpu.VMEM((8, tc), F32),
                        pltpu.VMEM((nb, LRU_BLOCK_W, LRU_BLOCK_W), BF16),
                        pltpu.VMEM((nb, LRU_BLOCK_W, LRU_BLOCK_W), BF16)],
        compiler_params=_params(("parallel", "arbitrary")),
        name="rglru",
    )(rest, rest, conv_w, vec(conv_b), w_rg_a, vec(b_rg_a), w_rg_x, vec(b_rg_x), vec(lru_lambda))


def _merge_kernel(att_ref, lru_ref, wa_ref, wl_ref, ga_ref, gl_ref, o_ref, wab_ref, wlb_ref):
    _load_weight(wa_ref, wab_ref)
    _load_weight(wl_ref, wlb_ref)
    bm = o_ref.shape[0]
    sub = bm // 2
    for r in range(2):
        rows = slice(r * sub, (r + 1) * sub)
        pa = jnp.dot(att_ref[rows, :], wab_ref[...], preferred_element_type=F32)
        plru = jnp.dot(lru_ref[rows, :], wlb_ref[...], preferred_element_type=F32)
        o_ref[rows, :] = (jax.nn.sigmoid(ga_ref[rows, :]) * pa
                          + jax.nn.sigmoid(gl_ref[rows, :]) * plru).astype(o_ref.dtype)


def _merge(att, lru, w_att, w_lru, rest, gate_col0, bm=512, bn=1024):
    m, k = att.shape
    n = w_att.shape[1]
    ga0 = gate_col0 // bn
    gl0 = (gate_col0 + n) // bn
    once = pl.Buffered(1)
    return pl.pallas_call(
        _merge_kernel,
        grid=(n // bn, m // bm),
        in_specs=[pl.BlockSpec((bm, k), lambda j, i: (i, 0)),
                  pl.BlockSpec((bm, k), lambda j, i: (i, 0)),
                  pl.BlockSpec((k, bn), lambda j, i: (0, j), pipeline_mode=once),
                  pl.BlockSpec((k, bn), lambda j, i: (0, j), pipeline_mode=once),
                  pl.BlockSpec((bm, bn), lambda j, i: (i, ga0 + j)),
                  pl.BlockSpec((bm, bn), lambda j, i: (i, gl0 + j))],
        out_specs=pl.BlockSpec((bm, bn), lambda j, i: (i, j)),
        out_shape=jax.ShapeDtypeStruct((m, n), BF16),
        scratch_shapes=[pltpu.VMEM((k, bn), BF16), pltpu.VMEM((k, bn), BF16)],
        compiler_params=_params(("parallel", "arbitrary")),
        name="merge",
    )(att, lru, w_att, w_lru, rest, rest)


def _outproj_kernel(a_ref, w_ref, x_ref, nw_ref, x1_ref, h2_ref, wb_ref):
    _load_weight(w_ref, wb_ref, row_axis=0)
    x1 = x_ref[...] + jnp.dot(a_ref[...], wb_ref[...], preferred_element_type=F32)
    x1_ref[...] = x1
    y = x1 * lax.rsqrt(jnp.mean(x1 * x1, axis=-1, keepdims=True) + EPS)
    h2_ref[...] = (y * nw_ref[...]).astype(h2_ref.dtype)


def _outproj(merged, w_out, x, norm2_w, bm=512):
    m, k = merged.shape
    d = w_out.shape[1]
    return pl.pallas_call(
        _outproj_kernel,
        grid=(m // bm,),
        in_specs=[pl.BlockSpec((bm, k), lambda i: (i, 0)),
                  pl.BlockSpec((k, d), lambda i: (0, 0), pipeline_mode=pl.Buffered(1)),
                  pl.BlockSpec((bm, d), lambda i: (i, 0)),
                  pl.BlockSpec((1, d), lambda i: (0, 0))],
        out_specs=[pl.BlockSpec((bm, d), lambda i: (i, 0)),
                   pl.BlockSpec((bm, d), lambda i: (i, 0))],
        out_shape=[jax.ShapeDtypeStruct((m, d), F32), jax.ShapeDtypeStruct((m, d), BF16)],
        scratch_shapes=[pltpu.VMEM((k, d), BF16)],
        compiler_params=_params(("arbitrary",)),
        name="outproj",
    )(merged, w_out, x, norm2_w.reshape(1, d))


def _ffn_up_kernel(h_ref, wg_ref, wu_ref, o_ref, wgb_ref, wub_ref):
    _load_weight(wg_ref, wgb_ref)
    _load_weight(wu_ref, wub_ref)
    splits = 4
    sub = o_ref.shape[0] // splits
    for r in range(splits):
        rows = slice(r * sub, (r + 1) * sub)
        h = h_ref[rows, :]
        g = jnp.dot(h, wgb_ref[...], preferred_element_type=F32)
        u = jnp.dot(h, wub_ref[...], preferred_element_type=F32)
        o_ref[rows, :] = (jax.nn.silu(g) * u).astype(o_ref.dtype)


def _ffn_up(h2, w_gate, w_up, bm=2048, bn=512):
    m, k = h2.shape
    n = w_gate.shape[1]
    return pl.pallas_call(
        _ffn_up_kernel,
        grid=(n // bn, m // bm),
        in_specs=[pl.BlockSpec((bm, k), lambda j, i: (i, 0)),
                  pl.BlockSpec((k, bn), lambda j, i: (0, j)),
                  pl.BlockSpec((k, bn), lambda j, i: (0, j))],
        out_specs=pl.BlockSpec((bm, bn), lambda j, i: (i, j)),
        out_shape=jax.ShapeDtypeStruct((m, n), BF16),
        scratch_shapes=[pltpu.VMEM((k, bn), BF16), pltpu.VMEM((k, bn), BF16)],
        compiler_params=_params(("parallel", "arbitrary")),
        name="ffn_up",
    )(h2, w_gate, w_up)


def _ffn_down_kernel(a_ref, w_ref, x_ref, o_ref, wb_ref):
    _load_weight(w_ref, wb_ref)
    o_ref[...] = x_ref[...] + jnp.dot(a_ref[...], wb_ref[...], preferred_element_type=F32)


def _ffn_down(act, w_down, x1, bm=512, bn=512):
    m, k = act.shape
    n = w_down.shape[1]
    return pl.pallas_call(
        _ffn_down_kernel,
        grid=(n // bn, m // bm),
        in_specs=[pl.BlockSpec((bm, k), lambda j, i: (i, 0)),
                  pl.BlockSpec((k, bn), lambda j, i: (0, j)),
                  pl.BlockSpec((bm, bn), lambda j, i: (i, j))],
        out_specs=pl.BlockSpec((bm, bn), lambda j, i: (i, j)),
        out_shape=jax.ShapeDtypeStruct((m, n), F32),
        scratch_shapes=[pltpu.VMEM((k, bn), BF16)],
        compiler_params=_params(("parallel", "arbitrary")),
        name="ffn_down",
    )(act, w_down, x1)


def _layer(x, norm1_w, w_in, q_norm_w, k_norm_w, conv_w, conv_b, w_rg_a, b_rg_a, w_rg_x, b_rg_x,
           lru_lambda, w_proj_attn, w_proj_lru, w_out, norm2_w, w_ffn_gate, w_ffn_up, w_ffn_down):
    d = x.shape[1]
    att_w = N_HEADS * HEAD_DIM
    lru_w = w_proj_lru.shape[0]

    h = _rmsnorm(x, norm1_w)
    qkv = _qkv_proj(h, w_in, q_norm_w, k_norm_w, 3 * att_w)
    rest = _rest_proj(h, w_in, 3 * att_w, 2 * lru_w + 2 * d)

    head = jnp.arange(1, N_HEADS + 1, dtype=F32)
    slopes = jnp.broadcast_to(jnp.exp2(-8.0 * head / N_HEADS)[:, None, None], (N_HEADS, 1, MOBA_BLOCK))
    att = _moba_attention(qkv, slopes)

    lru = _rglru(rest, conv_w, conv_b, w_rg_a, b_rg_a, w_rg_x, b_rg_x, lru_lambda, lru_w)

    merged = _merge(att, lru, w_proj_attn, w_proj_lru, rest, 2 * lru_w)
    x1, h2 = _outproj(merged, w_out, x, norm2_w)
    act = _ffn_up(h2, w_ffn_gate, w_ffn_up)
    return _ffn_down(act, w_ffn_down, x1)


def kernel(x, norm1_w, w_in, q_norm_w, k_norm_w, conv_w, conv_b, w_rg_a, b_rg_a, w_rg_x, b_rg_x,
           lru_lambda, w_proj_attn, w_proj_lru, w_out, norm2_w, w_ffn_gate, w_ffn_up, w_ffn_down):
    b, s, d = x.shape
    assert b == 1, "kernel handles the batch-1 prefill shape"
    y = x.reshape(s, d)
    for layer in range(norm1_w.shape[0]):
        y = _layer(y, norm1_w[layer], w_in[layer], q_norm_w[layer], k_norm_w[layer], conv_w[layer],
                   conv_b[layer], w_rg_a[layer], b_rg_a[layer], w_rg_x[layer], b_rg_x[layer],
                   lru_lambda[layer], w_proj_attn[layer], w_proj_lru[layer], w_out[layer],
                   norm2_w[layer], w_ffn_gate[layer], w_ffn_up[layer], w_ffn_down[layer])
    return y.reshape(b, s, d)
```

```python
import functools

import jax
import jax.numpy as jnp
from jax import lax
from jax.experimental import pallas as pl
from jax.experimental.pallas import tpu as pltpu

F32 = jnp.float32
BF16 = jnp.bfloat16

N_HEADS = 16
HEAD_DIM = 128
MOBA_BLOCK = 256
MOBA_TOPK = 3
LRU_BLOCK_W = 128
CONV_WIDTH = 4
LRU_C = 8.0
EPS = 1e-6
NEG_INF = -1e30
LOG2E = 1.4426950408889634
ONES_ROWS = 16

V7X_VMEM_BYTES = 64 * 1024 * 1024
VMEM_LIMIT = 56 * 1024 * 1024


def _params(semantics):
    return pltpu.CompilerParams(dimension_semantics=semantics, vmem_limit_bytes=VMEM_LIMIT)


def _rmsnorm_kernel(x_ref, w_ref, o_ref):
    x = x_ref[...]
    y = x * lax.rsqrt(jnp.mean(x * x, axis=-1, keepdims=True) + EPS)
    o_ref[...] = (y * w_ref[...]).astype(o_ref.dtype)


def _rmsnorm(x, w, tm=512):
    m, d = x.shape
    return pl.pallas_call(
        _rmsnorm_kernel,
        grid=(m // tm,),
        in_specs=[pl.BlockSpec((tm, d), lambda i: (i, 0)),
                  pl.BlockSpec((1, d), lambda i: (0, 0))],
        out_specs=pl.BlockSpec((tm, d), lambda i: (i, 0)),
        out_shape=jax.ShapeDtypeStruct((m, d), BF16),
        compiler_params=_params(("parallel",)),
        name="rmsnorm1",
    )(x, w.reshape(1, d))


def _load_weight(w_ref, wb_ref, row_axis=1):
    @pl.when(pl.program_id(row_axis) == 0)
    def _():
        wb_ref[...] = w_ref[...].astype(BF16)


def _qkv_kernel(h_ref, w_ref, qw_ref, kw_ref, o_ref, wb_ref, *, qk_tiles, row_splits):
    j = pl.program_id(0)
    _load_weight(w_ref, wb_ref)
    n_heads_tile, bm, _ = o_ref.shape
    sub = bm // row_splits

    @pl.when(j < qk_tiles)
    def _():
        nw = jnp.where(j < qk_tiles // 2, qw_ref[...], kw_ref[...])
        for r in range(row_splits):
            rows = slice(r * sub, (r + 1) * sub)
            acc = jnp.dot(h_ref[rows, :], wb_ref[...], preferred_element_type=F32)
            for hh in range(n_heads_tile):
                a = acc[:, hh * HEAD_DIM:(hh + 1) * HEAD_DIM]
                y = a * lax.rsqrt(jnp.mean(a * a, axis=-1, keepdims=True) + EPS)
                o_ref[hh, rows, :] = (y * nw).astype(o_ref.dtype)

    @pl.when(j >= qk_tiles)
    def _():
        acc = jnp.dot(h_ref[...], wb_ref[...], preferred_element_type=F32)
        for hh in range(n_heads_tile):
            o_ref[hh] = acc[:, hh * HEAD_DIM:(hh + 1) * HEAD_DIM].astype(o_ref.dtype)


def _qkv_proj(h, w_in, q_norm_w, k_norm_w, n_cols, bm=1024, bn=1024):
    m, k = h.shape
    return pl.pallas_call(
        functools.partial(_qkv_kernel, qk_tiles=(2 * N_HEADS * HEAD_DIM) // bn, row_splits=4),
        grid=(n_cols // bn, m // bm),
        in_specs=[pl.BlockSpec((bm, k), lambda j, i: (i, 0)),
                  pl.BlockSpec((k, bn), lambda j, i: (0, j)),
                  pl.BlockSpec((1, HEAD_DIM), lambda j, i: (0, 0)),
                  pl.BlockSpec((1, HEAD_DIM), lambda j, i: (0, 0))],
        out_specs=pl.BlockSpec((bn // HEAD_DIM, bm, HEAD_DIM), lambda j, i: (j, i, 0)),
        out_shape=jax.ShapeDtypeStruct((n_cols // HEAD_DIM, m, HEAD_DIM), BF16),
        scratch_shapes=[pltpu.VMEM((k, bn), BF16)],
        compiler_params=_params(("parallel", "arbitrary")),
        name="qkv_proj",
    )(h, w_in, q_norm_w.reshape(1, HEAD_DIM), k_norm_w.reshape(1, HEAD_DIM))


def _matmul_kernel(a_ref, w_ref, o_ref, wb_ref):
    _load_weight(w_ref, wb_ref)
    o_ref[...] = jnp.dot(a_ref[...], wb_ref[...], preferred_element_type=F32).astype(o_ref.dtype)


def _rest_proj(h, w_in, col0, n_cols, bm=1024, bn=1024):
    m, k = h.shape
    jb = col0 // bn
    return pl.pallas_call(
        _matmul_kernel,
        grid=(n_cols // bn, m // bm),
        in_specs=[pl.BlockSpec((bm, k), lambda j, i: (i, 0)),
                  pl.BlockSpec((k, bn), lambda j, i: (0, jb + j))],
        out_specs=pl.BlockSpec((bm, bn), lambda j, i: (i, j)),
        out_shape=jax.ShapeDtypeStruct((m, n_cols), F32),
        scratch_shapes=[pltpu.VMEM((k, bn), BF16)],
        compiler_params=_params(("parallel", "arbitrary")),
        name="rest_proj",
    )(h, w_in)


def _attn_prep_kernel(slope_ref, q_ref, k_ref, v_ref, qt_ref, vt_ref, rb_ref, kmean_ref,
                      *, n_blk, group, cols):
    blk = MOBA_BLOCK
    for jb in range(n_blk):
        rows = slice(jb * blk, (jb + 1) * blk)
        qt_ref[:, rows] = q_ref[rows, :].astype(F32).T.astype(BF16)
        c, g = divmod(jb, group)
        vt_ref[c, 0:HEAD_DIM, g * blk:(g + 1) * blk] = v_ref[rows, :].astype(F32).T.astype(BF16)
        vt_ref[c, HEAD_DIM:, g * blk:(g + 1) * blk] = jnp.ones((ONES_ROWS, blk), BF16)
        kmean_ref[jb:jb + 1, :] = jnp.sum(k_ref[rows, :].astype(F32), axis=0, keepdims=True) * (1.0 / blk)
    vt_ref[n_blk // group] = jnp.zeros(vt_ref.shape[1:], BF16)
    kmean = kmean_ref[...].astype(BF16)
    slope = slope_ref[:, 0:1]
    row = lax.broadcasted_iota(jnp.int32, (n_blk, cols), 0)
    col = lax.broadcasted_iota(jnp.int32, (n_blk, cols), 1)
    for qc in range(qt_ref.shape[1] // cols):
        csl = slice(qc * cols, (qc + 1) * cols)
        gate = jnp.dot(kmean, qt_ref[:, csl], preferred_element_type=F32)
        qblk = lax.shift_right_logical(col + qc * cols, blk.bit_length() - 1)
        g = jnp.where(row < qblk, gate, NEG_INF)
        bias = jnp.full((n_blk, cols), NEG_INF, F32)
        for r in range(MOBA_TOPK):
            mx = jnp.max(g, axis=0, keepdims=True)
            idx = jnp.min(jnp.where(g == mx, row, n_blk), axis=0, keepdims=True)
            pick = row == idx
            bias = jnp.where(pick, jnp.where(qblk > r, 0.0, NEG_INF), bias)
            g = jnp.where(pick, -jnp.inf, g)
        past = bias - slope * ((qblk - row) * blk).astype(F32)
        rb = jnp.where(row == qblk, 0.0, jnp.where(row < qblk, past, NEG_INF))
        rb_ref[:, csl] = rb * LOG2E


def _attn_kernel(slope_ref, qt_ref, k_ref, vt_ref, rb_ref, rbp_ref, o_ref, *scratch,
                 group, n_blk, heads, nq):
    u_refs, m_refs, dmat_refs, acc_refs, p_refs = (scratch[n * heads:(n + 1) * heads] for n in range(5))
    blk = MOBA_BLOCK
    cb = group * blk
    lg = group.bit_length() - 1
    i = pl.program_id(1)

    @pl.when(i == 0)
    def _per_head_setup():
        kk = lax.broadcasted_iota(jnp.int32, (blk, blk), 0)
        qq = lax.broadcasted_iota(jnp.int32, (blk, blk), 1)
        for hh in range(heads):
            d = slope_ref[hh][:, 0:1] * (qq - kk).astype(F32) * LOG2E
            dmat_refs[hh][0] = d
            dmat_refs[hh][1] = jnp.where(kk <= qq, d, -NEG_INF)
            m_refs[hh][...] = jnp.zeros_like(m_refs[hh])
            p_refs[hh][...] = jnp.zeros_like(p_refs[hh])

    k2 = (HEAD_DIM ** -0.5) * LOG2E
    n1 = jnp.where(i < n_blk // nq, lax.shift_right_logical(nq * (i + 1) + group - 1, lg), 0)
    n2 = lax.shift_right_logical(nq * i + group - 1, lg)
    common = jnp.minimum(n1, n2)
    m_prev = [m_refs[hh][...] for hh in range(heads)]
    for hh in range(heads):
        acc_refs[hh][...] = jnp.zeros_like(acc_refs[hh])

    def pass1(hh, c, mx):
        r0 = pl.multiple_of(c * cb, cb)
        k_chunk = k_ref[hh, pl.ds(r0, cb), :]
        s = jnp.dot(k_chunk, qt_ref[hh], preferred_element_type=F32)
        for g in range(group):
            j = c * group + g
            rows = slice(g * blk, (g + 1) * blk)
            cands = []
            for qb in range(nq):
                cols = slice(qb * blk, (qb + 1) * blk)
                own = (j == nq * i + qb).astype(jnp.int32)
                u = s[rows, cols] * k2 - dmat_refs[hh][own]
                u_refs[hh][c, rows, cols] = u
                cands.append(jnp.max(u, axis=0, keepdims=True))
            mx = jnp.maximum(mx, jnp.concatenate(cands, axis=1) + rb_ref[hh, pl.ds(j, 1), :])
        return mx

    def probs(hh, c):
        for g in range(group):
            j = c * group + g
            rows = slice(g * blk, (g + 1) * blk)
            p = jnp.exp2(u_refs[hh][c, rows, :] + (rbp_ref[hh, pl.ds(j, 1), :] - m_prev[hh]))
            p_refs[hh][rows, :] = p.astype(BF16)

    def pv(hh, c):
        n_chunk = vt_ref.shape[1] - 1
        acc_refs[hh][...] += jnp.dot(vt_ref[hh, jnp.where(c == 0, n_chunk, c - 1)], p_refs[hh][...],
                                     preferred_element_type=F32)

    def only1(c, mxs):
        return tuple(pass1(hh, c, mxs[hh]) for hh in range(heads))

    def only2(c, carry):
        for hh in range(heads):
            pv(hh, c)
        for hh in range(heads):
            probs(hh, c)
        return carry

    def both(c, mxs):
        for hh in range(heads):
            pv(hh, c)
        out = []
        for hh in range(heads):
            probs(hh, c)
            out.append(pass1(hh, c, mxs[hh]))
        return tuple(out)

    def both_twice(c2, mxs):
        return both(2 * c2 + 1, both(2 * c2, mxs))

    pairs = lax.shift_right_logical(common, 1)
    mxs = lax.fori_loop(0, pairs, both_twice,
                        tuple(jnp.full((1, nq * blk), -jnp.inf, F32) for _ in range(heads)))
    mxs = lax.fori_loop(2 * pairs, common, both, mxs)
    mxs = lax.fori_loop(common, n1, only1, mxs)
    lax.fori_loop(common, n2, only2, 0)
    for hh in range(heads):
        pv(hh, n2)
        m_refs[hh][...] = mxs[hh]

    @pl.when(i > 0)
    def _():
        for hh in range(heads):
            acc = acc_refs[hh][...]
            o = acc[0:HEAD_DIM, :] / acc[HEAD_DIM:HEAD_DIM + 1, :]
            for qb in range(nq):
                o_ref[qb * blk:(qb + 1) * blk, hh * HEAD_DIM:(hh + 1) * HEAD_DIM] = (
                    o[:, qb * blk:(qb + 1) * blk].T.astype(o_ref.dtype))


def _moba_attention(qkv, slopes, group=4, cols=2048, heads=2, nq=2):
    s = qkv.shape[1]
    blk = MOBA_BLOCK
    n_blk = s // blk
    n_chunk = n_blk // group
    vt_rows = HEAD_DIM + ONES_ROWS
    qt, vt, rb = pl.pallas_call(
        functools.partial(_attn_prep_kernel, n_blk=n_blk, group=group, cols=cols),
        grid=(N_HEADS,),
        in_specs=[pl.BlockSpec((None, 1, blk), lambda h: (h, 0, 0)),
                  pl.BlockSpec((None, s, HEAD_DIM), lambda h: (h, 0, 0)),
                  pl.BlockSpec((None, s, HEAD_DIM), lambda h: (N_HEADS + h, 0, 0)),
                  pl.BlockSpec((None, s, HEAD_DIM), lambda h: (2 * N_HEADS + h, 0, 0))],
        out_specs=[pl.BlockSpec((None, HEAD_DIM, s), lambda h: (h, 0, 0)),
                   pl.BlockSpec((None, n_chunk + 1, vt_rows, group * blk), lambda h: (h, 0, 0, 0)),
                   pl.BlockSpec((None, n_blk, s), lambda h: (h, 0, 0))],
        out_shape=[jax.ShapeDtypeStruct((N_HEADS, HEAD_DIM, s), BF16),
                   jax.ShapeDtypeStruct((N_HEADS, n_chunk + 1, vt_rows, group * blk), BF16),
                   jax.ShapeDtypeStruct((N_HEADS, n_blk, s), F32)],
        scratch_shapes=[pltpu.VMEM((n_blk, HEAD_DIM), F32)],
        compiler_params=_params(("parallel",)),
        name="moba_prep",
    )(slopes, qkv, qkv, qkv)
    n_tiles = n_blk // nq
    last = n_tiles - 1
    qw = nq * blk
    hw = heads * HEAD_DIM
    k_blk0 = N_HEADS // heads
    once = pl.Buffered(1)
    per_head = lambda shape: [pltpu.VMEM(shape, F32) for _ in range(heads)]
    return pl.pallas_call(
        functools.partial(_attn_kernel, group=group, n_blk=n_blk, heads=heads, nq=nq),
        grid=(N_HEADS // heads, n_tiles + 1),
        in_specs=[pl.BlockSpec((heads, 1, blk), lambda h, i: (h, 0, 0)),
                  pl.BlockSpec((heads, HEAD_DIM, qw), lambda h, i: (h, 0, jnp.minimum(i, last))),
                  pl.BlockSpec((heads, s, HEAD_DIM), lambda h, i: (k_blk0 + h, 0, 0)),
                  pl.BlockSpec((heads, n_chunk + 1, vt_rows, group * blk), lambda h, i: (h, 0, 0, 0),
                               pipeline_mode=once),
                  pl.BlockSpec((heads, n_blk, qw), lambda h, i: (h, 0, jnp.minimum(i, last))),
                  pl.BlockSpec((heads, n_blk, qw), lambda h, i: (h, 0, jnp.maximum(i - 1, 0)))],
        out_specs=pl.BlockSpec((qw, hw), lambda h, i: (jnp.maximum(i - 1, 0), h)),
        out_shape=jax.ShapeDtypeStruct((s, N_HEADS * HEAD_DIM), BF16),
        scratch_shapes=(per_head((n_chunk, group * blk, qw))
                        + per_head((1, qw))
                        + per_head((2, blk, blk))
                        + per_head((vt_rows, qw))
                        + [pltpu.VMEM((group * blk, qw), BF16) for _ in range(heads)]),
        compiler_params=_params(("parallel", "arbitrary")),
        name="moba_attention",
    )(slopes, qt, qkv, vt, rb, rb)


def _lru_kernel(xr_ref, yr_ref, cw_ref, cb_ref, wa_ref, ba_ref, wx_ref, bx_ref, lam_ref, o_ref,
                xs_ref, xp_ref, hx_ref, a_ref, b_ref, hl_ref, ac_ref, h_ref, wab_ref, wxb_ref, *, ts, tc):
    t = pl.program_id(1)
    lane_tiles = tc // 128
    ng = ts // 8
    pitch = ng + 1
    lead = 8
    hist = CONV_WIDTH - 1
    row = lax.broadcasted_iota(jnp.int32, (8, tc), 0)

    @pl.when(t == 0)
    def _():
        hx_ref[...] = jnp.zeros_like(hx_ref)
        h_ref[...] = jnp.zeros_like(h_ref)
        wab_ref[...] = wa_ref[...].astype(BF16)
        wxb_ref[...] = wx_ref[...].astype(BF16)

    x = xr_ref[...]
    for lt in range(lane_tiles):
        for sg in range(8):
            xs_ref[lt, sg * pitch:sg * pitch + ng, :] = x[sg * ng:(sg + 1) * ng, lt * 128:(lt + 1) * 128]

    def gather(g, c):
        r0 = pl.multiple_of((lead + g) * 8, 8)
        xp_ref[pl.ds(r0, 8), :] = jnp.concatenate(
            [xs_ref[lt, pl.ds(g, 8, stride=pitch), :] for lt in range(lane_tiles)], axis=1)
        return c

    lax.fori_loop(0, ng, gather, 0, unroll=8)
    for k in range(1, hist + 1):
        tail = xp_ref[(lead + ng - k) * 8:(lead + ng - k + 1) * 8, :]
        before_tile = jnp.broadcast_to(hx_ref[k - 1][7:8, :], (8, tc))
        xp_ref[(lead - k) * 8:(lead - k + 1) * 8, :] = jnp.where(row == 0, before_tile, pltpu.roll(tail, 1, 0))
        hx_ref[k - 1] = tail

    cw = cw_ref[...]
    u = cb_ref[...]
    for tap in range(CONV_WIDTH):
        off = (lead - hist + tap) * 8
        u = u + xp_ref[off:off + ts, :] * cw[tap:tap + 1, :]

    ub = u.astype(BF16)
    ga, gx = [], []
    for n in range(tc // LRU_BLOCK_W):
        un = ub[:, n * LRU_BLOCK_W:(n + 1) * LRU_BLOCK_W]
        ga.append(jnp.dot(un, wab_ref[n], preferred_element_type=F32))
        gx.append(jnp.dot(un, wxb_ref[n], preferred_element_type=F32))
    r = jax.nn.sigmoid(jnp.concatenate(ga, axis=1) + ba_ref[...])
    ig = jax.nn.sigmoid(jnp.concatenate(gx, axis=1) + bx_ref[...])
    log_a = -LRU_C * r * jax.nn.softplus(-lam_ref[...])
    a = jnp.exp(log_a)
    a_ref[...] = a
    one_minus_a2 = -jnp.tanh(log_a) * (a * a + 1.0)
    mult = jnp.where(one_minus_a2 == 0.0, 0.0, one_minus_a2 * lax.rsqrt(one_minus_a2))
    b_ref[...] = mult * ig * u

    half = ng // 2

    def local(g, carry):
        out = []
        for hf in range(2):
            hloc, acum = carry[hf]
            r0 = pl.multiple_of((g + hf * half) * 8, 8)
            av = a_ref[pl.ds(r0, 8), :]
            hloc = av * hloc + b_ref[pl.ds(r0, 8), :]
            acum = av * acum
            hl_ref[pl.ds(r0, 8), :] = hloc
            ac_ref[pl.ds(r0, 8), :] = acum
            out.append((hloc, acum))
        return tuple(out)

    start = (jnp.zeros((8, tc), F32), jnp.ones((8, tc), F32))
    (q0, p0), (q1, p1) = lax.fori_loop(0, half, local, (start, start), unroll=8)
    p = p1 * p0
    q = p1 * q0 + q1
    for d in (1, 2, 4):
        keep = row >= d
        p_sh = pltpu.roll(p, d, 0)
        q_sh = pltpu.roll(q, d, 0)
        q = jnp.where(keep, p * q_sh + q, q)
        p = jnp.where(keep, p * p_sh, p)
    h_in = h_ref[...]
    after = p * h_in + q
    enter = jnp.where(row == 0, h_in, pltpu.roll(after, 1, 0))
    h_ref[...] = jnp.broadcast_to(after[7:8, :], (8, tc))

    enter_half = (enter, p0 * enter + q0)

    def scatter(g, c):
        for hf in range(2):
            gg = g + hf * half
            r0 = pl.multiple_of(gg * 8, 8)
            hv = hl_ref[pl.ds(r0, 8), :] + ac_ref[pl.ds(r0, 8), :] * enter_half[hf]
            for lt in range(lane_tiles):
                xs_ref[lt, pl.ds(gg, 8, stride=pitch), :] = hv[:, lt * 128:(lt + 1) * 128]
        return c

    lax.fori_loop(0, half, scatter, 0, unroll=4)
    h_all = jnp.concatenate(
        [jnp.concatenate([xs_ref[lt, sg * pitch:sg * pitch + ng, :] for sg in range(8)], axis=0)
         for lt in range(lane_tiles)], axis=1)
    o_ref[...] = (h_all * jax.nn.gelu(yr_ref[...])).astype(o_ref.dtype)


def _rglru(rest, conv_w, conv_b, w_rg_a, b_rg_a, w_rg_x, b_rg_x, lru_lambda, width, ts=512, tc=512):
    s = rest.shape[0]
    nct = width // tc
    nb = tc // LRU_BLOCK_W
    vec = lambda v: v.reshape(1, width)
    vspec = pl.BlockSpec((1, tc), lambda c, t: (0, c))
    wspec = pl.BlockSpec((nb, LRU_BLOCK_W, LRU_BLOCK_W), lambda c, t: (c, 0, 0))
    return pl.pallas_call(
        functools.partial(_lru_kernel, ts=ts, tc=tc),
        grid=(nct, s // ts),
        in_specs=[pl.BlockSpec((ts, tc), lambda c, t: (t, c)),
                  pl.BlockSpec((ts, tc), lambda c, t: (t, nct + c)),
                  pl.BlockSpec((CONV_WIDTH, tc), lambda c, t: (0, c)),
                  vspec, wspec, vspec, wspec, vspec, vspec],
        out_specs=pl.BlockSpec((ts, tc), lambda c, t: (t, c)),
        out_shape=jax.ShapeDtypeStruct((s, width), BF16),
        scratch_shapes=[pltpu.VMEM((tc // 128, ts + 8, 128), F32),
                        pltpu.VMEM((ts + 64, tc), F32),
                        pltpu.VMEM((CONV_WIDTH - 1, 8, tc), F32),
                        pltpu.VMEM((ts, tc), F32),
                        pltpu.VMEM((ts, tc), F32),
                        pltpu.VMEM((ts, tc), F32),
                        pltpu.VMEM((ts, tc), F32),
                        pltpu.VMEM((8, tc), F32),
                        pltpu.VMEM((nb, LRU_BLOCK_W, LRU_BLOCK_W), BF16),
                        pltpu.VMEM((nb, LRU_BLOCK_W, LRU_BLOCK_W), BF16)],
        compiler_params=_params(("parallel", "arbitrary")),
        name="rglru",
    )(rest, rest, conv_w, vec(conv_b), w_rg_a, vec(b_rg_a), w_rg_x, vec(b_rg_x), vec(lru_lambda))


def _merge_kernel(att_ref, lru_ref, wa_ref, wl_ref, ga_ref, gl_ref, o_ref, wab_ref, wlb_ref):
    _load_weight(wa_ref, wab_ref)
    _load_weight(wl_ref, wlb_ref)
    bm = o_ref.shape[0]
    sub = bm // 2
    for r in range(2):
        rows = slice(r * sub, (r + 1) * sub)
        pa = jnp.dot(att_ref[rows, :], wab_ref[...], preferred_element_type=F32)
        plru = jnp.dot(lru_ref[rows, :], wlb_ref[...], preferred_element_type=F32)
        o_ref[rows, :] = (jax.nn.sigmoid(ga_ref[rows, :]) * pa
                          + jax.nn.sigmoid(gl_ref[rows, :]) * plru).astype(o_ref.dtype)


def _merge(att, lru, w_att, w_lru, rest, gate_col0, bm=512, bn=1024):
    m, k = att.shape
    n = w_att.shape[1]
    ga0 = gate_col0 // bn
    gl0 = (gate_col0 + n) // bn
    once = pl.Buffered(1)
    return pl.pallas_call(
        _merge_kernel,
        grid=(n // bn, m // bm),
        in_specs=[pl.BlockSpec((bm, k), lambda j, i: (i, 0)),
                  pl.BlockSpec((bm, k), lambda j, i: (i, 0)),
                  pl.BlockSpec((k, bn), lambda j, i: (0, j), pipeline_mode=once),
                  pl.BlockSpec((k, bn), lambda j, i: (0, j), pipeline_mode=once),
                  pl.BlockSpec((bm, bn), lambda j, i: (i, ga0 + j)),
                  pl.BlockSpec((bm, bn), lambda j, i: (i, gl0 + j))],
        out_specs=pl.BlockSpec((bm, bn), lambda j, i: (i, j)),
        out_shape=jax.ShapeDtypeStruct((m, n), BF16),
        scratch_shapes=[pltpu.VMEM((k, bn), BF16), pltpu.VMEM((k, bn), BF16)],
        compiler_params=_params(("parallel", "arbitrary")),
        name="merge",
    )(att, lru, w_att, w_lru, rest, rest)


def _outproj_kernel(a_ref, w_ref, x_ref, nw_ref, x1_ref, h2_ref, wb_ref):
    _load_weight(w_ref, wb_ref, row_axis=0)
    x1 = x_ref[...] + jnp.dot(a_ref[...], wb_ref[...], preferred_element_type=F32)
    x1_ref[...] = x1
    y = x1 * lax.rsqrt(jnp.mean(x1 * x1, axis=-1, keepdims=True) + EPS)
    h2_ref[...] = (y * nw_ref[...]).astype(h2_ref.dtype)


def _outproj(merged, w_out, x, norm2_w, bm=512):
    m, k = merged.shape
    d = w_out.shape[1]
    return pl.pallas_call(
        _outproj_kernel,
        grid=(m // bm,),
        in_specs=[pl.BlockSpec((bm, k), lambda i: (i, 0)),
                  pl.BlockSpec((k, d), lambda i: (0, 0), pipeline_mode=pl.Buffered(1)),
                  pl.BlockSpec((bm, d), lambda i: (i, 0)),
                  pl.BlockSpec((1, d), lambda i: (0, 0))],
        out_specs=[pl.BlockSpec((bm, d), lambda i: (i, 0)),
                   pl.BlockSpec((bm, d), lambda i: (i, 0))],
        out_shape=[jax.ShapeDtypeStruct((m, d), F32), jax.ShapeDtypeStruct((m, d), BF16)],
        scratch_shapes=[pltpu.VMEM((k, d), BF16)],
        compiler_params=_params(("arbitrary",)),
        name="outproj",
    )(merged, w_out, x, norm2_w.reshape(1, d))


def _ffn_up_kernel(h_ref, wg_ref, wu_ref, o_ref, wgb_ref, wub_ref):
    _load_weight(wg_ref, wgb_ref)
    _load_weight(wu_ref, wub_ref)
    splits = 4
    sub = o_ref.shape[0] // splits
    for r in range(splits):
        rows = slice(r * sub, (r + 1) * sub)
        h = h_ref[rows, :]
        g = jnp.dot(h, wgb_ref[...], preferred_element_type=F32)
        u = jnp.dot(h, wub_ref[...], preferred_element_type=F32)
        o_ref[rows, :] = (jax.nn.silu(g) * u).astype(o_ref.dtype)


def _ffn_up(h2, w_gate, w_up, bm=2048, bn=512):
    m, k = h2.shape
    n = w_gate.shape[1]
    return pl.pallas_call(
        _ffn_up_kernel,
        grid=(n // bn, m // bm),
        in_specs=[pl.BlockSpec((bm, k), lambda j, i: (i, 0)),
                  pl.BlockSpec((k, bn), lambda j, i: (0, j)),
                  pl.BlockSpec((k, bn), lambda j, i: (0, j))],
        out_specs=pl.BlockSpec((bm, bn), lambda j, i: (i, j)),
        out_shape=jax.ShapeDtypeStruct((m, n), BF16),
        scratch_shapes=[pltpu.VMEM((k, bn), BF16), pltpu.VMEM((k, bn), BF16)],
        compiler_params=_params(("parallel", "arbitrary")),
        name="ffn_up",
    )(h2, w_gate, w_up)


def _ffn_down_kernel(a_ref, w_ref, x_ref, o_ref, wb_ref):
    _load_weight(w_ref, wb_ref)
    o_ref[...] = x_ref[...] + jnp.dot(a_ref[...], wb_ref[...], preferred_element_type=F32)


def _ffn_down(act, w_down, x1, bm=512, bn=512):
    m, k = act.shape
    n = w_down.shape[1]
    return pl.pallas_call(
        _ffn_down_kernel,
        grid=(n // bn, m // bm),
        in_specs=[pl.BlockSpec((bm, k), lambda j, i: (i, 0)),
                  pl.BlockSpec((k, bn), lambda j, i: (0, j)),
                  pl.BlockSpec((bm, bn), lambda j, i: (i, j))],
        out_specs=pl.BlockSpec((bm, bn), lambda j, i: (i, j)),
        out_shape=jax.ShapeDtypeStruct((m, n), F32),
        scratch_shapes=[pltpu.VMEM((k, bn), BF16)],
        compiler_params=_params(("parallel", "arbitrary")),
        name="ffn_down",
    )(act, w_down, x1)


def _layer(x, norm1_w, w_in, q_norm_w, k_norm_w, conv_w, conv_b, w_rg_a, b_rg_a, w_rg_x, b_rg_x,
           lru_lambda, w_proj_attn, w_proj_lru, w_out, norm2_w, w_ffn_gate, w_ffn_up, w_ffn_down):
    d = x.shape[1]
    att_w = N_HEADS * HEAD_DIM
    lru_w = w_proj_lru.shape[0]

    h = _rmsnorm(x, norm1_w)
    qkv = _qkv_proj(h, w_in, q_norm_w, k_norm_w, 3 * att_w)
    rest = _rest_proj(h, w_in, 3 * att_w, 2 * lru_w + 2 * d)

    head = jnp.arange(1, N_HEADS + 1, dtype=F32)
    slopes = jnp.broadcast_to(jnp.exp2(-8.0 * head / N_HEADS)[:, None, None], (N_HEADS, 1, MOBA_BLOCK))
    att = _moba_attention(qkv, slopes)

    lru = _rglru(rest, conv_w, conv_b, w_rg_a, b_rg_a, w_rg_x, b_rg_x, lru_lambda, lru_w)

    merged = _merge(att, lru, w_proj_attn, w_proj_lru, rest, 2 * lru_w)
    x1, h2 = _outproj(merged, w_out, x, norm2_w)
    act = _ffn_up(h2, w_ffn_gate, w_ffn_up)
    return _ffn_down(act, w_ffn_down, x1)


def kernel(x, norm1_w, w_in, q_norm_w, k_norm_w, conv_w, conv_b, w_rg_a, b_rg_a, w_rg_x, b_rg_x,
           lru_lambda, w_proj_attn, w_proj_lru, w_out, norm2_w, w_ffn_gate, w_ffn_up, w_ffn_down):
    b, s, d = x.shape
    assert b == 1, "kernel handles the batch-1 prefill shape"
    y = x.reshape(s, d)
    for layer in range(norm1_w.shape[0]):
        y = _layer(y, norm1_w[layer], w_in[layer], q_norm_w[layer], k_norm_w[layer], conv_w[layer],
                   conv_b[layer], w_rg_a[layer], b_rg_a[layer], w_rg_x[layer], b_rg_x[layer],
                   lru_lambda[layer], w_proj_attn[layer], w_proj_lru[layer], w_out[layer],
                   norm2_w[layer], w_ffn_gate[layer], w_ffn_up[layer], w_ffn_down[layer])
    return y.reshape(b, s, d)
```

```python
import functools

import jax
import jax.numpy as jnp
from jax import lax
from jax.experimental import pallas as pl
from jax.experimental.pallas import tpu as pltpu

F32 = jnp.float32
BF16 = jnp.bfloat16

N_HEADS = 16
HEAD_DIM = 128
MOBA_BLOCK = 256
MOBA_TOPK = 3
LRU_BLOCK_W = 128
CONV_WIDTH = 4
LRU_C = 8.0
EPS = 1e-6
NEG_INF = -1e30
LOG2E = 1.4426950408889634
ONES_ROWS = 16

V7X_VMEM_BYTES = 64 * 1024 * 1024
VMEM_LIMIT = 56 * 1024 * 1024


def _params(semantics):
    return pltpu.CompilerParams(dimension_semantics=semantics, vmem_limit_bytes=VMEM_LIMIT)


def _rmsnorm_kernel(x_ref, w_ref, o_ref):
    x = x_ref[...]
    y = x * lax.rsqrt(jnp.mean(x * x, axis=-1, keepdims=True) + EPS)
    o_ref[...] = (y * w_ref[...]).astype(o_ref.dtype)


def _rmsnorm(x, w, tm=512):
    m, d = x.shape
    return pl.pallas_call(
        _rmsnorm_kernel,
        grid=(m // tm,),
        in_specs=[pl.BlockSpec((tm, d), lambda i: (i, 0)),
                  pl.BlockSpec((1, d), lambda i: (0, 0))],
        out_specs=pl.BlockSpec((tm, d), lambda i: (i, 0)),
        out_shape=jax.ShapeDtypeStruct((m, d), BF16),
        compiler_params=_params(("parallel",)),
        name="rmsnorm1",
    )(x, w.reshape(1, d))


def _load_weight(w_ref, wb_ref, row_axis=1):
    @pl.when(pl.program_id(row_axis) == 0)
    def _():
        wb_ref[...] = w_ref[...].astype(BF16)


def _qkv_kernel(h_ref, w_ref, qw_ref, kw_ref, o_ref, wb_ref, *, qk_tiles, row_splits):
    j = pl.program_id(0)
    _load_weight(w_ref, wb_ref)
    n_heads_tile, bm, _ = o_ref.shape
    sub = bm // row_splits

    @pl.when(j < qk_tiles)
    def _():
        nw = jnp.where(j < qk_tiles // 2, qw_ref[...], kw_ref[...])
        for r in range(row_splits):
            rows = slice(r * sub, (r + 1) * sub)
            acc = jnp.dot(h_ref[rows, :], wb_ref[...], preferred_element_type=F32)
            for hh in range(n_heads_tile):
                a = acc[:, hh * HEAD_DIM:(hh + 1) * HEAD_DIM]
                y = a * lax.rsqrt(jnp.mean(a * a, axis=-1, keepdims=True) + EPS)
                o_ref[hh, rows, :] = (y * nw).astype(o_ref.dtype)

    @pl.when(j >= qk_tiles)
    def _():
        acc = jnp.dot(h_ref[...], wb_ref[...], preferred_element_type=F32)
        for hh in range(n_heads_tile):
            o_ref[hh] = acc[:, hh * HEAD_DIM:(hh + 1) * HEAD_DIM].astype(o_ref.dtype)


def _qkv_proj(h, w_in, q_norm_w, k_norm_w, n_cols, bm=1024, bn=1024):
    m, k = h.shape
    return pl.pallas_call(
        functools.partial(_qkv_kernel, qk_tiles=(2 * N_HEADS * HEAD_DIM) // bn, row_splits=4),
        grid=(n_cols // bn, m // bm),
        in_specs=[pl.BlockSpec((bm, k), lambda j, i: (i, 0)),
                  pl.BlockSpec((k, bn), lambda j, i: (0, j)),
                  pl.BlockSpec((1, HEAD_DIM), lambda j, i: (0, 0)),
                  pl.BlockSpec((1, HEAD_DIM), lambda j, i: (0, 0))],
        out_specs=pl.BlockSpec((bn // HEAD_DIM, bm, HEAD_DIM), lambda j, i: (j, i, 0)),
        out_shape=jax.ShapeDtypeStruct((n_cols // HEAD_DIM, m, HEAD_DIM), BF16),
        scratch_shapes=[pltpu.VMEM((k, bn), BF16)],
        compiler_params=_params(("parallel", "arbitrary")),
        name="qkv_proj",
    )(h, w_in, q_norm_w.reshape(1, HEAD_DIM), k_norm_w.reshape(1, HEAD_DIM))


def _matmul_kernel(a_ref, w_ref, o_ref, wb_ref):
    _load_weight(w_ref, wb_ref)
    o_ref[...] = jnp.dot(a_ref[...], wb_ref[...], preferred_element_type=F32).astype(o_ref.dtype)


def _rest_proj(h, w_in, col0, n_cols, bm=1024, bn=1024):
    m, k = h.shape
    jb = col0 // bn
    return pl.pallas_call(
        _matmul_kernel,
        grid=(n_cols // bn, m // bm),
        in_specs=[pl.BlockSpec((bm, k), lambda j, i: (i, 0)),
                  pl.BlockSpec((k, bn), lambda j, i: (0, jb + j))],
        out_specs=pl.BlockSpec((bm, bn), lambda j, i: (i, j)),
        out_shape=jax.ShapeDtypeStruct((m, n_cols), F32),
        scratch_shapes=[pltpu.VMEM((k, bn), BF16)],
        compiler_params=_params(("parallel", "arbitrary")),
        name="rest_proj",
    )(h, w_in)


def _attn_prep_kernel(slope_ref, q_ref, k_ref, v_ref, qt_ref, vt_ref, rb_ref, kmean_ref,
                      *, n_blk, group, cols):
    blk = MOBA_BLOCK
    for jb in range(n_blk):
        rows = slice(jb * blk, (jb + 1) * blk)
        qt_ref[:, rows] = q_ref[rows, :].astype(F32).T.astype(BF16)
        c, g = divmod(jb, group)
        vt_ref[c, 0:HEAD_DIM, g * blk:(g + 1) * blk] = v_ref[rows, :].astype(F32).T.astype(BF16)
        vt_ref[c, HEAD_DIM:, g * blk:(g + 1) * blk] = jnp.ones((ONES_ROWS, blk), BF16)
        kmean_ref[jb:jb + 1, :] = jnp.sum(k_ref[rows, :].astype(F32), axis=0, keepdims=True) * (1.0 / blk)
    vt_ref[n_blk // group] = jnp.zeros(vt_ref.shape[1:], BF16)
    kmean = kmean_ref[...].astype(BF16)
    slope = slope_ref[:, 0:1]
    row = lax.broadcasted_iota(jnp.int32, (n_blk, cols), 0)
    col = lax.broadcasted_iota(jnp.int32, (n_blk, cols), 1)
    for qc in range(qt_ref.shape[1] // cols):
        csl = slice(qc * cols, (qc + 1) * cols)
        gate = jnp.dot(kmean, qt_ref[:, csl], preferred_element_type=F32)
        qblk = lax.shift_right_logical(col + qc * cols, blk.bit_length() - 1)
        g = jnp.where(row < qblk, gate, NEG_INF)
        bias = jnp.full((n_blk, cols), NEG_INF, F32)
        for r in range(MOBA_TOPK):
            mx = jnp.max(g, axis=0, keepdims=True)
            idx = jnp.min(jnp.where(g == mx, row, n_blk), axis=0, keepdims=True)
            pick = row == idx
            bias = jnp.where(pick, jnp.where(qblk > r, 0.0, NEG_INF), bias)
            g = jnp.where(pick, -jnp.inf, g)
        past = bias - slope * ((qblk - row) * blk).astype(F32)
        rb = jnp.where(row == qblk, 0.0, jnp.where(row < qblk, past, NEG_INF))
        rb_ref[:, csl] = rb * LOG2E


def _attn_kernel(slope_ref, qt_ref, k_ref, vt_ref, rb_ref, rbp_ref, o_ref, *scratch,
                 group, n_blk, heads, nq):
    u_refs, m_refs, dmat_refs, acc_refs, p_refs = (scratch[n * heads:(n + 1) * heads] for n in range(5))
    blk = MOBA_BLOCK
    cb = group * blk
    lg = group.bit_length() - 1
    i = pl.program_id(1)

    @pl.when(i == 0)
    def _per_head_setup():
        kk = lax.broadcasted_iota(jnp.int32, (blk, blk), 0)
        qq = lax.broadcasted_iota(jnp.int32, (blk, blk), 1)
        for hh in range(heads):
            d = slope_ref[hh][:, 0:1] * (qq - kk).astype(F32) * LOG2E
            dmat_refs[hh][0] = d
            dmat_refs[hh][1] = jnp.where(kk <= qq, d, -NEG_INF)
            m_refs[hh][...] = jnp.zeros_like(m_refs[hh])
            p_refs[hh][...] = jnp.zeros_like(p_refs[hh])

    k2 = (HEAD_DIM ** -0.5) * LOG2E
    n1 = jnp.where(i < n_blk // nq, lax.shift_right_logical(nq * (i + 1) + group - 1, lg), 0)
    n2 = lax.shift_right_logical(nq * i + group - 1, lg)
    common = jnp.minimum(n1, n2)
    m_prev = [m_refs[hh][...] for hh in range(heads)]
    for hh in range(heads):
        acc_refs[hh][...] = jnp.zeros_like(acc_refs[hh])

    def pass1(hh, c, mx):
        r0 = pl.multiple_of(c * cb, cb)
        k_chunk = k_ref[hh, pl.ds(r0, cb), :]
        s = jnp.dot(k_chunk, qt_ref[hh], preferred_element_type=F32)
        for g in range(group):
            j = c * group + g
            rows = slice(g * blk, (g + 1) * blk)
            cands = []
            for qb in range(nq):
                cols = slice(qb * blk, (qb + 1) * blk)
                own = (j == nq * i + qb).astype(jnp.int32)
                u = s[rows, cols] * k2 - dmat_refs[hh][own]
                u_refs[hh][c, rows, cols] = u
                cands.append(jnp.max(u, axis=0, keepdims=True))
            mx = jnp.maximum(mx, jnp.concatenate(cands, axis=1) + rb_ref[hh, pl.ds(j, 1), :])
        return mx

    def probs(hh, c):
        for g in range(group):
            j = c * group + g
            rows = slice(g * blk, (g + 1) * blk)
            p = jnp.exp2(u_refs[hh][c, rows, :] + (rbp_ref[hh, pl.ds(j, 1), :] - m_prev[hh]))
            p_refs[hh][rows, :] = p.astype(BF16)

    def pv(hh, c):
        n_chunk = vt_ref.shape[1] - 1
        acc_refs[hh][...] += jnp.dot(vt_ref[hh, jnp.where(c == 0, n_chunk, c - 1)], p_refs[hh][...],
                                     preferred_element_type=F32)

    def only1(c, mxs):
        return tuple(pass1(hh, c, mxs[hh]) for hh in range(heads))

    def only2(c, carry):
        for hh in range(heads):
            pv(hh, c)
        for hh in range(heads):
            probs(hh, c)
        return carry

    def both(c, mxs):
        for hh in range(heads):
            pv(hh, c)
        out = []
        for hh in range(heads):
            probs(hh, c)
            out.append(pass1(hh, c, mxs[hh]))
        return tuple(out)

    def both_twice(c2, mxs):
        return both(2 * c2 + 1, both(2 * c2, mxs))

    pairs = lax.shift_right_logical(common, 1)
    mxs = lax.fori_loop(0, pairs, both_twice,
                        tuple(jnp.full((1, nq * blk), -jnp.inf, F32) for _ in range(heads)))
    mxs = lax.fori_loop(2 * pairs, common, both, mxs)
    mxs = lax.fori_loop(common, n1, only1, mxs)
    lax.fori_loop(common, n2, only2, 0)
    for hh in range(heads):
        pv(hh, n2)
        m_refs[hh][...] = mxs[hh]

    @pl.when(i > 0)
    def _():
        for hh in range(heads):
            acc = acc_refs[hh][...]
            o = acc[0:HEAD_DIM, :] / acc[HEAD_DIM:HEAD_DIM + 1, :]
            for qb in range(nq):
                o_ref[qb * blk:(qb + 1) * blk, hh * HEAD_DIM:(hh + 1) * HEAD_DIM] = (
                    o[:, qb * blk:(qb + 1) * blk].T.astype(o_ref.dtype))


def _moba_attention(qkv, slopes, group=4, cols=2048, heads=2, nq=2):
    s = qkv.shape[1]
    blk = MOBA_BLOCK
    n_blk = s // blk
    n_chunk = n_blk // group
    vt_rows = HEAD_DIM + ONES_ROWS
    qt, vt, rb = pl.pallas_call(
        functools.partial(_attn_prep_kernel, n_blk=n_blk, group=group, cols=cols),
        grid=(N_HEADS,),
        in_specs=[pl.BlockSpec((None, 1, blk), lambda h: (h, 0, 0)),
                  pl.BlockSpec((None, s, HEAD_DIM), lambda h: (h, 0, 0)),
                  pl.BlockSpec((None, s, HEAD_DIM), lambda h: (N_HEADS + h, 0, 0)),
                  pl.BlockSpec((None, s, HEAD_DIM), lambda h: (2 * N_HEADS + h, 0, 0))],
        out_specs=[pl.BlockSpec((None, HEAD_DIM, s), lambda h: (h, 0, 0)),
                   pl.BlockSpec((None, n_chunk + 1, vt_rows, group * blk), lambda h: (h, 0, 0, 0)),
                   pl.BlockSpec((None, n_blk, s), lambda h: (h, 0, 0))],
        out_shape=[jax.ShapeDtypeStruct((N_HEADS, HEAD_DIM, s), BF16),
                   jax.ShapeDtypeStruct((N_HEADS, n_chunk + 1, vt_rows, group * blk), BF16),
                   jax.ShapeDtypeStruct((N_HEADS, n_blk, s), F32)],
        scratch_shapes=[pltpu.VMEM((n_blk, HEAD_DIM), F32)],
        compiler_params=_params(("parallel",)),
        name="moba_prep",
    )(slopes, qkv, qkv, qkv)
    n_tiles = n_blk // nq
    last = n_tiles - 1
    qw = nq * blk
    hw = heads * HEAD_DIM
    k_blk0 = N_HEADS // heads
    once = pl.Buffered(1)
    per_head = lambda shape: [pltpu.VMEM(shape, F32) for _ in range(heads)]
    return pl.pallas_call(
        functools.partial(_attn_kernel, group=group, n_blk=n_blk, heads=heads, nq=nq),
        grid=(N_HEADS // heads, n_tiles + 1),
        in_specs=[pl.BlockSpec((heads, 1, blk), lambda h, i: (h, 0, 0)),
                  pl.BlockSpec((heads, HEAD_DIM, qw), lambda h, i: (h, 0, jnp.minimum(i, last))),
                  pl.BlockSpec((heads, s, HEAD_DIM), lambda h, i: (k_blk0 + h, 0, 0)),
                  pl.BlockSpec((heads, n_chunk + 1, vt_rows, group * blk), lambda h, i: (h, 0, 0, 0),
                               pipeline_mode=once),
                  pl.BlockSpec((heads, n_blk, qw), lambda h, i: (h, 0, jnp.minimum(i, last))),
                  pl.BlockSpec((heads, n_blk, qw), lambda h, i: (h, 0, jnp.maximum(i - 1, 0)))],
        out_specs=pl.BlockSpec((qw, hw), lambda h, i: (jnp.maximum(i - 1, 0), h)),
        out_shape=jax.ShapeDtypeStruct((s, N_HEADS * HEAD_DIM), BF16),
        scratch_shapes=(per_head((n_chunk, group * blk, qw))
                        + per_head((1, qw))
                        + per_head((2, blk, blk))
                        + per_head((vt_rows, qw))
                        + [pltpu.VMEM((group * blk, qw), BF16) for _ in range(heads)]),
        compiler_params=_params(("parallel", "arbitrary")),
        name="moba_attention",
    )(slopes, qt, qkv, vt, rb, rb)


def _lru_kernel(xr_ref, yr_ref, cw_ref, cb_ref, wa_ref, ba_ref, wx_ref, bx_ref, lam_ref, o_ref,
                xs_ref, xp_ref, hx_ref, a_ref, b_ref, hl_ref, ac_ref, h_ref, wab_ref, wxb_ref, *, ts, tc):
    t = pl.program_id(1)
    lane_tiles = tc // 128
    ng = ts // 8
    pitch = ng + 1
    lead = 8
    hist = CONV_WIDTH - 1
    row = lax.broadcasted_iota(jnp.int32, (8, tc), 0)

    @pl.when(t == 0)
    def _():
        hx_ref[...] = jnp.zeros_like(hx_ref)
        h_ref[...] = jnp.zeros_like(h_ref)
        wab_ref[...] = wa_ref[...].astype(BF16)
        wxb_ref[...] = wx_ref[...].astype(BF16)

    x = xr_ref[...]
    for lt in range(lane_tiles):
        for sg in range(8):
            xs_ref[lt, sg * pitch:sg * pitch + ng, :] = x[sg * ng:(sg + 1) * ng, lt * 128:(lt + 1) * 128]

    def gather(g, c):
        r0 = pl.multiple_of((lead + g) * 8, 8)
        xp_ref[pl.ds(r0, 8), :] = jnp.concatenate(
            [xs_ref[lt, pl.ds(g, 8, stride=pitch), :] for lt in range(lane_tiles)], axis=1)
        return c

    lax.fori_loop(0, ng, gather, 0, unroll=8)
    for k in range(1, hist + 1):
        tail = xp_ref[(lead + ng - k) * 8:(lead + ng - k + 1) * 8, :]
        before_tile = jnp.broadcast_to(hx_ref[k - 1][7:8, :], (8, tc))
        xp_ref[(lead - k) * 8:(lead - k + 1) * 8, :] = jnp.where(row == 0, before_tile, pltpu.roll(tail, 1, 0))
        hx_ref[k - 1] = tail

    cw = cw_ref[...]
    u = cb_ref[...]
    for tap in range(CONV_WIDTH):
        off = (lead - hist + tap) * 8
        u = u + xp_ref[off:off + ts, :] * cw[tap:tap + 1, :]

    ub = u.astype(BF16)
    ga, gx = [], []
    for n in range(tc // LRU_BLOCK_W):
        un = ub[:, n * LRU_BLOCK_W:(n + 1) * LRU_BLOCK_W]
        ga.append(jnp.dot(un, wab_ref[n], preferred_element_type=F32))
        gx.append(jnp.dot(un, wxb_ref[n], preferred_element_type=F32))
    r = jax.nn.sigmoid(jnp.concatenate(ga, axis=1) + ba_ref[...])
    ig = jax.nn.sigmoid(jnp.concatenate(gx, axis=1) + bx_ref[...])
    log_a = -LRU_C * r * jax.nn.softplus(-lam_ref[...])
    a = jnp.exp(log_a)
    a_ref[...] = a
    one_minus_a2 = -jnp.tanh(log_a) * (a * a + 1.0)
    mult = jnp.where(one_minus_a2 == 0.0, 0.0, one_minus_a2 * lax.rsqrt(one_minus_a2))
    b_ref[...] = mult * ig * u

    half = ng // 2

    def local(g, carry):
        out = []
        for hf in range(2):
            hloc, acum = carry[hf]
            r0 = pl.multiple_of((g + hf * half) * 8, 8)
            av = a_ref[pl.ds(r0, 8), :]
            hloc = av * hloc + b_ref[pl.ds(r0, 8), :]
            acum = av * acum
            hl_ref[pl.ds(r0, 8), :] = hloc
            ac_ref[pl.ds(r0, 8), :] = acum
            out.append((hloc, acum))
        return tuple(out)

    start = (jnp.zeros((8, tc), F32), jnp.ones((8, tc), F32))
    (q0, p0), (q1, p1) = lax.fori_loop(0, half, local, (start, start), unroll=8)
    p = p1 * p0
    q = p1 * q0 + q1
    for d in (1, 2, 4):
        keep = row >= d
        p_sh = pltpu.roll(p, d, 0)
        q_sh = pltpu.roll(q, d, 0)
        q = jnp.where(keep, p * q_sh + q, q)
        p = jnp.where(keep, p * p_sh, p)
    h_in = h_ref[...]
    after = p * h_in + q
    enter = jnp.where(row == 0, h_in, pltpu.roll(after, 1, 0))
    h_ref[...] = jnp.broadcast_to(after[7:8, :], (8, tc))

    enter_half = (enter, p0 * enter + q0)

    def scatter(g, c):
        for hf in range(2):
            gg = g + hf * half
            r0 = pl.multiple_of(gg * 8, 8)
            hv = hl_ref[pl.ds(r0, 8), :] + ac_ref[pl.ds(r0, 8), :] * enter_half[hf]
            for lt in range(lane_tiles):
                xs_ref[lt, pl.ds(gg, 8, stride=pitch), :] = hv[:, lt * 128:(lt + 1) * 128]
        return c

    lax.fori_loop(0, half, scatter, 0, unroll=4)
    h_all = jnp.concatenate(
        [jnp.concatenate([xs_ref[lt, sg * pitch:sg * pitch + ng, :] for sg in range(8)], axis=0)
         for lt in range(lane_tiles)], axis=1)
    o_ref[...] = (h_all * jax.nn.gelu(yr_ref[...])).astype(o_ref.dtype)


def _rglru(rest, conv_w, conv_b, w_rg_a, b_rg_a, w_rg_x, b_rg_x, lru_lambda, width, ts=512, tc=512):
    s = rest.shape[0]
    nct = width // tc
    nb = tc // LRU_BLOCK_W
    vec = lambda v: v.reshape(1, width)
    vspec = pl.BlockSpec((1, tc), lambda c, t: (0, c))
    wspec = pl.BlockSpec((nb, LRU_BLOCK_W, LRU_BLOCK_W), lambda c, t: (c, 0, 0))
    return pl.pallas_call(
        functools.partial(_lru_kernel, ts=ts, tc=tc),
        grid=(nct, s // ts),
        in_specs=[pl.BlockSpec((ts, tc), lambda c, t: (t, c)),
                  pl.BlockSpec((ts, tc), lambda c, t: (t, nct + c)),
                  pl.BlockSpec((CONV_WIDTH, tc), lambda c, t: (0, c)),
                  vspec, wspec, vspec, wspec, vspec, vspec],
        out_specs=pl.BlockSpec((ts, tc), lambda c, t: (t, c)),
        out_shape=jax.ShapeDtypeStruct((s, width), BF16),
        scratch_shapes=[pltpu.VMEM((tc // 128, ts + 8, 128), F32),
                        pltpu.VMEM((ts + 64, tc), F32),
                        pltpu.VMEM((CONV_WIDTH - 1, 8, tc), F32),
                        pltpu.VMEM((ts, tc), F32),
                        pltpu.VMEM((ts, tc), F32),
                        pltpu.VMEM((ts, tc), F32),
                        pltpu.VMEM((ts, tc), F32),
                        pltpu.VMEM((8, tc), F32),
                        pltpu.VMEM((nb, LRU_BLOCK_W, LRU_BLOCK_W), BF16),
                        pltpu.VMEM((nb, LRU_BLOCK_W, LRU_BLOCK_W), BF16)],
        compiler_params=_params(("parallel", "arbitrary")),
        name="rglru",
    )(rest, rest, conv_w, vec(conv_b), w_rg_a, vec(b_rg_a), w_rg_x, vec(b_rg_x), vec(lru_lambda))


def _merge_kernel(att_ref, lru_ref, wa_ref, wl_ref, ga_ref, gl_ref, o_ref, wab_ref, wlb_ref):
    _load_weight(wa_ref, wab_ref)
    _load_weight(wl_ref, wlb_ref)
    bm = o_ref.shape[0]
    sub = bm // 2
    for r in range(2):
        rows = slice(r * sub, (r + 1) * sub)
        pa = jnp.dot(att_ref[rows, :], wab_ref[...], preferred_element_type=F32)
        plru = jnp.dot(lru_ref[rows, :], wlb_ref[...], preferred_element_type=F32)
        o_ref[rows, :] = (jax.nn.sigmoid(ga_ref[rows, :]) * pa
                          + jax.nn.sigmoid(gl_ref[rows, :]) * plru).astype(o_ref.dtype)


def _merge(att, lru, w_att, w_lru, rest, gate_col0, bm=512, bn=1024):
    m, k = att.shape
    n = w_att.shape[1]
    ga0 = gate_col0 // bn
    gl0 = (gate_col0 + n) // bn
    once = pl.Buffered(1)
    return pl.pallas_call(
        _merge_kernel,
        grid=(n // bn, m // bm),
        in_specs=[pl.BlockSpec((bm, k), lambda j, i: (i, 0)),
                  pl.BlockSpec((bm, k), lambda j, i: (i, 0)),
                  pl.BlockSpec((k, bn), lambda j, i: (0, j), pipeline_mode=once),
                  pl.BlockSpec((k, bn), lambda j, i: (0, j), pipeline_mode=once),
                  pl.BlockSpec((bm, bn), lambda j, i: (i, ga0 + j)),
                  pl.BlockSpec((bm, bn), lambda j, i: (i, gl0 + j))],
        out_specs=pl.BlockSpec((bm, bn), lambda j, i: (i, j)),
        out_shape=jax.ShapeDtypeStruct((m, n), BF16),
        scratch_shapes=[pltpu.VMEM((k, bn), BF16), pltpu.VMEM((k, bn), BF16)],
        compiler_params=_params(("parallel", "arbitrary")),
        name="merge",
    )(att, lru, w_att, w_lru, rest, rest)


def _outproj_kernel(a_ref, w_ref, x_ref, nw_ref, x1_ref, h2_ref, wb_ref):
    _load_weight(w_ref, wb_ref, row_axis=0)
    x1 = x_ref[...] + jnp.dot(a_ref[...], wb_ref[...], preferred_element_type=F32)
    x1_ref[...] = x1
    y = x1 * lax.rsqrt(jnp.mean(x1 * x1, axis=-1, keepdims=True) + EPS)
    h2_ref[...] = (y * nw_ref[...]).astype(h2_ref.dtype)


def _outproj(merged, w_out, x, norm2_w, bm=512):
    m, k = merged.shape
    d = w_out.shape[1]
    return pl.pallas_call(
        _outproj_kernel,
        grid=(m // bm,),
        in_specs=[pl.BlockSpec((bm, k), lambda i: (i, 0)),
                  pl.BlockSpec((k, d), lambda i: (0, 0), pipeline_mode=pl.Buffered(1)),
                  pl.BlockSpec((bm, d), lambda i: (i, 0)),
                  pl.BlockSpec((1, d), lambda i: (0, 0))],
        out_specs=[pl.BlockSpec((bm, d), lambda i: (i, 0)),
                   pl.BlockSpec((bm, d), lambda i: (i, 0))],
        out_shape=[jax.ShapeDtypeStruct((m, d), F32), jax.ShapeDtypeStruct((m, d), BF16)],
        scratch_shapes=[pltpu.VMEM((k, d), BF16)],
        compiler_params=_params(("arbitrary",)),
        name="outproj",
    )(merged, w_out, x, norm2_w.reshape(1, d))


def _ffn_up_kernel(h_ref, wg_ref, wu_ref, wd_ref, o_ref, wdb_ref, wgb_ref, wub_ref):
    _load_weight(wg_ref, wgb_ref)
    _load_weight(wu_ref, wub_ref)
    _load_weight(wd_ref, wdb_ref)
    splits = 4
    sub = o_ref.shape[0] // splits
    for r in range(splits):
        rows = slice(r * sub, (r + 1) * sub)
        h = h_ref[rows, :]
        g = jnp.dot(h, wgb_ref[...], preferred_element_type=F32)
        u = jnp.dot(h, wub_ref[...], preferred_element_type=F32)
        o_ref[rows, :] = (jax.nn.silu(g) * u).astype(o_ref.dtype)


def _ffn_up(h2, w_gate, w_up, w_down, bm=2048, bn=512):
    m, k = h2.shape
    n = w_gate.shape[1]
    d = w_down.shape[1]
    return pl.pallas_call(
        _ffn_up_kernel,
        grid=(n // bn, m // bm),
        in_specs=[pl.BlockSpec((bm, k), lambda j, i: (i, 0)),
                  pl.BlockSpec((k, bn), lambda j, i: (0, j)),
                  pl.BlockSpec((k, bn), lambda j, i: (0, j)),
                  pl.BlockSpec((bn, d), lambda j, i: (j, 0))],
        out_specs=[pl.BlockSpec((bm, bn), lambda j, i: (i, j)),
                   pl.BlockSpec((bn, d), lambda j, i: (j, 0))],
        out_shape=[jax.ShapeDtypeStruct((m, n), BF16), jax.ShapeDtypeStruct((n, d), BF16)],
        scratch_shapes=[pltpu.VMEM((k, bn), BF16), pltpu.VMEM((k, bn), BF16)],
        compiler_params=_params(("parallel", "arbitrary")),
        name="ffn_up",
    )(h2, w_gate, w_up, w_down)


def _ffn_down_kernel(a_ref, w_ref, x_ref, o_ref):
    o_ref[...] = x_ref[...] + jnp.dot(a_ref[...], w_ref[...], preferred_element_type=F32)


def _ffn_down(act, w_down_bf16, x1, bm=512, bn=1024):
    m, k = act.shape
    n = w_down_bf16.shape[1]
    return pl.pallas_call(
        _ffn_down_kernel,
        grid=(n // bn, m // bm),
        in_specs=[pl.BlockSpec((bm, k), lambda j, i: (i, 0)),
                  pl.BlockSpec((k, bn), lambda j, i: (0, j)),
                  pl.BlockSpec((bm, bn), lambda j, i: (i, j))],
        out_specs=pl.BlockSpec((bm, bn), lambda j, i: (i, j)),
        out_shape=jax.ShapeDtypeStruct((m, n), F32),
        compiler_params=_params(("parallel", "arbitrary")),
        name="ffn_down",
    )(act, w_down_bf16, x1)


def _layer(x, norm1_w, w_in, q_norm_w, k_norm_w, conv_w, conv_b, w_rg_a, b_rg_a, w_rg_x, b_rg_x,
           lru_lambda, w_proj_attn, w_proj_lru, w_out, norm2_w, w_ffn_gate, w_ffn_up, w_ffn_down):
    d = x.shape[1]
    att_w = N_HEADS * HEAD_DIM
    lru_w = w_proj_lru.shape[0]

    h = _rmsnorm(x, norm1_w)
    qkv = _qkv_proj(h, w_in, q_norm_w, k_norm_w, 3 * att_w)
    rest = _rest_proj(h, w_in, 3 * att_w, 2 * lru_w + 2 * d)

    head = jnp.arange(1, N_HEADS + 1, dtype=F32)
    slopes = jnp.broadcast_to(jnp.exp2(-8.0 * head / N_HEADS)[:, None, None], (N_HEADS, 1, MOBA_BLOCK))
    att = _moba_attention(qkv, slopes)

    lru = _rglru(rest, conv_w, conv_b, w_rg_a, b_rg_a, w_rg_x, b_rg_x, lru_lambda, lru_w)

    merged = _merge(att, lru, w_proj_attn, w_proj_lru, rest, 2 * lru_w)
    x1, h2 = _outproj(merged, w_out, x, norm2_w)
    act, w_down_bf16 = _ffn_up(h2, w_ffn_gate, w_ffn_up, w_ffn_down)
    return _ffn_down(act, w_down_bf16, x1)


def kernel(x, norm1_w, w_in, q_norm_w, k_norm_w, conv_w, conv_b, w_rg_a, b_rg_a, w_rg_x, b_rg_x,
           lru_lambda, w_proj_attn, w_proj_lru, w_out, norm2_w, w_ffn_gate, w_ffn_up, w_ffn_down):
    b, s, d = x.shape
    assert b == 1, "kernel handles the batch-1 prefill shape"
    y = x.reshape(s, d)
    for layer in range(norm1_w.shape[0]):
        y = _layer(y, norm1_w[layer], w_in[layer], q_norm_w[layer], k_norm_w[layer], conv_w[layer],
                   conv_b[layer], w_rg_a[layer], b_rg_a[layer], w_rg_x[layer], b_rg_x[layer],
                   lru_lambda[layer], w_proj_attn[layer], w_proj_lru[layer], w_out[layer],
                   norm2_w[layer], w_ffn_gate[layer], w_ffn_up[layer], w_ffn_down[layer])
    return y.reshape(b, s, d)
```

```python
import functools

import jax
import jax.numpy as jnp
from jax import lax
from jax.experimental import pallas as pl
from jax.experimental.pallas import tpu as pltpu

F32 = jnp.float32
BF16 = jnp.bfloat16

N_HEADS = 16
HEAD_DIM = 128
MOBA_BLOCK = 256
MOBA_TOPK = 3
LRU_BLOCK_W = 128
CONV_WIDTH = 4
LRU_C = 8.0
EPS = 1e-6
NEG_INF = -1e30
LOG2E = 1.4426950408889634
ONES_ROWS = 16

V7X_VMEM_BYTES = 64 * 1024 * 1024
VMEM_LIMIT = 56 * 1024 * 1024


def _params(semantics):
    return pltpu.CompilerParams(dimension_semantics=semantics, vmem_limit_bytes=VMEM_LIMIT)


def _rmsnorm_kernel(x_ref, w_ref, o_ref):
    x = x_ref[...]
    y = x * lax.rsqrt(jnp.mean(x * x, axis=-1, keepdims=True) + EPS)
    o_ref[...] = (y * w_ref[...]).astype(o_ref.dtype)


def _rmsnorm(x, w, tm=512):
    m, d = x.shape
    return pl.pallas_call(
        _rmsnorm_kernel,
        grid=(m // tm,),
        in_specs=[pl.BlockSpec((tm, d), lambda i: (i, 0)),
                  pl.BlockSpec((1, d), lambda i: (0, 0))],
        out_specs=pl.BlockSpec((tm, d), lambda i: (i, 0)),
        out_shape=jax.ShapeDtypeStruct((m, d), BF16),
        compiler_params=_params(("parallel",)),
        name="rmsnorm1",
    )(x, w.reshape(1, d))


def _load_weight(w_ref, wb_ref):
    @pl.when(pl.program_id(1) == 0)
    def _():
        wb_ref[...] = w_ref[...].astype(BF16)


def _qkv_kernel(h_ref, w_ref, qw_ref, kw_ref, o_ref, wb_ref, *, qk_tiles, row_splits):
    j = pl.program_id(0)
    _load_weight(w_ref, wb_ref)
    n_heads_tile, bm, _ = o_ref.shape
    sub = bm // row_splits

    @pl.when(j < qk_tiles)
    def _():
        nw = jnp.where(j < qk_tiles // 2, qw_ref[...], kw_ref[...])
        for r in range(row_splits):
            rows = slice(r * sub, (r + 1) * sub)
            acc = jnp.dot(h_ref[rows, :], wb_ref[...], preferred_element_type=F32)
            for hh in range(n_heads_tile):
                a = acc[:, hh * HEAD_DIM:(hh + 1) * HEAD_DIM]
                y = a * lax.rsqrt(jnp.mean(a * a, axis=-1, keepdims=True) + EPS)
                o_ref[hh, rows, :] = (y * nw).astype(o_ref.dtype)

    @pl.when(j >= qk_tiles)
    def _():
        acc = jnp.dot(h_ref[...], wb_ref[...], preferred_element_type=F32)
        for hh in range(n_heads_tile):
            o_ref[hh] = acc[:, hh * HEAD_DIM:(hh + 1) * HEAD_DIM].astype(o_ref.dtype)


def _qkv_proj(h, w_in, q_norm_w, k_norm_w, n_cols, bm=1024, bn=1024):
    m, k = h.shape
    return pl.pallas_call(
        functools.partial(_qkv_kernel, qk_tiles=(2 * N_HEADS * HEAD_DIM) // bn, row_splits=4),
        grid=(n_cols // bn, m // bm),
        in_specs=[pl.BlockSpec((bm, k), lambda j, i: (i, 0)),
                  pl.BlockSpec((k, bn), lambda j, i: (0, j)),
                  pl.BlockSpec((1, HEAD_DIM), lambda j, i: (0, 0)),
                  pl.BlockSpec((1, HEAD_DIM), lambda j, i: (0, 0))],
        out_specs=pl.BlockSpec((bn // HEAD_DIM, bm, HEAD_DIM), lambda j, i: (j, i, 0)),
        out_shape=jax.ShapeDtypeStruct((n_cols // HEAD_DIM, m, HEAD_DIM), BF16),
        scratch_shapes=[pltpu.VMEM((k, bn), BF16)],
        compiler_params=_params(("parallel", "arbitrary")),
        name="qkv_proj",
    )(h, w_in, q_norm_w.reshape(1, HEAD_DIM), k_norm_w.reshape(1, HEAD_DIM))


def _rest_kernel(a_ref, w_ref, *refs, n_side):
    side_in, o_ref, side_out, wb_ref = refs[:n_side], refs[n_side], refs[n_side + 1:-1], refs[-1]
    _load_weight(w_ref, wb_ref)
    for s_ref, c_ref in zip(side_in, side_out):
        c_ref[...] = s_ref[...].astype(BF16)
    o_ref[...] = jnp.dot(a_ref[...], wb_ref[...], preferred_element_type=F32).astype(o_ref.dtype)


def _rest_proj(h, w_in, col0, n_cols, side_weights, bm=1024, bn=1024):
    m, k = h.shape
    jb = col0 // bn
    nj = n_cols // bn
    ni = m // bm
    slab = lambda w: pl.BlockSpec((w.shape[0] // (nj * ni), w.shape[1]), lambda j, i: (j * ni + i, 0))
    return pl.pallas_call(
        functools.partial(_rest_kernel, n_side=len(side_weights)),
        grid=(nj, m // bm),
        in_specs=[pl.BlockSpec((bm, k), lambda j, i: (i, 0)),
                  pl.BlockSpec((k, bn), lambda j, i: (0, jb + j))] + [slab(w) for w in side_weights],
        out_specs=[pl.BlockSpec((bm, bn), lambda j, i: (i, j))] + [slab(w) for w in side_weights],
        out_shape=[jax.ShapeDtypeStruct((m, n_cols), F32)]
                  + [jax.ShapeDtypeStruct(w.shape, BF16) for w in side_weights],
        scratch_shapes=[pltpu.VMEM((k, bn), BF16)],
        compiler_params=_params(("parallel", "arbitrary")),
        name="rest_proj",
    )(h, w_in, *side_weights)


def _attn_prep_kernel(slope_ref, q_ref, k_ref, v_ref, qt_ref, vt_ref, rb_ref, kmean_ref,
                      *, n_blk, group, cols):
    blk = MOBA_BLOCK
    for jb in range(n_blk):
        rows = slice(jb * blk, (jb + 1) * blk)
        qt_ref[:, rows] = q_ref[rows, :].astype(F32).T.astype(BF16)
        c, g = divmod(jb, group)
        vt_ref[c, 0:HEAD_DIM, g * blk:(g + 1) * blk] = v_ref[rows, :].astype(F32).T.astype(BF16)
        vt_ref[c, HEAD_DIM:, g * blk:(g + 1) * blk] = jnp.ones((ONES_ROWS, blk), BF16)
        kmean_ref[jb:jb + 1, :] = jnp.sum(k_ref[rows, :].astype(F32), axis=0, keepdims=True) * (1.0 / blk)
    vt_ref[n_blk // group] = jnp.zeros(vt_ref.shape[1:], BF16)
    kmean = kmean_ref[...].astype(BF16)
    slope = slope_ref[:, 0:1]
    row = lax.broadcasted_iota(jnp.int32, (n_blk, cols), 0)
    col = lax.broadcasted_iota(jnp.int32, (n_blk, cols), 1)
    for qc in range(qt_ref.shape[1] // cols):
        csl = slice(qc * cols, (qc + 1) * cols)
        gate = jnp.dot(kmean, qt_ref[:, csl], preferred_element_type=F32)
        qblk = lax.shift_right_logical(col + qc * cols, blk.bit_length() - 1)
        g = jnp.where(row < qblk, gate, NEG_INF)
        bias = jnp.full((n_blk, cols), NEG_INF, F32)
        for r in range(MOBA_TOPK):
            mx = jnp.max(g, axis=0, keepdims=True)
            idx = jnp.min(jnp.where(g == mx, row, n_blk), axis=0, keepdims=True)
            pick = row == idx
            bias = jnp.where(pick, jnp.where(qblk > r, 0.0, NEG_INF), bias)
            g = jnp.where(pick, -jnp.inf, g)
        past = bias - slope * ((qblk - row) * blk).astype(F32)
        rb = jnp.where(row == qblk, 0.0, jnp.where(row < qblk, past, NEG_INF))
        rb_ref[:, csl] = rb * LOG2E


def _attn_kernel(slope_ref, qt_ref, k_ref, vt_ref, rb_ref, rbp_ref, o_ref, *scratch,
                 group, n_blk, heads, nq):
    u_refs, m_refs, dmat_refs, acc_refs, p_refs = (scratch[n * heads:(n + 1) * heads] for n in range(5))
    blk = MOBA_BLOCK
    cb = group * blk
    lg = group.bit_length() - 1
    i = pl.program_id(1)

    @pl.when(i == 0)
    def _per_head_setup():
        kk = lax.broadcasted_iota(jnp.int32, (blk, blk), 0)
        qq = lax.broadcasted_iota(jnp.int32, (blk, blk), 1)
        for hh in range(heads):
            d = slope_ref[hh][:, 0:1] * (qq - kk).astype(F32) * LOG2E
            dmat_refs[hh][0] = d
            dmat_refs[hh][1] = jnp.where(kk <= qq, d, -NEG_INF)
            m_refs[hh][...] = jnp.zeros_like(m_refs[hh])
            p_refs[hh][...] = jnp.zeros_like(p_refs[hh])

    k2 = (HEAD_DIM ** -0.5) * LOG2E
    n1 = jnp.where(i < n_blk // nq, lax.shift_right_logical(nq * (i + 1) + group - 1, lg), 0)
    n2 = lax.shift_right_logical(nq * i + group - 1, lg)
    common = jnp.minimum(n1, n2)
    m_prev = [m_refs[hh][...] for hh in range(heads)]
    for hh in range(heads):
        acc_refs[hh][...] = jnp.zeros_like(acc_refs[hh])

    def pass1(hh, c, mx):
        r0 = pl.multiple_of(c * cb, cb)
        k_chunk = k_ref[hh, pl.ds(r0, cb), :]
        s = jnp.dot(k_chunk, qt_ref[hh], preferred_element_type=F32)
        for g in range(group):
            j = c * group + g
            rows = slice(g * blk, (g + 1) * blk)
            cands = []
            for qb in range(nq):
                cols = slice(qb * blk, (qb + 1) * blk)
                own = (j == nq * i + qb).astype(jnp.int32)
                u = s[rows, cols] * k2 - dmat_refs[hh][own]
                u_refs[hh][c, rows, cols] = u
                cands.append(jnp.max(u, axis=0, keepdims=True))
            mx = jnp.maximum(mx, jnp.concatenate(cands, axis=1) + rb_ref[hh, pl.ds(j, 1), :])
        return mx

    def probs(hh, c):
        for g in range(group):
            j = c * group + g
            rows = slice(g * blk, (g + 1) * blk)
            p = jnp.exp2(u_refs[hh][c, rows, :] + (rbp_ref[hh, pl.ds(j, 1), :] - m_prev[hh]))
            p_refs[hh][rows, :] = p.astype(BF16)

    def pv(hh, c):
        n_chunk = vt_ref.shape[1] - 1
        acc_refs[hh][...] += jnp.dot(vt_ref[hh, jnp.where(c == 0, n_chunk, c - 1)], p_refs[hh][...],
                                     preferred_element_type=F32)

    def only1(c, mxs):
        return tuple(pass1(hh, c, mxs[hh]) for hh in range(heads))

    def only2(c, carry):
        for hh in range(heads):
            pv(hh, c)
        for hh in range(heads):
            probs(hh, c)
        return carry

    def both(c, mxs):
        for hh in range(heads):
            pv(hh, c)
        out = []
        for hh in range(heads):
            probs(hh, c)
            out.append(pass1(hh, c, mxs[hh]))
        return tuple(out)

    def both_twice(c2, mxs):
        return both(2 * c2 + 1, both(2 * c2, mxs))

    pairs = lax.shift_right_logical(common, 1)
    mxs = lax.fori_loop(0, pairs, both_twice,
                        tuple(jnp.full((1, nq * blk), -jnp.inf, F32) for _ in range(heads)))
    mxs = lax.fori_loop(2 * pairs, common, both, mxs)
    mxs = lax.fori_loop(common, n1, only1, mxs)
    lax.fori_loop(common, n2, only2, 0)
    for hh in range(heads):
        pv(hh, n2)
        m_refs[hh][...] = mxs[hh]

    @pl.when(i > 0)
    def _():
        for hh in range(heads):
            acc = acc_refs[hh][...]
            o = acc[0:HEAD_DIM, :] / acc[HEAD_DIM:HEAD_DIM + 1, :]
            for qb in range(nq):
                o_ref[qb * blk:(qb + 1) * blk, hh * HEAD_DIM:(hh + 1) * HEAD_DIM] = (
                    o[:, qb * blk:(qb + 1) * blk].T.astype(o_ref.dtype))


def _moba_attention(qkv, slopes, group=4, cols=2048, heads=2, nq=2):
    s = qkv.shape[1]
    blk = MOBA_BLOCK
    n_blk = s // blk
    n_chunk = n_blk // group
    vt_rows = HEAD_DIM + ONES_ROWS
    qt, vt, rb = pl.pallas_call(
        functools.partial(_attn_prep_kernel, n_blk=n_blk, group=group, cols=cols),
        grid=(N_HEADS,),
        in_specs=[pl.BlockSpec((None, 1, blk), lambda h: (h, 0, 0)),
                  pl.BlockSpec((None, s, HEAD_DIM), lambda h: (h, 0, 0)),
                  pl.BlockSpec((None, s, HEAD_DIM), lambda h: (N_HEADS + h, 0, 0)),
                  pl.BlockSpec((None, s, HEAD_DIM), lambda h: (2 * N_HEADS + h, 0, 0))],
        out_specs=[pl.BlockSpec((None, HEAD_DIM, s), lambda h: (h, 0, 0)),
                   pl.BlockSpec((None, n_chunk + 1, vt_rows, group * blk), lambda h: (h, 0, 0, 0)),
                   pl.BlockSpec((None, n_blk, s), lambda h: (h, 0, 0))],
        out_shape=[jax.ShapeDtypeStruct((N_HEADS, HEAD_DIM, s), BF16),
                   jax.ShapeDtypeStruct((N_HEADS, n_chunk + 1, vt_rows, group * blk), BF16),
                   jax.ShapeDtypeStruct((N_HEADS, n_blk, s), F32)],
        scratch_shapes=[pltpu.VMEM((n_blk, HEAD_DIM), F32)],
        compiler_params=_params(("parallel",)),
        name="moba_prep",
    )(slopes, qkv, qkv, qkv)
    n_tiles = n_blk // nq
    last = n_tiles - 1
    qw = nq * blk
    hw = heads * HEAD_DIM
    k_blk0 = N_HEADS // heads
    once = pl.Buffered(1)
    per_head = lambda shape: [pltpu.VMEM(shape, F32) for _ in range(heads)]
    return pl.pallas_call(
        functools.partial(_attn_kernel, group=group, n_blk=n_blk, heads=heads, nq=nq),
        grid=(N_HEADS // heads, n_tiles + 1),
        in_specs=[pl.BlockSpec((heads, 1, blk), lambda h, i: (h, 0, 0)),
                  pl.BlockSpec((heads, HEAD_DIM, qw), lambda h, i: (h, 0, jnp.minimum(i, last))),
                  pl.BlockSpec((heads, s, HEAD_DIM), lambda h, i: (k_blk0 + h, 0, 0)),
                  pl.BlockSpec((heads, n_chunk + 1, vt_rows, group * blk), lambda h, i: (h, 0, 0, 0),
                               pipeline_mode=once),
                  pl.BlockSpec((heads, n_blk, qw), lambda h, i: (h, 0, jnp.minimum(i, last))),
                  pl.BlockSpec((heads, n_blk, qw), lambda h, i: (h, 0, jnp.maximum(i - 1, 0)))],
        out_specs=pl.BlockSpec((qw, hw), lambda h, i: (jnp.maximum(i - 1, 0), h)),
        out_shape=jax.ShapeDtypeStruct((s, N_HEADS * HEAD_DIM), BF16),
        scratch_shapes=(per_head((n_chunk, group * blk, qw))
                        + per_head((1, qw))
                        + per_head((2, blk, blk))
                        + per_head((vt_rows, qw))
                        + [pltpu.VMEM((group * blk, qw), BF16) for _ in range(heads)]),
        compiler_params=_params(("parallel", "arbitrary")),
        name="moba_attention",
    )(slopes, qt, qkv, vt, rb, rb)


def _lru_kernel(xr_ref, yr_ref, cw_ref, cb_ref, wa_ref, ba_ref, wx_ref, bx_ref, lam_ref, o_ref,
                xs_ref, xp_ref, hx_ref, a_ref, b_ref, hl_ref, ac_ref, h_ref, wab_ref, wxb_ref, *, ts, tc):
    t = pl.program_id(1)
    lane_tiles = tc // 128
    ng = ts // 8
    pitch = ng + 1
    lead = 8
    hist = CONV_WIDTH - 1
    row = lax.broadcasted_iota(jnp.int32, (8, tc), 0)

    @pl.when(t == 0)
    def _():
        hx_ref[...] = jnp.zeros_like(hx_ref)
        h_ref[...] = jnp.zeros_like(h_ref)
        wab_ref[...] = wa_ref[...].astype(BF16)
        wxb_ref[...] = wx_ref[...].astype(BF16)

    x = xr_ref[...]
    for lt in range(lane_tiles):
        for sg in range(8):
            xs_ref[lt, sg * pitch:sg * pitch + ng, :] = x[sg * ng:(sg + 1) * ng, lt * 128:(lt + 1) * 128]

    def gather(g, c):
        r0 = pl.multiple_of((lead + g) * 8, 8)
        xp_ref[pl.ds(r0, 8), :] = jnp.concatenate(
            [xs_ref[lt, pl.ds(g, 8, stride=pitch), :] for lt in range(lane_tiles)], axis=1)
        return c

    lax.fori_loop(0, ng, gather, 0, unroll=8)
    for k in range(1, hist + 1):
        tail = xp_ref[(lead + ng - k) * 8:(lead + ng - k + 1) * 8, :]
        before_tile = jnp.broadcast_to(hx_ref[k - 1][7:8, :], (8, tc))
        xp_ref[(lead - k) * 8:(lead - k + 1) * 8, :] = jnp.where(row == 0, before_tile, pltpu.roll(tail, 1, 0))
        hx_ref[k - 1] = tail

    cw = cw_ref[...]
    u = cb_ref[...]
    for tap in range(CONV_WIDTH):
        off = (lead - hist + tap) * 8
        u = u + xp_ref[off:off + ts, :] * cw[tap:tap + 1, :]

    ub = u.astype(BF16)
    ga, gx = [], []
    for n in range(tc // LRU_BLOCK_W):
        un = ub[:, n * LRU_BLOCK_W:(n + 1) * LRU_BLOCK_W]
        ga.append(jnp.dot(un, wab_ref[n], preferred_element_type=F32))
        gx.append(jnp.dot(un, wxb_ref[n], preferred_element_type=F32))
    r = jax.nn.sigmoid(jnp.concatenate(ga, axis=1) + ba_ref[...])
    ig = jax.nn.sigmoid(jnp.concatenate(gx, axis=1) + bx_ref[...])
    log_a = -LRU_C * r * jax.nn.softplus(-lam_ref[...])
    a = jnp.exp(log_a)
    a_ref[...] = a
    one_minus_a2 = -jnp.tanh(log_a) * (a * a + 1.0)
    mult = jnp.where(one_minus_a2 == 0.0, 0.0, one_minus_a2 * lax.rsqrt(one_minus_a2))
    b_ref[...] = mult * ig * u

    half = ng // 2

    def local(g, carry):
        out = []
        for hf in range(2):
            hloc, acum = carry[hf]
            r0 = pl.multiple_of((g + hf * half) * 8, 8)
            av = a_ref[pl.ds(r0, 8), :]
            hloc = av * hloc + b_ref[pl.ds(r0, 8), :]
            acum = av * acum
            hl_ref[pl.ds(r0, 8), :] = hloc
            ac_ref[pl.ds(r0, 8), :] = acum
            out.append((hloc, acum))
        return tuple(out)

    start = (jnp.zeros((8, tc), F32), jnp.ones((8, tc), F32))
    (q0, p0), (q1, p1) = lax.fori_loop(0, half, local, (start, start), unroll=8)
    p = p1 * p0
    q = p1 * q0 + q1
    for d in (1, 2, 4):
        keep = row >= d
        p_sh = pltpu.roll(p, d, 0)
        q_sh = pltpu.roll(q, d, 0)
        q = jnp.where(keep, p * q_sh + q, q)
        p = jnp.where(keep, p * p_sh, p)
    h_in = h_ref[...]
    after = p * h_in + q
    enter = jnp.where(row == 0, h_in, pltpu.roll(after, 1, 0))
    h_ref[...] = jnp.broadcast_to(after[7:8, :], (8, tc))

    enter_half = (enter, p0 * enter + q0)

    def scatter(g, c):
        for hf in range(2):
            gg = g + hf * half
            r0 = pl.multiple_of(gg * 8, 8)
            hv = hl_ref[pl.ds(r0, 8), :] + ac_ref[pl.ds(r0, 8), :] * enter_half[hf]
            for lt in range(lane_tiles):
                xs_ref[lt, pl.ds(gg, 8, stride=pitch), :] = hv[:, lt * 128:(lt + 1) * 128]
        return c

    lax.fori_loop(0, half, scatter, 0, unroll=4)
    h_all = jnp.concatenate(
        [jnp.concatenate([xs_ref[lt, sg * pitch:sg * pitch + ng, :] for sg in range(8)], axis=0)
         for lt in range(lane_tiles)], axis=1)
    o_ref[...] = (h_all * jax.nn.gelu(yr_ref[...])).astype(o_ref.dtype)


def _rglru(rest, conv_w, conv_b, w_rg_a, b_rg_a, w_rg_x, b_rg_x, lru_lambda, width, ts=512, tc=512):
    s = rest.shape[0]
    nct = width // tc
    nb = tc // LRU_BLOCK_W
    vec = lambda v: v.reshape(1, width)
    vspec = pl.BlockSpec((1, tc), lambda c, t: (0, c))
    wspec = pl.BlockSpec((nb, LRU_BLOCK_W, LRU_BLOCK_W), lambda c, t: (c, 0, 0))
    return pl.pallas_call(
        functools.partial(_lru_kernel, ts=ts, tc=tc),
        grid=(nct, s // ts),
        in_specs=[pl.BlockSpec((ts, tc), lambda c, t: (t, c)),
                  pl.BlockSpec((ts, tc), lambda c, t: (t, nct + c)),
                  pl.BlockSpec((CONV_WIDTH, tc), lambda c, t: (0, c)),
                  vspec, wspec, vspec, wspec, vspec, vspec],
        out_specs=pl.BlockSpec((ts, tc), lambda c, t: (t, c)),
        out_shape=jax.ShapeDtypeStruct((s, width), BF16),
        scratch_shapes=[pltpu.VMEM((tc // 128, ts + 8, 128), F32),
                        pltpu.VMEM((ts + 64, tc), F32),
                        pltpu.VMEM((CONV_WIDTH - 1, 8, tc), F32),
                        pltpu.VMEM((ts, tc), F32),
                        pltpu.VMEM((ts, tc), F32),
                        pltpu.VMEM((ts, tc), F32),
                        pltpu.VMEM((ts, tc), F32),
                        pltpu.VMEM((8, tc), F32),
                        pltpu.VMEM((nb, LRU_BLOCK_W, LRU_BLOCK_W), BF16),
                        pltpu.VMEM((nb, LRU_BLOCK_W, LRU_BLOCK_W), BF16)],
        compiler_params=_params(("parallel", "arbitrary")),
        name="rglru",
    )(rest, rest, conv_w, vec(conv_b), w_rg_a, vec(b_rg_a), w_rg_x, vec(b_rg_x), vec(lru_lambda))


def _merge_kernel(att_ref, lru_ref, wa_ref, wl_ref, ga_ref, gl_ref, o_ref):
    bm = o_ref.shape[0]
    sub = bm // 2
    for r in range(2):
        rows = slice(r * sub, (r + 1) * sub)
        pa = jnp.dot(att_ref[rows, :], wa_ref[...], preferred_element_type=F32)
        plru = jnp.dot(lru_ref[rows, :], wl_ref[...], preferred_element_type=F32)
        o_ref[rows, :] = (jax.nn.sigmoid(ga_ref[rows, :]) * pa
                          + jax.nn.sigmoid(gl_ref[rows, :]) * plru).astype(o_ref.dtype)


def _merge(att, lru, w_att_bf16, w_lru_bf16, rest, gate_col0, bm=512):
    m, k = att.shape
    n = w_att_bf16.shape[1]
    ga0 = gate_col0 // n
    once = pl.Buffered(1)
    return pl.pallas_call(
        _merge_kernel,
        grid=(m // bm,),
        in_specs=[pl.BlockSpec((bm, k), lambda i: (i, 0)),
                  pl.BlockSpec((bm, k), lambda i: (i, 0)),
                  pl.BlockSpec((k, n), lambda i: (0, 0), pipeline_mode=once),
                  pl.BlockSpec((k, n), lambda i: (0, 0), pipeline_mode=once),
                  pl.BlockSpec((bm, n), lambda i: (i, ga0)),
                  pl.BlockSpec((bm, n), lambda i: (i, ga0 + 1))],
        out_specs=pl.BlockSpec((bm, n), lambda i: (i, 0)),
        out_shape=jax.ShapeDtypeStruct((m, n), BF16),
        compiler_params=_params(("parallel",)),
        name="merge",
    )(att, lru, w_att_bf16, w_lru_bf16, rest, rest)


def _outproj_kernel(a_ref, w_ref, x_ref, nw_ref, x1_ref, h2_ref):
    x1 = x_ref[...] + jnp.dot(a_ref[...], w_ref[...], preferred_element_type=F32)
    x1_ref[...] = x1
    y = x1 * lax.rsqrt(jnp.mean(x1 * x1, axis=-1, keepdims=True) + EPS)
    h2_ref[...] = (y * nw_ref[...]).astype(h2_ref.dtype)


def _outproj(merged, w_out_bf16, x, norm2_w, bm=512):
    m, k = merged.shape
    d = w_out_bf16.shape[1]
    return pl.pallas_call(
        _outproj_kernel,
        grid=(m // bm,),
        in_specs=[pl.BlockSpec((bm, k), lambda i: (i, 0)),
                  pl.BlockSpec((k, d), lambda i: (0, 0), pipeline_mode=pl.Buffered(1)),
                  pl.BlockSpec((bm, d), lambda i: (i, 0)),
                  pl.BlockSpec((1, d), lambda i: (0, 0))],
        out_specs=[pl.BlockSpec((bm, d), lambda i: (i, 0)),
                   pl.BlockSpec((bm, d), lambda i: (i, 0))],
        out_shape=[jax.ShapeDtypeStruct((m, d), F32), jax.ShapeDtypeStruct((m, d), BF16)],
        compiler_params=_params(("parallel",)),
        name="outproj",
    )(merged, w_out_bf16, x, norm2_w.reshape(1, d))


def _ffn_up_kernel(h_ref, wg_ref, wu_ref, wd_ref, o_ref, wdb_ref, wgb_ref, wub_ref):
    _load_weight(wg_ref, wgb_ref)
    _load_weight(wu_ref, wub_ref)
    _load_weight(wd_ref, wdb_ref)
    splits = 4
    sub = o_ref.shape[0] // splits
    for r in range(splits):
        rows = slice(r * sub, (r + 1) * sub)
        h = h_ref[rows, :]
        g = jnp.dot(h, wgb_ref[...], preferred_element_type=F32)
        u = jnp.dot(h, wub_ref[...], preferred_element_type=F32)
        o_ref[rows, :] = (jax.nn.silu(g) * u).astype(o_ref.dtype)


def _ffn_up(h2, w_gate, w_up, w_down, bm=2048, bn=512):
    m, k = h2.shape
    n = w_gate.shape[1]
    d = w_down.shape[1]
    return pl.pallas_call(
        _ffn_up_kernel,
        grid=(n // bn, m // bm),
        in_specs=[pl.BlockSpec((bm, k), lambda j, i: (i, 0)),
                  pl.BlockSpec((k, bn), lambda j, i: (0, j)),
                  pl.BlockSpec((k, bn), lambda j, i: (0, j)),
                  pl.BlockSpec((bn, d), lambda j, i: (j, 0))],
        out_specs=[pl.BlockSpec((bm, bn), lambda j, i: (i, j)),
                   pl.BlockSpec((bn, d), lambda j, i: (j, 0))],
        out_shape=[jax.ShapeDtypeStruct((m, n), BF16), jax.ShapeDtypeStruct((n, d), BF16)],
        scratch_shapes=[pltpu.VMEM((k, bn), BF16), pltpu.VMEM((k, bn), BF16)],
        compiler_params=_params(("parallel", "arbitrary")),
        name="ffn_up",
    )(h2, w_gate, w_up, w_down)


def _ffn_down_kernel(a_ref, w_ref, x_ref, o_ref):
    o_ref[...] = x_ref[...] + jnp.dot(a_ref[...], w_ref[...], preferred_element_type=F32)


def _ffn_down(act, w_down_bf16, x1, bm=512, bn=1024):
    m, k = act.shape
    n = w_down_bf16.shape[1]
    return pl.pallas_call(
        _ffn_down_kernel,
        grid=(n // bn, m // bm),
        in_specs=[pl.BlockSpec((bm, k), lambda j, i: (i, 0)),
                  pl.BlockSpec((k, bn), lambda j, i: (0, j)),
                  pl.BlockSpec((bm, bn), lambda j, i: (i, j))],
        out_specs=pl.BlockSpec((bm, bn), lambda j, i: (i, j)),
        out_shape=jax.ShapeDtypeStruct((m, n), F32),
        compiler_params=_params(("parallel", "arbitrary")),
        name="ffn_down",
    )(act, w_down_bf16, x1)


def _layer(x, norm1_w, w_in, q_norm_w, k_norm_w, conv_w, conv_b, w_rg_a, b_rg_a, w_rg_x, b_rg_x,
           lru_lambda, w_proj_attn, w_proj_lru, w_out, norm2_w, w_ffn_gate, w_ffn_up, w_ffn_down):
    d = x.shape[1]
    att_w = N_HEADS * HEAD_DIM
    lru_w = w_proj_lru.shape[0]

    h = _rmsnorm(x, norm1_w)
    qkv = _qkv_proj(h, w_in, q_norm_w, k_norm_w, 3 * att_w)
    rest, w_att_b, w_lru_b, w_out_b = _rest_proj(
        h, w_in, 3 * att_w, 2 * lru_w + 2 * d, (w_proj_attn, w_proj_lru, w_out))

    head = jnp.arange(1, N_HEADS + 1, dtype=F32)
    slopes = jnp.broadcast_to(jnp.exp2(-8.0 * head / N_HEADS)[:, None, None], (N_HEADS, 1, MOBA_BLOCK))
    att = _moba_attention(qkv, slopes)

    lru = _rglru(rest, conv_w, conv_b, w_rg_a, b_rg_a, w_rg_x, b_rg_x, lru_lambda, lru_w)

    merged = _merge(att, lru, w_att_b, w_lru_b, rest, 2 * lru_w)
    x1, h2 = _outproj(merged, w_out_b, x, norm2_w)
    act, w_down_bf16 = _ffn_up(h2, w_ffn_gate, w_ffn_up, w_ffn_down)
    return _ffn_down(act, w_down_bf16, x1)


def kernel(x, norm1_w, w_in, q_norm_w, k_norm_w, conv_w, conv_b, w_rg_a, b_rg_a, w_rg_x, b_rg_x,
           lru_lambda, w_proj_attn, w_proj_lru, w_out, norm2_w, w_ffn_gate, w_ffn_up, w_ffn_down):
    b, s, d = x.shape
    assert b == 1, "kernel handles the batch-1 prefill shape"
    y = x.reshape(s, d)
    for layer in range(norm1_w.shape[0]):
        y = _layer(y, norm1_w[layer], w_in[layer], q_norm_w[layer], k_norm_w[layer], conv_w[layer],
                   conv_b[layer], w_rg_a[layer], b_rg_a[layer], w_rg_x[layer], b_rg_x[layer],
                   lru_lambda[layer], w_proj_attn[layer], w_proj_lru[layer], w_out[layer],
                   norm2_w[layer], w_ffn_gate[layer], w_ffn_up[layer], w_ffn_down[layer])
    return y.reshape(b, s, d)
```

```python
import functools

import jax
import jax.numpy as jnp
from jax import lax
from jax.experimental import pallas as pl
from jax.experimental.pallas import tpu as pltpu

F32 = jnp.float32
BF16 = jnp.bfloat16

N_HEADS = 16
HEAD_DIM = 128
MOBA_BLOCK = 256
MOBA_TOPK = 3
LRU_BLOCK_W = 128
CONV_WIDTH = 4
LRU_C = 8.0
EPS = 1e-6
NEG_INF = -1e30
LOG2E = 1.4426950408889634
ONES_ROWS = 16

V7X_VMEM_BYTES = 64 * 1024 * 1024
VMEM_LIMIT = 56 * 1024 * 1024


def _params(semantics):
    return pltpu.CompilerParams(dimension_semantics=semantics, vmem_limit_bytes=VMEM_LIMIT)


def _rmsnorm_kernel(x_ref, w_ref, o_ref):
    x = x_ref[...]
    y = x * lax.rsqrt(jnp.mean(x * x, axis=-1, keepdims=True) + EPS)
    o_ref[...] = (y * w_ref[...]).astype(o_ref.dtype)


def _rmsnorm(x, w, tm=512):
    m, d = x.shape
    return pl.pallas_call(
        _rmsnorm_kernel,
        grid=(m // tm,),
        in_specs=[pl.BlockSpec((tm, d), lambda i: (i, 0)),
                  pl.BlockSpec((1, d), lambda i: (0, 0))],
        out_specs=pl.BlockSpec((tm, d), lambda i: (i, 0)),
        out_shape=jax.ShapeDtypeStruct((m, d), BF16),
        compiler_params=_params(("parallel",)),
        name="rmsnorm1",
    )(x, w.reshape(1, d))


def _load_weight(w_ref, wb_ref):
    @pl.when(pl.program_id(1) == 0)
    def _():
        wb_ref[...] = w_ref[...].astype(BF16)


def _qkv_kernel(h_ref, w_ref, qw_ref, kw_ref, o_ref, wb_ref, *, qk_tiles, row_splits):
    j = pl.program_id(0)
    _load_weight(w_ref, wb_ref)
    n_heads_tile, bm, _ = o_ref.shape
    sub = bm // row_splits

    @pl.when(j < qk_tiles)
    def _():
        nw = jnp.where(j < qk_tiles // 2, qw_ref[...], kw_ref[...])
        for r in range(row_splits):
            rows = slice(r * sub, (r + 1) * sub)
            acc = jnp.dot(h_ref[rows, :], wb_ref[...], preferred_element_type=F32)
            for hh in range(n_heads_tile):
                a = acc[:, hh * HEAD_DIM:(hh + 1) * HEAD_DIM]
                y = a * lax.rsqrt(jnp.mean(a * a, axis=-1, keepdims=True) + EPS)
                o_ref[hh, rows, :] = (y * nw).astype(o_ref.dtype)

    @pl.when(j >= qk_tiles)
    def _():
        acc = jnp.dot(h_ref[...], wb_ref[...], preferred_element_type=F32)
        for hh in range(n_heads_tile):
            o_ref[hh] = acc[:, hh * HEAD_DIM:(hh + 1) * HEAD_DIM].astype(o_ref.dtype)


def _qkv_proj(h, w_in, q_norm_w, k_norm_w, n_cols, bm=1024, bn=1024):
    m, k = h.shape
    return pl.pallas_call(
        functools.partial(_qkv_kernel, qk_tiles=(2 * N_HEADS * HEAD_DIM) // bn, row_splits=4),
        grid=(n_cols // bn, m // bm),
        in_specs=[pl.BlockSpec((bm, k), lambda j, i: (i, 0)),
                  pl.BlockSpec((k, bn), lambda j, i: (0, j)),
                  pl.BlockSpec((1, HEAD_DIM), lambda j, i: (0, 0)),
                  pl.BlockSpec((1, HEAD_DIM), lambda j, i: (0, 0))],
        out_specs=pl.BlockSpec((bn // HEAD_DIM, bm, HEAD_DIM), lambda j, i: (j, i, 0)),
        out_shape=jax.ShapeDtypeStruct((n_cols // HEAD_DIM, m, HEAD_DIM), BF16),
        scratch_shapes=[pltpu.VMEM((k, bn), BF16)],
        compiler_params=_params(("parallel", "arbitrary")),
        name="qkv_proj",
    )(h, w_in, q_norm_w.reshape(1, HEAD_DIM), k_norm_w.reshape(1, HEAD_DIM))


def _rest_kernel(a_ref, w_ref, *refs, n_side):
    side_in, o_ref, side_out, wb_ref = refs[:n_side], refs[n_side], refs[n_side + 1:-1], refs[-1]
    _load_weight(w_ref, wb_ref)
    for s_ref, c_ref in zip(side_in, side_out):
        c_ref[...] = s_ref[...].astype(BF16)
    o_ref[...] = jnp.dot(a_ref[...], wb_ref[...], preferred_element_type=F32).astype(o_ref.dtype)


def _rest_proj(h, w_in, col0, n_cols, side_weights, bm=1024, bn=1024):
    m, k = h.shape
    jb = col0 // bn
    nj = n_cols // bn
    ni = m // bm
    slab = lambda w: pl.BlockSpec((w.shape[0] // (nj * ni), w.shape[1]), lambda j, i: (j * ni + i, 0))
    return pl.pallas_call(
        functools.partial(_rest_kernel, n_side=len(side_weights)),
        grid=(nj, m // bm),
        in_specs=[pl.BlockSpec((bm, k), lambda j, i: (i, 0)),
                  pl.BlockSpec((k, bn), lambda j, i: (0, jb + j))] + [slab(w) for w in side_weights],
        out_specs=[pl.BlockSpec((bm, bn), lambda j, i: (i, j))] + [slab(w) for w in side_weights],
        out_shape=[jax.ShapeDtypeStruct((m, n_cols), F32)]
                  + [jax.ShapeDtypeStruct(w.shape, BF16) for w in side_weights],
        scratch_shapes=[pltpu.VMEM((k, bn), BF16)],
        compiler_params=_params(("parallel", "arbitrary")),
        name="rest_proj",
    )(h, w_in, *side_weights)


def _attn_prep_kernel(slope_ref, q_ref, k_ref, v_ref, qt_ref, vt_ref, rb_ref, kmean_ref,
                      *, n_blk, group, cols):
    blk = MOBA_BLOCK
    for jb in range(n_blk):
        rows = slice(jb * blk, (jb + 1) * blk)
        qt_ref[:, rows] = q_ref[rows, :].astype(F32).T.astype(BF16)
        c, g = divmod(jb, group)
        vt_ref[c, 0:HEAD_DIM, g * blk:(g + 1) * blk] = v_ref[rows, :].astype(F32).T.astype(BF16)
        vt_ref[c, HEAD_DIM:, g * blk:(g + 1) * blk] = jnp.ones((ONES_ROWS, blk), BF16)
        kmean_ref[jb:jb + 1, :] = jnp.sum(k_ref[rows, :].astype(F32), axis=0, keepdims=True) * (1.0 / blk)
    vt_ref[n_blk // group] = jnp.zeros(vt_ref.shape[1:], BF16)
    kmean = kmean_ref[...].astype(BF16)
    slope = slope_ref[:, 0:1]
    row = lax.broadcasted_iota(jnp.int32, (n_blk, cols), 0)
    col = lax.broadcasted_iota(jnp.int32, (n_blk, cols), 1)
    for qc in range(qt_ref.shape[1] // cols):
        csl = slice(qc * cols, (qc + 1) * cols)
        gate = jnp.dot(kmean, qt_ref[:, csl], preferred_element_type=F32)
        qblk = lax.shift_right_logical(col + qc * cols, blk.bit_length() - 1)
        g = jnp.where(row < qblk, gate, NEG_INF)
        bias = jnp.full((n_blk, cols), NEG_INF, F32)
        for r in range(MOBA_TOPK):
            mx = jnp.max(g, axis=0, keepdims=True)
            idx = jnp.min(jnp.where(g == mx, row, n_blk), axis=0, keepdims=True)
            pick = row == idx
            bias = jnp.where(pick, jnp.where(qblk > r, 0.0, NEG_INF), bias)
            g = jnp.where(pick, -jnp.inf, g)
        past = bias - slope * ((qblk - row) * blk).astype(F32)
        rb = jnp.where(row == qblk, 0.0, jnp.where(row < qblk, past, NEG_INF))
        rb_ref[:, csl] = rb * LOG2E


def _attn_kernel(slope_ref, qt_ref, k_ref, vt_ref, rb_ref, rbp_ref, o_ref, *scratch,
                 group, n_blk, heads, nq):
    u_refs, m_refs, dmat_refs, acc_refs, p_refs = (scratch[n * heads:(n + 1) * heads] for n in range(5))
    blk = MOBA_BLOCK
    cb = group * blk
    lg = group.bit_length() - 1
    i = pl.program_id(1)

    @pl.when(i == 0)
    def _per_head_setup():
        kk = lax.broadcasted_iota(jnp.int32, (blk, blk), 0)
        qq = lax.broadcasted_iota(jnp.int32, (blk, blk), 1)
        for hh in range(heads):
            d = slope_ref[hh][:, 0:1] * (qq - kk).astype(F32) * LOG2E
            dmat_refs[hh][0] = d
            dmat_refs[hh][1] = jnp.where(kk <= qq, d, -NEG_INF)
            m_refs[hh][...] = jnp.zeros_like(m_refs[hh])
            p_refs[hh][...] = jnp.zeros_like(p_refs[hh])

    k2 = (HEAD_DIM ** -0.5) * LOG2E
    n1 = jnp.where(i < n_blk // nq, lax.shift_right_logical(nq * (i + 1) + group - 1, lg), 0)
    n2 = lax.shift_right_logical(nq * i + group - 1, lg)
    common = jnp.minimum(n1, n2)
    m_prev = [m_refs[hh][...] for hh in range(heads)]
    for hh in range(heads):
        acc_refs[hh][...] = jnp.zeros_like(acc_refs[hh])

    def pass1(hh, c, mx):
        r0 = pl.multiple_of(c * cb, cb)
        k_chunk = k_ref[hh, pl.ds(r0, cb), :]
        s = jnp.dot(k_chunk, qt_ref[hh], preferred_element_type=F32)
        for g in range(group):
            j = c * group + g
            rows = slice(g * blk, (g + 1) * blk)
            cands = []
            for qb in range(nq):
                cols = slice(qb * blk, (qb + 1) * blk)
                own = (j == nq * i + qb).astype(jnp.int32)
                u = s[rows, cols] * k2 - dmat_refs[hh][own]
                u_refs[hh][c, rows, cols] = u
                cands.append(jnp.max(u, axis=0, keepdims=True))
            mx = jnp.maximum(mx, jnp.concatenate(cands, axis=1) + rb_ref[hh, pl.ds(j, 1), :])
        return mx

    def probs(hh, c):
        for g in range(group):
            j = c * group + g
            rows = slice(g * blk, (g + 1) * blk)
            p = jnp.exp2(u_refs[hh][c, rows, :] + (rbp_ref[hh, pl.ds(j, 1), :] - m_prev[hh]))
            p_refs[hh][rows, :] = p.astype(BF16)

    def pv(hh, c):
        n_chunk = vt_ref.shape[1] - 1
        acc_refs[hh][...] += jnp.dot(vt_ref[hh, jnp.where(c == 0, n_chunk, c - 1)], p_refs[hh][...],
                                     preferred_element_type=F32)

    def only1(c, mxs):
        return tuple(pass1(hh, c, mxs[hh]) for hh in range(heads))

    def only2(c, carry):
        for hh in range(heads):
            pv(hh, c)
        for hh in range(heads):
            probs(hh, c)
        return carry

    def both(c, mxs):
        for hh in range(heads):
            pv(hh, c)
        out = []
        for hh in range(heads):
            probs(hh, c)
            out.append(pass1(hh, c, mxs[hh]))
        return tuple(out)

    def both_twice(c2, mxs):
        return both(2 * c2 + 1, both(2 * c2, mxs))

    pairs = lax.shift_right_logical(common, 1)
    mxs = lax.fori_loop(0, pairs, both_twice,
                        tuple(jnp.full((1, nq * blk), -jnp.inf, F32) for _ in range(heads)))
    mxs = lax.fori_loop(2 * pairs, common, both, mxs)
    mxs = lax.fori_loop(common, n1, only1, mxs)
    lax.fori_loop(common, n2, only2, 0)
    for hh in range(heads):
        pv(hh, n2)
        m_refs[hh][...] = mxs[hh]

    @pl.when(i > 0)
    def _():
        for hh in range(heads):
            acc = acc_refs[hh][...]
            o = acc[0:HEAD_DIM, :] / acc[HEAD_DIM:HEAD_DIM + 1, :]
            for qb in range(nq):
                o_ref[qb * blk:(qb + 1) * blk, hh * HEAD_DIM:(hh + 1) * HEAD_DIM] = (
                    o[:, qb * blk:(qb + 1) * blk].T.astype(o_ref.dtype))


def _moba_attention(qkv, slopes, group=4, cols=2048, heads=2, nq=2):
    s = qkv.shape[1]
    blk = MOBA_BLOCK
    n_blk = s // blk
    n_chunk = n_blk // group
    vt_rows = HEAD_DIM + ONES_ROWS
    qt, vt, rb = pl.pallas_call(
        functools.partial(_attn_prep_kernel, n_blk=n_blk, group=group, cols=cols),
        grid=(N_HEADS,),
        in_specs=[pl.BlockSpec((None, 1, blk), lambda h: (h, 0, 0)),
                  pl.BlockSpec((None, s, HEAD_DIM), lambda h: (h, 0, 0)),
                  pl.BlockSpec((None, s, HEAD_DIM), lambda h: (N_HEADS + h, 0, 0)),
                  pl.BlockSpec((None, s, HEAD_DIM), lambda h: (2 * N_HEADS + h, 0, 0))],
        out_specs=[pl.BlockSpec((None, HEAD_DIM, s), lambda h: (h, 0, 0)),
                   pl.BlockSpec((None, n_chunk + 1, vt_rows, group * blk), lambda h: (h, 0, 0, 0)),
                   pl.BlockSpec((None, n_blk, s), lambda h: (h, 0, 0))],
        out_shape=[jax.ShapeDtypeStruct((N_HEADS, HEAD_DIM, s), BF16),
                   jax.ShapeDtypeStruct((N_HEADS, n_chunk + 1, vt_rows, group * blk), BF16),
                   jax.ShapeDtypeStruct((N_HEADS, n_blk, s), F32)],
        scratch_shapes=[pltpu.VMEM((n_blk, HEAD_DIM), F32)],
        compiler_params=_params(("parallel",)),
        name="moba_prep",
    )(slopes, qkv, qkv, qkv)
    n_tiles = n_blk // nq
    last = n_tiles - 1
    qw = nq * blk
    hw = heads * HEAD_DIM
    k_blk0 = N_HEADS // heads
    once = pl.Buffered(1)
    per_head = lambda shape: [pltpu.VMEM(shape, F32) for _ in range(heads)]
    return pl.pallas_call(
        functools.partial(_attn_kernel, group=group, n_blk=n_blk, heads=heads, nq=nq),
        grid=(N_HEADS // heads, n_tiles + 1),
        in_specs=[pl.BlockSpec((heads, 1, blk), lambda h, i: (h, 0, 0)),
                  pl.BlockSpec((heads, HEAD_DIM, qw), lambda h, i: (h, 0, jnp.minimum(i, last))),
                  pl.BlockSpec((heads, s, HEAD_DIM), lambda h, i: (k_blk0 + h, 0, 0)),
                  pl.BlockSpec((heads, n_chunk + 1, vt_rows, group * blk), lambda h, i: (h, 0, 0, 0),
                               pipeline_mode=once),
                  pl.BlockSpec((heads, n_blk, qw), lambda h, i: (h, 0, jnp.minimum(i, last))),
                  pl.BlockSpec((heads, n_blk, qw), lambda h, i: (h, 0, jnp.maximum(i - 1, 0)))],
        out_specs=pl.BlockSpec((qw, hw), lambda h, i: (jnp.maximum(i - 1, 0), h)),
        out_shape=jax.ShapeDtypeStruct((s, N_HEADS * HEAD_DIM), BF16),
        scratch_shapes=(per_head((n_chunk, group * blk, qw))
                        + per_head((1, qw))
                        + per_head((2, blk, blk))
                        + per_head((vt_rows, qw))
                        + [pltpu.VMEM((group * blk, qw), BF16) for _ in range(heads)]),
        compiler_params=_params(("parallel", "arbitrary")),
        name="moba_attention",
    )(slopes, qt, qkv, vt, rb, rb)


def _lru_kernel(xr_ref, yr_ref, cw_ref, cb_ref, wa_ref, ba_ref, wx_ref, bx_ref, lam_ref, o_ref,
                xs_ref, xp_ref, hx_ref, a_ref, b_ref, hl_ref, ac_ref, h_ref, wab_ref, wxb_ref, *, ts, tc):
    t = pl.program_id(1)
    lane_tiles = tc // 128
    ng = ts // 8
    pitch = ng + 1
    lead = 8
    hist = CONV_WIDTH - 1
    row = lax.broadcasted_iota(jnp.int32, (8, tc), 0)

    @pl.when(t == 0)
    def _():
        hx_ref[...] = jnp.zeros_like(hx_ref)
        h_ref[...] = jnp.zeros_like(h_ref)
        wab_ref[...] = wa_ref[...].astype(BF16)
        wxb_ref[...] = wx_ref[...].astype(BF16)

    x = xr_ref[...]
    for lt in range(lane_tiles):
        for sg in range(8):
            xs_ref[lt, sg * pitch:sg * pitch + ng, :] = x[sg * ng:(sg + 1) * ng, lt * 128:(lt + 1) * 128]

    def gather(g, c):
        r0 = pl.multiple_of((lead + g) * 8, 8)
        xp_ref[pl.ds(r0, 8), :] = jnp.concatenate(
            [xs_ref[lt, pl.ds(g, 8, stride=pitch), :] for lt in range(lane_tiles)], axis=1)
        return c

    lax.fori_loop(0, ng, gather, 0, unroll=8)
    for k in range(1, hist + 1):
        tail = xp_ref[(lead + ng - k) * 8:(lead + ng - k + 1) * 8, :]
        before_tile = jnp.broadcast_to(hx_ref[k - 1][7:8, :], (8, tc))
        xp_ref[(lead - k) * 8:(lead - k + 1) * 8, :] = jnp.where(row == 0, before_tile, pltpu.roll(tail, 1, 0))
        hx_ref[k - 1] = tail

    cw = cw_ref[...]
    u = cb_ref[...]
    for tap in range(CONV_WIDTH):
        off = (lead - hist + tap) * 8
        u = u + xp_ref[off:off + ts, :] * cw[tap:tap + 1, :]

    ub = u.astype(BF16)
    ga, gx = [], []
    for n in range(tc // LRU_BLOCK_W):
        un = ub[:, n * LRU_BLOCK_W:(n + 1) * LRU_BLOCK_W]
        ga.append(jnp.dot(un, wab_ref[n], preferred_element_type=F32))
        gx.append(jnp.dot(un, wxb_ref[n], preferred_element_type=F32))
    r = jax.nn.sigmoid(jnp.concatenate(ga, axis=1) + ba_ref[...])
    ig = jax.nn.sigmoid(jnp.concatenate(gx, axis=1) + bx_ref[...])
    log_a = -LRU_C * r * jax.nn.softplus(-lam_ref[...])
    a = jnp.exp(log_a)
    a_ref[...] = a
    one_minus_a2 = -jnp.tanh(log_a) * (a * a + 1.0)
    mult = jnp.where(one_minus_a2 == 0.0, 0.0, one_minus_a2 * lax.rsqrt(one_minus_a2))
    b_ref[...] = mult * ig * u

    half = ng // 2

    def local(g, carry):
        out = []
        for hf in range(2):
            hloc, acum = carry[hf]
            r0 = pl.multiple_of((g + hf * half) * 8, 8)
            av = a_ref[pl.ds(r0, 8), :]
            hloc = av * hloc + b_ref[pl.ds(r0, 8), :]
            acum = av * acum
            hl_ref[pl.ds(r0, 8), :] = hloc
            ac_ref[pl.ds(r0, 8), :] = acum
            out.append((hloc, acum))
        return tuple(out)

    start = (jnp.zeros((8, tc), F32), jnp.ones((8, tc), F32))
    (q0, p0), (q1, p1) = lax.fori_loop(0, half, local, (start, start), unroll=8)
    p = p1 * p0
    q = p1 * q0 + q1
    for d in (1, 2, 4):
        keep = row >= d
        p_sh = pltpu.roll(p, d, 0)
        q_sh = pltpu.roll(q, d, 0)
        q = jnp.where(keep, p * q_sh + q, q)
        p = jnp.where(keep, p * p_sh, p)
    h_in = h_ref[...]
    after = p * h_in + q
    enter = jnp.where(row == 0, h_in, pltpu.roll(after, 1, 0))
    h_ref[...] = jnp.broadcast_to(after[7:8, :], (8, tc))

    enter_half = (enter, p0 * enter + q0)

    def scatter(g, c):
        for hf in range(2):
            gg = g + hf * half
            r0 = pl.multiple_of(gg * 8, 8)
            hv = hl_ref[pl.ds(r0, 8), :] + ac_ref[pl.ds(r0, 8), :] * enter_half[hf]
            for lt in range(lane_tiles):
                xs_ref[lt, pl.ds(gg, 8, stride=pitch), :] = hv[:, lt * 128:(lt + 1) * 128]
        return c

    lax.fori_loop(0, half, scatter, 0, unroll=4)
    h_all = jnp.concatenate(
        [jnp.concatenate([xs_ref[lt, sg * pitch:sg * pitch + ng, :] for sg in range(8)], axis=0)
         for lt in range(lane_tiles)], axis=1)
    o_ref[...] = (h_all * jax.nn.gelu(yr_ref[...])).astype(o_ref.dtype)


def _rglru(rest, conv_w, conv_b, w_rg_a, b_rg_a, w_rg_x, b_rg_x, lru_lambda, width, ts=512, tc=512):
    s = rest.shape[0]
    nct = width // tc
    nb = tc // LRU_BLOCK_W
    vec = lambda v: v.reshape(1, width)
    vspec = pl.BlockSpec((1, tc), lambda c, t: (0, c))
    wspec = pl.BlockSpec((nb, LRU_BLOCK_W, LRU_BLOCK_W), lambda c, t: (c, 0, 0))
    return pl.pallas_call(
        functools.partial(_lru_kernel, ts=ts, tc=tc),
        grid=(nct, s // ts),
        in_specs=[pl.BlockSpec((ts, tc), lambda c, t: (t, c)),
                  pl.BlockSpec((ts, tc), lambda c, t: (t, nct + c)),
                  pl.BlockSpec((CONV_WIDTH, tc), lambda c, t: (0, c)),
                  vspec, wspec, vspec, wspec, vspec, vspec],
        out_specs=pl.BlockSpec((ts, tc), lambda c, t: (t, c)),
        out_shape=jax.ShapeDtypeStruct((s, width), BF16),
        scratch_shapes=[pltpu.VMEM((tc // 128, ts + 8, 128), F32),
                        pltpu.VMEM((ts + 64, tc), F32),
                        pltpu.VMEM((CONV_WIDTH - 1, 8, tc), F32),
                        pltpu.VMEM((ts, tc), F32),
                        pltpu.VMEM((ts, tc), F32),
                        pltpu.VMEM((ts, tc), F32),
                        pltpu.VMEM((ts, tc), F32),
                        pltpu.VMEM((8, tc), F32),
                        pltpu.VMEM((nb, LRU_BLOCK_W, LRU_BLOCK_W), BF16),
                        pltpu.VMEM((nb, LRU_BLOCK_W, LRU_BLOCK_W), BF16)],
        compiler_params=_params(("parallel", "arbitrary")),
        name="rglru",
    )(rest, rest, conv_w, vec(conv_b), w_rg_a, vec(b_rg_a), w_rg_x, vec(b_rg_x), vec(lru_lambda))


def _merge_outproj_kernel(att_ref, lru_ref, wa_ref, wl_ref, ga_ref, gl_ref, wo_ref, x_ref, nw_ref,
                          x1_ref, h2_ref):
    pa = jnp.dot(att_ref[...], wa_ref[...], preferred_element_type=F32)
    plru = jnp.dot(lru_ref[...], wl_ref[...], preferred_element_type=F32)
    merged = (jax.nn.sigmoid(ga_ref[...]) * pa + jax.nn.sigmoid(gl_ref[...]) * plru).astype(BF16)
    x1 = x_ref[...] + jnp.dot(merged, wo_ref[...], preferred_element_type=F32)
    x1_ref[...] = x1
    y = x1 * lax.rsqrt(jnp.mean(x1 * x1, axis=-1, keepdims=True) + EPS)
    h2_ref[...] = (y * nw_ref[...]).astype(h2_ref.dtype)


def _merge_outproj(att, lru, w_att_bf16, w_lru_bf16, rest, gate_col0, w_out_bf16, x, norm2_w, bm=256):
    m, k = att.shape
    n = w_att_bf16.shape[1]
    d = w_out_bf16.shape[1]
    ga0 = gate_col0 // n
    once = pl.Buffered(1)
    row = lambda width: pl.BlockSpec((bm, width), lambda i: (i, 0))
    return pl.pallas_call(
        _merge_outproj_kernel,
        grid=(m // bm,),
        in_specs=[row(k), row(k),
                  pl.BlockSpec((k, n), lambda i: (0, 0), pipeline_mode=once),
                  pl.BlockSpec((k, n), lambda i: (0, 0), pipeline_mode=once),
                  pl.BlockSpec((bm, n), lambda i: (i, ga0)),
                  pl.BlockSpec((bm, n), lambda i: (i, ga0 + 1)),
                  pl.BlockSpec((n, d), lambda i: (0, 0), pipeline_mode=once),
                  row(d),
                  pl.BlockSpec((1, d), lambda i: (0, 0))],
        out_specs=[row(d), row(d)],
        out_shape=[jax.ShapeDtypeStruct((m, d), F32), jax.ShapeDtypeStruct((m, d), BF16)],
        compiler_params=_params(("parallel",)),
        name="merge_outproj",
    )(att, lru, w_att_bf16, w_lru_bf16, rest, rest, w_out_bf16, x, norm2_w.reshape(1, d))


def _ffn_up_kernel(h_ref, wg_ref, wu_ref, wd_ref, o_ref, wdb_ref, wgb_ref, wub_ref):
    _load_weight(wg_ref, wgb_ref)
    _load_weight(wu_ref, wub_ref)
    _load_weight(wd_ref, wdb_ref)
    splits = 4
    sub = o_ref.shape[0] // splits
    for r in range(splits):
        rows = slice(r * sub, (r + 1) * sub)
        h = h_ref[rows, :]
        g = jnp.dot(h, wgb_ref[...], preferred_element_type=F32)
        u = jnp.dot(h, wub_ref[...], preferred_element_type=F32)
        o_ref[rows, :] = (jax.nn.silu(g) * u).astype(o_ref.dtype)


def _ffn_up(h2, w_gate, w_up, w_down, bm=2048, bn=512):
    m, k = h2.shape
    n = w_gate.shape[1]
    d = w_down.shape[1]
    return pl.pallas_call(
        _ffn_up_kernel,
        grid=(n // bn, m // bm),
        in_specs=[pl.BlockSpec((bm, k), lambda j, i: (i, 0)),
                  pl.BlockSpec((k, bn), lambda j, i: (0, j)),
                  pl.BlockSpec((k, bn), lambda j, i: (0, j)),
                  pl.BlockSpec((bn, d), lambda j, i: (j, 0))],
        out_specs=[pl.BlockSpec((bm, bn), lambda j, i: (i, j)),
                   pl.BlockSpec((bn, d), lambda j, i: (j, 0))],
        out_shape=[jax.ShapeDtypeStruct((m, n), BF16), jax.ShapeDtypeStruct((n, d), BF16)],
        scratch_shapes=[pltpu.VMEM((k, bn), BF16), pltpu.VMEM((k, bn), BF16)],
        compiler_params=_params(("parallel", "arbitrary")),
        name="ffn_up",
    )(h2, w_gate, w_up, w_down)


def _ffn_down_kernel(a_ref, w_ref, x_ref, o_ref):
    o_ref[...] = x_ref[...] + jnp.dot(a_ref[...], w_ref[...], preferred_element_type=F32)


def _ffn_down(act, w_down_bf16, x1, bm=512, bn=1024):
    m, k = act.shape
    n = w_down_bf16.shape[1]
    return pl.pallas_call(
        _ffn_down_kernel,
        grid=(n // bn, m // bm),
        in_specs=[pl.BlockSpec((bm, k), lambda j, i: (i, 0)),
                  pl.BlockSpec((k, bn), lambda j, i: (0, j)),
                  pl.BlockSpec((bm, bn), lambda j, i: (i, j))],
        out_specs=pl.BlockSpec((bm, bn), lambda j, i: (i, j)),
        out_shape=jax.ShapeDtypeStruct((m, n), F32),
        compiler_params=_params(("parallel", "arbitrary")),
        name="ffn_down",
    )(act, w_down_bf16, x1)


def _layer(x, norm1_w, w_in, q_norm_w, k_norm_w, conv_w, conv_b, w_rg_a, b_rg_a, w_rg_x, b_rg_x,
           lru_lambda, w_proj_attn, w_proj_lru, w_out, norm2_w, w_ffn_gate, w_ffn_up, w_ffn_down):
    d = x.shape[1]
    att_w = N_HEADS * HEAD_DIM
    lru_w = w_proj_lru.shape[0]

    h = _rmsnorm(x, norm1_w)
    qkv = _qkv_proj(h, w_in, q_norm_w, k_norm_w, 3 * att_w)
    rest, w_att_b, w_lru_b, w_out_b = _rest_proj(
        h, w_in, 3 * att_w, 2 * lru_w + 2 * d, (w_proj_attn, w_proj_lru, w_out))

    head = jnp.arange(1, N_HEADS + 1, dtype=F32)
    slopes = jnp.broadcast_to(jnp.exp2(-8.0 * head / N_HEADS)[:, None, None], (N_HEADS, 1, MOBA_BLOCK))
    att = _moba_attention(qkv, slopes)

    lru = _rglru(rest, conv_w, conv_b, w_rg_a, b_rg_a, w_rg_x, b_rg_x, lru_lambda, lru_w)

    x1, h2 = _merge_outproj(att, lru, w_att_b, w_lru_b, rest, 2 * lru_w, w_out_b, x, norm2_w)
    act, w_down_bf16 = _ffn_up(h2, w_ffn_gate, w_ffn_up, w_ffn_down)
    return _ffn_down(act, w_down_bf16, x1)


def kernel(x, norm1_w, w_in, q_norm_w, k_norm_w, conv_w, conv_b, w_rg_a, b_rg_a, w_rg_x, b_rg_x,
           lru_lambda, w_proj_attn, w_proj_lru, w_out, norm2_w, w_ffn_gate, w_ffn_up, w_ffn_down):
    b, s, d = x.shape
    assert b == 1, "kernel handles the batch-1 prefill shape"
    y = x.reshape(s, d)
    for layer in range(norm1_w.shape[0]):
        y = _layer(y, norm1_w[layer], w_in[layer], q_norm_w[layer], k_norm_w[layer], conv_w[layer],
                   conv_b[layer], w_rg_a[layer], b_rg_a[layer], w_rg_x[layer], b_rg_x[layer],
                   lru_lambda[layer], w_proj_attn[layer], w_proj_lru[layer], w_out[layer],
                   norm2_w[layer], w_ffn_gate[layer], w_ffn_up[layer], w_ffn_down[layer])
    return y.reshape(b, s, d)
```
